```python
import jax, jax.numpy as jnp
from jax import lax
import numpy as np

D_MODEL = 1024
BATCH = 32
SEQ = 2048
DEPTH = 4

CHUNK = 64
QBLOCK = 128
HG_HEADS = 4
HG_DK = 128
HG_DV = 128
HG_WIDTH = HG_HEADS * HG_DK
SA_HEADS = 8
SA_DH = 64
SA_WIDTH = SA_HEADS * SA_DH
SA_LATENT = 128
IDX_HEADS = 4
IDX_DIM = 64
TOPK_MAX = 256
SB_HEADS = 8
SB_DH = 64
SB_WIDTH = SB_HEADS * SB_DH
N_BRANCH = 3
BRANCH_WIDTH = 512
D_FF = 2816
N_EXPERTS = 8
TOP_K_EXPERTS = 2
D_FF_EXPERT = 2816
N_DENSE = (DEPTH + 1) // 2
N_MOE = DEPTH // 2
PLE_DIM = 256
EPS = 1e-6
IN_SPLITS = (HG_WIDTH, HG_WIDTH, HG_HEADS * HG_DV, HG_HEADS * HG_DV,
             SA_WIDTH, SA_LATENT, IDX_HEADS * IDX_DIM, IDX_DIM, IDX_HEADS,
             SB_WIDTH, SB_WIDTH, SB_WIDTH, N_BRANCH * D_MODEL)
IN_WIDTH = sum(IN_SPLITS)

kernel_name = "hybrid_hgrn2_dsa_stickbreak_moe_block"

F32 = jnp.float32


def rmsnorm(x, g):
    xf = x.astype(F32)
    y = xf * lax.rsqrt(jnp.mean(xf * xf, axis=-1, keepdims=True) + EPS)
    return (y * g.astype(F32)).astype(x.dtype)


def hgrn2_mixer(q_in, f_in, i_in, g_in, lb, onorm):
    B, S, _ = q_in.shape
    nC = S // CHUNK
    lbf = lb.astype(F32)
    fpre = f_in.astype(F32)
    q = jax.nn.silu(q_in.astype(F32))
    log_f = jnp.log(lbf + (1.0 - lbf) * jax.nn.sigmoid(fpre))
    k = (1.0 - lbf) * jax.nn.sigmoid(-fpre)
    v = i_in.astype(F32)

    def to_chunks(t, d):
        return t.reshape(B, nC, CHUNK, HG_HEADS, d).transpose(1, 0, 3, 2, 4)

    causal = jnp.tril(jnp.ones((CHUNK, CHUNK), dtype=bool))[None, None, :, :, None]

    def step(state, inp):
        qc, kc, vc, lfc = inp
        b = jnp.cumsum(lfc, axis=2)
        diff = b[:, :, :, None, :] - b[:, :, None, :, :]
        decay = jnp.exp(jnp.where(causal, diff, -jnp.inf))
        scores = jnp.einsum('bhtd,bhsd,bhtsd->bhts', qc, kc, decay)
        o = jnp.einsum('bhts,bhsv->bhtv', scores, vc) + \
            jnp.einsum('bhtd,bhdv->bhtv', qc * jnp.exp(b), state)
        b_last = b[:, :, -1:, :]
        state = jnp.exp(b_last[:, :, 0, :])[..., None] * state + \
            jnp.einsum('bhsd,bhsv->bhdv', kc * jnp.exp(b_last - b), vc)
        return state, o

    state0 = jnp.zeros((B, HG_HEADS, HG_DK, HG_DV), F32)
    _, o = lax.scan(step, state0, (to_chunks(q, HG_DK), to_chunks(k, HG_DK),
                                   to_chunks(v, HG_DV), to_chunks(log_f, HG_DK)))
    o = o.transpose(1, 0, 3, 2, 4).reshape(B, S, HG_HEADS, HG_DV)
    o = rmsnorm(o, onorm).reshape(B, S, HG_HEADS * HG_DV)
    o = o * jax.nn.silu(g_in.astype(F32))
    return o.astype(q_in.dtype)


def dsa_mixer(q_in, c_in, iq_in, ik_in, iw_in, cnorm, wk, wv):
    B, S, _ = q_in.shape
    nb = S // QBLOCK
    topk = min(TOPK_MAX, S // 4)
    c = rmsnorm(c_in, cnorm)
    k = c @ wk
    v = c @ wv
    q = q_in.reshape(B, S, SA_HEADS, SA_DH)
    iq = iq_in.reshape(B, S, IDX_HEADS, IDX_DIM)
    ik = ik_in.astype(F32)
    iw = iw_in * (IDX_HEADS ** -0.5)
    kchunk = jnp.arange(S) // CHUNK
    gather = jax.vmap(lambda t, idx: t[idx])

    def blocks(t):
        return t.reshape((B, nb, QBLOCK) + t.shape[2:]).swapaxes(0, 1)

    def attend(blk):
        qb, iqb, iwb, qchunk = blk
        isc = jnp.einsum('bthd,bsd->bths', iqb.astype(F32), ik)
        isc = jnp.einsum('bths,bth->bts', jax.nn.relu(isc), iwb.astype(F32))
        adm = kchunk[None, :] <= qchunk[:, None]
        isc = jnp.where(adm[None], isc, -jnp.inf)
        _, sel = lax.top_k(isc, topk)
        valid = kchunk[sel] <= qchunk[None, :, None]
        k_sel = gather(k, sel).astype(F32)
        v_sel = gather(v, sel).astype(F32)
        logits = jnp.einsum('bthd,btkd->bthk', qb.astype(F32), k_sel) * (SA_DH ** -0.5)
        logits = jnp.where(valid[:, :, None, :], logits, -jnp.inf)
        w = jax.nn.softmax(logits, axis=-1)
        return jnp.einsum('bthk,btkd->bthd', w, v_sel)

    qchunks = (jnp.arange(S) // CHUNK).reshape(nb, QBLOCK)
    o = lax.map(attend, (blocks(q), blocks(iq), blocks(iw), qchunks))
    o = o.swapaxes(0, 1).reshape(B, S, SA_WIDTH)
    return o.astype(q_in.dtype)


def stick_breaking_mixer(q_in, k_in, v_in):
    B, S, _ = q_in.shape
    q = q_in.reshape(B, S, SB_HEADS, SB_DH).astype(F32)
    k = k_in.reshape(B, S, SB_HEADS, SB_DH).astype(F32)
    v = v_in.reshape(B, S, SB_HEADS, SB_DH).astype(F32)
    scale = SB_DH ** -0.5
    outs = []
    for j in range(S // QBLOCK):
        start = j * QBLOCK
        end = start + QBLOCK
        z = jnp.einsum('bthd,bshd->bhts', q[:, start:end], k[:, :end]) * scale
        t_pos = start + jnp.arange(QBLOCK)
        s_pos = jnp.arange(end)
        strict = s_pos[None, :] < t_pos[:, None]
        log_keep = jnp.where(strict, jax.nn.log_sigmoid(-z), 0.0)
        after = lax.cumsum(log_keep, axis=3, reverse=True) - log_keep
        A = jnp.where(strict, jnp.exp(jax.nn.log_sigmoid(z) + after), 0.0)
        outs.append(jnp.einsum('bhts,bshd->bthd', A, v[:, :end]))
    o = jnp.concatenate(outs, axis=1).reshape(B, S, SB_WIDTH)
    return o.astype(q_in.dtype)


def swiglu(v, wg, wu, wd):
    return (jax.nn.silu(v @ wg) * (v @ wu)) @ wd


def moe_swiglu(v, router, wg, wu, wd):
    logits = (v @ router).astype(F32)
    top_val, top_idx = lax.top_k(logits, TOP_K_EXPERTS)
    top_w = jax.nn.softmax(top_val, axis=-1)
    combine = jnp.sum(jax.nn.one_hot(top_idx, N_EXPERTS, dtype=F32) * top_w[..., None], axis=-2)
    combine = combine.astype(v.dtype)
    y = jnp.zeros_like(v)
    for e in range(N_EXPERTS):
        y = y + combine[..., e:e + 1] * swiglu(v, wg[e], wu[e], wd[e])
    return y


def setup_inputs(seed: int = 0) -> dict:
    key = jax.random.key(seed)
    ks = jax.random.split(key, 24)
    D = D_MODEL

    def nrm(k, shape, fan_in):
        return jax.random.normal(k, shape, F32) * (fan_in ** -0.5)

    def gain(k, shape):
        return 1.0 + 0.02 * jax.random.normal(k, shape, F32)

    return {
        "x": jax.random.normal(ks[0], (BATCH, SEQ, D), F32),
        "p": jax.random.normal(ks[1], (DEPTH, BATCH, SEQ, PLE_DIM), F32),
        "lb_param": 0.1 * jax.random.normal(ks[2], (DEPTH, HG_WIDTH), F32),
        "norm_mix": gain(ks[3], (DEPTH, D)),
        "w_in": nrm(ks[4], (DEPTH, D, IN_WIDTH), D),
        "hg_onorm": gain(ks[5], (DEPTH, HG_DV)),
        "sa_cnorm": gain(ks[6], (DEPTH, SA_LATENT)),
        "sa_wk": nrm(ks[7], (DEPTH, SA_LATENT, SA_DH), SA_LATENT),
        "sa_wv": nrm(ks[8], (DEPTH, SA_LATENT, SA_DH), SA_LATENT),
        "w_branch": nrm(ks[9], (DEPTH, N_BRANCH, BRANCH_WIDTH, D), BRANCH_WIDTH),
        "w_out": nrm(ks[10], (DEPTH, D, D), D),
        "norm_ffn": gain(ks[11], (DEPTH, D)),
        "dense_wg": nrm(ks[12], (N_DENSE, D, D_FF), D),
        "dense_wu": nrm(ks[13], (N_DENSE, D, D_FF), D),
        "dense_wd": nrm(ks[14], (N_DENSE, D_FF, D), D_FF),
        "moe_router": nrm(ks[15], (N_MOE, D, N_EXPERTS), D),
        "moe_wg": nrm(ks[16], (N_MOE, N_EXPERTS, D, D_FF_EXPERT), D),
        "moe_wu": nrm(ks[17], (N_MOE, N_EXPERTS, D, D_FF_EXPERT), D),
        "moe_wd": nrm(ks[18], (N_MOE, N_EXPERTS, D_FF_EXPERT, D), D_FF_EXPERT),
        "norm_ple": gain(ks[19], (DEPTH, D)),
        "ple_gate": nrm(ks[20], (DEPTH, D, D), D),
        "ple_proj": nrm(ks[21], (DEPTH, PLE_DIM, D), PLE_DIM),
        "norm_final": gain(ks[22], (D,)),
    }


def reference(x, p, lb_param, norm_mix, w_in, hg_onorm, sa_cnorm, sa_wk, sa_wv, w_branch,
              w_out, norm_ffn, dense_wg, dense_wu, dense_wd, moe_router, moe_wg, moe_wu,
              moe_wd, norm_ple, ple_gate, ple_proj, norm_final):
    B, S, D = x.shape
    lb_sm = jax.nn.softmax(lb_param.astype(F32), axis=0)
    lb_all = jnp.cumsum(lb_sm, axis=0) - lb_sm[0:1]
    split_points = np.cumsum(np.array(IN_SPLITS))[:-1]
    h = x
    for i in range(DEPTH):
        u = rmsnorm(h, norm_mix[i])
        z = u @ w_in[i]
        (hq, hf, hi, hg, sq, sc, iq, ik, iw, bq, bk, bv, gz) = jnp.split(z, split_points, axis=-1)
        a = hgrn2_mixer(hq, hf, hi, hg, lb_all[i], hg_onorm[i])
        b = dsa_mixer(sq, sc, iq, ik, iw, sa_cnorm[i], sa_wk[i], sa_wv[i])
        c = stick_breaking_mixer(bq, bk, bv)
        branches = jnp.stack([a, b, c], axis=2)
        proj = jnp.einsum('bsnw,nwd->bsnd', branches, w_branch[i])
        gates = jax.nn.sigmoid(gz.astype(F32)).reshape(B, S, N_BRANCH, D).astype(proj.dtype)
        merged = jnp.sum(gates * proj, axis=2)
        h = h + merged @ w_out[i]
        v = rmsnorm(h, norm_ffn[i])
        if i % 2 == 0:
            j = i // 2
            h = h + swiglu(v, dense_wg[j], dense_wu[j], dense_wd[j])
        else:
            j = i // 2
            h = h + moe_swiglu(v, moe_router[j], moe_wg[j], moe_wu[j], moe_wd[j])
        ple_g = jax.nn.sigmoid((rmsnorm(h, norm_ple[i]) @ ple_gate[i]).astype(F32)).astype(h.dtype)
        h = h + ple_g * (p[i] @ ple_proj[i])
    return rmsnorm(h, norm_final)
```

```python
import functools

import numpy as np
import jax
import jax.numpy as jnp
from jax import lax
from jax.experimental import pallas as pl
from jax.experimental.pallas import tpu as pltpu

F32 = jnp.float32
BF16 = jnp.bfloat16
I32 = jnp.int32

D_MODEL = 1024
CHUNK = 64
QBLOCK = 128
HG_HEADS = 4
HG_DK = 128
HG_WIDTH = HG_HEADS * HG_DK
SA_HEADS = 8
SA_DH = 64
SA_WIDTH = SA_HEADS * SA_DH
SA_LATENT = 128
IDX_HEADS = 4
IDX_DIM = 64
TOPK_MAX = 256
SB_HEADS = 8
SB_DH = 64
SB_WIDTH = SB_HEADS * SB_DH
N_BRANCH = 3
BRANCH_WIDTH = 512
N_EXPERTS = 8
PLE_DIM = 256
EPS = 1e-6

LANES = 128

GZ_OFF = 0
HG_OFF = 3 * D_MODEL
SQ_OFF = HG_OFF + 4 * HG_WIDTH
IQ_OFF = SQ_OFF + SA_WIDTH
SC_OFF = IQ_OFF + IDX_HEADS * IDX_DIM
IKW_OFF = SC_OFF + SA_LATENT
SB_OFF = IKW_OFF + LANES
ZW = SB_OFF + 3 * SB_WIDTH

VMEM_LIMIT = 56 * 1024 * 1024

NEG_BIG = -1e30
KEY_NEG_INF = -2139095041


def _cparams(sem):
    return pltpu.CompilerParams(dimension_semantics=sem, vmem_limit_bytes=VMEM_LIMIT)


def _sigmoid(x):
    e = jnp.exp(-jnp.abs(x))
    inv = 1.0 / (1.0 + e)
    return jnp.where(x >= 0, inv, e * inv)


def _dot(a, b):
    return jnp.dot(a, b, preferred_element_type=F32)


def _dot_nt(a, b):
    return lax.dot_general(a, b, (((1,), (1,)), ((), ())), preferred_element_type=F32)


def _dot_tn(a, b):
    return lax.dot_general(a, b, (((0,), (0,)), ((), ())), preferred_element_type=F32)


def _split_bf16(x):
    hi = x.astype(BF16)
    lo = (x - hi.astype(F32)).astype(BF16)
    return hi, lo


def _inproj_kernel(x_ref, g_ref, w_ref, o_ref, u_ref):
    @pl.when(pl.program_id(1) == 0)
    def _():
        x = x_ref[...]
        ms = jnp.mean(x * x, axis=-1, keepdims=True)
        u_ref[...] = (x * lax.rsqrt(ms + EPS) * g_ref[...]).astype(BF16)

    o_ref[...] = _dot(u_ref[...], w_ref[...])


def _inproj(h, gain, w, tm=1024, tn=1536):
    T, D = h.shape
    N = w.shape[1]
    return pl.pallas_call(
        _inproj_kernel,
        out_shape=jax.ShapeDtypeStruct((T, N), F32),
        grid=(T // tm, N // tn),
        in_specs=[
            pl.BlockSpec((tm, D), lambda i, j: (i, 0)),
            pl.BlockSpec((1, D), lambda i, j: (0, 0)),
            pl.BlockSpec((D, tn), lambda i, j: (0, j)),
        ],
        out_specs=pl.BlockSpec((tm, tn), lambda i, j: (i, j)),
        scratch_shapes=[pltpu.VMEM((tm, D), BF16)],
        compiler_params=_cparams(("parallel", "arbitrary")),
        name="inproj",
    )(h, gain, w)


_HG_LEVELS = (64, 32, 16, 8, 4, 2)


def _hgrn_consts():
    C = CHUNK
    blocks, masks = [], []
    for n in _HG_LEVELS:
        half = n // 2
        L = np.zeros((C, C), np.float32)
        M = np.zeros((C, C), np.float32)
        for t in range(C):
            base = (t // n) * n
            mid = base + half
            if t >= mid:
                L[t, mid:t + 1] = 1.0
                M[t, base:mid] = 1.0
            else:
                L[t, t + 1:mid] = 1.0
        blocks.append(L)
        masks.append(M)
    masks.append(np.eye(C, dtype=np.float32))
    blocks.append(np.tril(np.ones((C, C), np.float32)))
    blocks.append(np.triu(np.ones((C, C), np.float32), 1))
    return np.concatenate(blocks, 0), np.stack(masks)


def _hgrn_kernel(q_ref, f_ref, i_ref, g_ref, lb_ref, on_ref, lc_ref, mc_ref, o_ref, st_ref):
    C = CHUNK
    nlev = len(_HG_LEVELS)

    @pl.when(pl.program_id(1) == 0)
    def _():
        st_ref[...] = jnp.zeros(st_ref.shape, F32)

    lc = lc_ref[...]
    onorm = on_ref[...]
    n_chunks = q_ref.shape[0] // C

    def chunk(c, carry):
        r0 = pl.multiple_of(c * C, C)
        rows = pl.ds(r0, C)
        for h in range(HG_HEADS):
            cs = slice(h * HG_DK, (h + 1) * HG_DK)
            qin = q_ref[rows, cs]
            fpre = f_ref[rows, cs]
            v = i_ref[rows, cs]
            g = g_ref[rows, cs]
            lb = lb_ref[:, cs]
            e = jnp.exp(-jnp.abs(fpre))
            inv = 1.0 / (1.0 + e)
            sg_pos = jnp.where(fpre >= 0, inv, e * inv)
            sg_neg = jnp.where(fpre >= 0, e * inv, inv)
            lf = jnp.log(lb + (1.0 - lb) * sg_pos)
            k = (1.0 - lb) * sg_neg
            q = qin * _sigmoid(qin)
            lf_hi, lf_lo = _split_bf16(lf)
            dsum = _dot(lc, lf_hi) + _dot(lc, lf_lo)
            ex = jnp.exp(dsum)
            scores = _dot_nt(q.astype(BF16), k.astype(BF16)) * mc_ref[nlev]
            for l in range(nlev):
                el = ex[l * C:(l + 1) * C]
                s_l = _dot_nt((q * el).astype(BF16), (k * el).astype(BF16))
                scores = scores + s_l * mc_ref[l]
            eb = ex[nlev * C:(nlev + 1) * C]
            er = ex[(nlev + 1) * C:(nlev + 2) * C]
            st = st_ref[h]
            vb = v.astype(BF16)
            o = _dot(scores.astype(BF16), vb) + _dot_nt((q * eb).astype(BF16), st.astype(BF16))
            st_ref[h] = st * eb[C - 1:C, :] + _dot_tn(vb, (k * er).astype(BF16))
            ms = jnp.mean(o * o, axis=-1, keepdims=True)
            on = o * lax.rsqrt(ms + EPS) * onorm
            o_ref[rows, cs] = on * (g * _sigmoid(g))
        return carry

    lax.fori_loop(0, n_chunks, chunk, 0)


def _hgrn(z, lb, onorm, B, S, sblk=1024):
    T = B * S
    sblk = min(sblk, S)
    ns = S // sblk
    lc_np, mc_np = _hgrn_consts()
    lc = jnp.asarray(lc_np, BF16)
    mc = jnp.asarray(mc_np, F32)
    cb = HG_OFF // HG_WIDTH

    def zspec(k):
        return pl.BlockSpec((sblk, HG_WIDTH), lambda b, s, k=k: (b * ns + s, cb + k))

    return pl.pallas_call(
        _hgrn_kernel,
        out_shape=jax.ShapeDtypeStruct((T, HG_WIDTH), F32),
        grid=(B, ns),
        in_specs=[
            zspec(0), zspec(1), zspec(2), zspec(3),
            pl.BlockSpec((1, HG_WIDTH), lambda b, s: (0, 0)),
            pl.BlockSpec((1, HG_DK), lambda b, s: (0, 0)),
            pl.BlockSpec(lc.shape, lambda b, s: (0, 0)),
            pl.BlockSpec(mc.shape, lambda b, s: (0, 0, 0)),
        ],
        out_specs=pl.BlockSpec((sblk, HG_WIDTH), lambda b, s: (b * ns + s, 0)),
        scratch_shapes=[pltpu.VMEM((HG_HEADS, HG_DK, HG_DK), F32)],
        compiler_params=_cparams(("parallel", "arbitrary")),
        name="hgrn2",
    )(z, z, z, z, lb, onorm, lc, mc)


def _dsa_kernel(topk, sq_ref, iq_ref, iwq_ref, sc_ref, ikw_ref, cn_ref, wk_ref, wvt_ref,
                o_ref, kk_ref, vt_ref, ik_ref, key_ref, qt_ref, m_ref, l_ref, acc_ref):
    QB = QBLOCK
    j = pl.program_id(1)
    nkb = j + 1
    S = sc_ref.shape[0]
    nb = S // QB
    idx_bits = (S - 1).bit_length()

    @pl.when(j == 0)
    def _():
        c = sc_ref[...]
        ms = jnp.mean(c * c, axis=-1, keepdims=True)
        cb = (c * lax.rsqrt(ms + EPS) * cn_ref[...]).astype(BF16)
        kfull = _dot(cb, wk_ref[...]).astype(BF16)
        vtfull = _dot_nt(wvt_ref[...], cb).astype(BF16)
        ikfull = ikw_ref[:, 0:IDX_DIM].astype(BF16)
        for kb in range(nb):
            kk_ref[kb] = kfull[kb * QB:(kb + 1) * QB]
            vt_ref[kb] = vtfull[:, kb * QB:(kb + 1) * QB]
            ik_ref[kb] = ikfull[kb * QB:(kb + 1) * QB]

    qt_ref[...] = (sq_ref[...] * (SA_DH ** -0.5)).T.astype(BF16)
    iqt = iq_ref[...].T.astype(BF16)
    iwt = iwq_ref[...].T * (IDX_HEADS ** -0.5)

    row = lax.broadcasted_iota(I32, (QB, QB), 0)
    lane = lax.broadcasted_iota(I32, (QB, QB), 1)
    diag_bad = (row >= CHUNK) & (lane < CHUNK)

    def score_block(kb, carry):
        ikb = ik_ref[kb]
        acc = jnp.zeros((QB, QB), F32)
        for h in range(IDX_HEADS):
            s_h = _dot(ikb, iqt[h * IDX_DIM:(h + 1) * IDX_DIM])
            acc = acc + jnp.maximum(s_h, 0.0) * iwt[IDX_DIM + h:IDX_DIM + h + 1, :]
        acc = acc + 0.0
        acc = jnp.where((kb == j) & diag_bad, -jnp.inf, acc)
        bits = pltpu.bitcast(acc, I32)
        key_ref[kb] = jnp.where(bits < 0, bits ^ jnp.int32(0x7FFFFFFF), bits)
        return carry

    lax.fori_loop(0, nkb, score_block, 0)

    def count(pred_fn):
        def body(kb, cnt):
            p = pred_fn(kb, key_ref[kb])
            return cnt + jnp.sum(jnp.where(p, 1, 0).astype(I32).reshape(QB // 8, 8, QB), axis=0)
        c8 = lax.fori_loop(0, nkb, body, jnp.zeros((8, QB), I32))
        return jnp.sum(c8, axis=0, keepdims=True)

    def search(_):
        sign = jnp.int32(-2147483648)

        def vbit(i, u):
            cand_u = u | lax.shift_left(jnp.int32(1), 31 - i)
            cand_s = cand_u ^ sign
            cnt = count(lambda kb, key: key >= cand_s)
            return jnp.where(cnt >= topk, cand_u, u)

        u = lax.fori_loop(0, 32, vbit, jnp.zeros((1, QB), I32))
        thr = u ^ sign
        need = topk - count(lambda kb, key: key > thr)

        def ibit(i, jp):
            cand = jp + lax.shift_left(jnp.int32(1), idx_bits - 1 - i)
            cnt = count(lambda kb, key: (key == thr) & ((row + kb * QB) < cand))
            return jnp.where(cnt < need, cand, jp)

        jp = lax.fori_loop(0, idx_bits, ibit, jnp.zeros((1, QB), I32))
        return thr, jp

    def take_all(_):
        return (jnp.full((1, QB), -2147483648, I32), jnp.full((1, QB), -1, I32))

    thr, jp = lax.cond(nkb * QB > topk, search, take_all, 0)

    m_ref[...] = jnp.full(m_ref.shape, NEG_BIG, F32)
    l_ref[...] = jnp.zeros(l_ref.shape, F32)
    acc_ref[...] = jnp.zeros(acc_ref.shape, F32)

    def attend(kb, carry):
        key = key_ref[kb]
        sel = (key > thr) | ((key == thr) & ((row + kb * QB) <= jp))
        sel = sel & jnp.logical_not((kb == j) & diag_bad)
        kblk = kk_ref[kb]
        vtb = vt_ref[kb]
        for h in range(SA_HEADS):
            lg = _dot(kblk, qt_ref[h * SA_DH:(h + 1) * SA_DH, :])
            lg = jnp.where(sel, lg, NEG_BIG)
            m_old = m_ref[h:h + 1, :]
            m_new = jnp.maximum(m_old, jnp.max(lg, axis=0, keepdims=True))
            alpha = jnp.exp(m_old - m_new)
            p = jnp.where(sel, jnp.exp(lg - m_new), 0.0)
            l_ref[h:h + 1, :] = alpha * l_ref[h:h + 1, :] + jnp.sum(p, axis=0, keepdims=True)
            acc_ref[h] = acc_ref[h] * alpha + _dot(vtb, p.astype(BF16))
            m_ref[h:h + 1, :] = m_new
        return carry

    lax.fori_loop(0, nkb, attend, 0)

    outs = []
    for h in range(SA_HEADS):
        outs.append(acc_ref[h] * (1.0 / l_ref[h:h + 1, :]))
    o_ref[...] = jnp.concatenate(outs, axis=0).T


def _dsa(z, cnorm, wk, wvt, B, S):
    T = B * S
    nb = S // QBLOCK
    topk = min(TOPK_MAX, S // 4)
    kern = functools.partial(_dsa_kernel, topk)
    return pl.pallas_call(
        kern,
        out_shape=jax.ShapeDtypeStruct((T, SA_WIDTH), F32),
        grid=(B, nb),
        in_specs=[
            pl.BlockSpec((QBLOCK, SA_WIDTH), lambda b, j: (b * nb + j, SQ_OFF // SA_WIDTH)),
            pl.BlockSpec((QBLOCK, IDX_HEADS * IDX_DIM),
                         lambda b, j: (b * nb + j, IQ_OFF // (IDX_HEADS * IDX_DIM))),
            pl.BlockSpec((QBLOCK, LANES), lambda b, j: (b * nb + j, IKW_OFF // LANES)),
            pl.BlockSpec((S, SA_LATENT), lambda b, j: (b, SC_OFF // SA_LATENT)),
            pl.BlockSpec((S, LANES), lambda b, j: (b, IKW_OFF // LANES)),
            pl.BlockSpec((1, SA_LATENT), lambda b, j: (0, 0)),
            pl.BlockSpec((SA_LATENT, SA_DH), lambda b, j: (0, 0)),
            pl.BlockSpec((SA_DH, SA_LATENT), lambda b, j: (0, 0)),
        ],
        out_specs=pl.BlockSpec((QBLOCK, SA_WIDTH), lambda b, j: (b * nb + j, 0)),
        scratch_shapes=[
            pltpu.VMEM((nb, QBLOCK, SA_DH), BF16),
            pltpu.VMEM((nb, SA_DH, QBLOCK), BF16),
            pltpu.VMEM((nb, QBLOCK, IDX_DIM), BF16),
            pltpu.VMEM((nb, QBLOCK, QBLOCK), I32),
            pltpu.VMEM((SA_WIDTH, QBLOCK), BF16),
            pltpu.VMEM((SA_HEADS, QBLOCK), F32),
            pltpu.VMEM((SA_HEADS, QBLOCK), F32),
            pltpu.VMEM((SA_HEADS, SA_DH, QBLOCK), F32),
        ],
        compiler_params=_cparams(("parallel", "arbitrary")),
        name="dsa",
    )(z, z, z, z, z, cnorm, wk, wvt)


def _sb_kernel(q_ref, k_ref, v_ref, uo_ref, o_ref, kt_ref, vb_ref):
    QB = QBLOCK
    j = pl.program_id(2)
    S = k_ref.shape[0]
    nb = S // QB
    hp = q_ref.shape[1] // SB_DH

    @pl.when(j == 0)
    def _():
        kt = k_ref[...].T.astype(BF16)
        vb = v_ref[...].astype(BF16)
        for kb in range(nb):
            kt_ref[kb] = kt[:, kb * QB:(kb + 1) * QB]
            vb_ref[kb] = vb[kb * QB:(kb + 1) * QB]

    uo = uo_ref[...]
    row = lax.broadcasted_iota(I32, (QB, QB), 0)
    lane = lax.broadcasted_iota(I32, (QB, QB), 1)
    strict = lane < row
    qs = (q_ref[...] * (SB_DH ** -0.5)).astype(BF16)

    def block(h, kb, run, o, diag):
        z = _dot(qs[:, h * SB_DH:(h + 1) * SB_DH], kt_ref[kb, h * SB_DH:(h + 1) * SB_DH, :])
        l1p = jnp.log(1.0 + jnp.exp(-jnp.abs(z)))
        ls_pos = jnp.minimum(z, 0.0) - l1p
        lk = ls_pos - z
        if diag:
            lk = jnp.where(strict, lk, 0.0)
        lk_hi, lk_lo = _split_bf16(lk)
        cs = _dot(lk_hi, uo) + _dot(lk_lo, uo)
        a = jnp.exp(ls_pos + cs[:, :QB] + run)
        if diag:
            a = jnp.where(strict, a, 0.0)
        o = o + _dot(a.astype(BF16), vb_ref[kb, :, h * SB_DH:(h + 1) * SB_DH])
        return run + cs[:, QB:], o

    outs = []
    for h in range(hp):
        run, o = block(h, j, jnp.zeros((QB, QB), F32), jnp.zeros((QB, SB_DH), F32), True)

        def body(i, carry, h=h):
            run, o = carry
            return block(h, j - 1 - i, run, o, False)

        run, o = lax.fori_loop(0, j, body, (run, o))
        outs.append(o)
    o_ref[...] = jnp.concatenate(outs, axis=1)


def _sb(z, B, S):
    T = B * S
    nb = S // QBLOCK
    hpw = LANES
    ng = SB_WIDTH // hpw
    u = np.triu(np.ones((QBLOCK, QBLOCK), np.float32), 0).T
    u = u - np.eye(QBLOCK, dtype=np.float32)
    uo = jnp.asarray(np.concatenate([u, np.ones((QBLOCK, QBLOCK), np.float32)], 1), BF16)
    qoff = SB_OFF // hpw
    return pl.pallas_call(
        _sb_kernel,
        out_shape=jax.ShapeDtypeStruct((T, SB_WIDTH), F32),
        grid=(B, ng, nb),
        in_specs=[
            pl.BlockSpec((QBLOCK, hpw), lambda b, g, j: (b * nb + j, qoff + g)),
            pl.BlockSpec((S, hpw), lambda b, g, j: (b, qoff + ng + g)),
            pl.BlockSpec((S, hpw), lambda b, g, j: (b, qoff + 2 * ng + g)),
            pl.BlockSpec(uo.shape, lambda b, g, j: (0, 0)),
        ],
        out_specs=pl.BlockSpec((QBLOCK, hpw), lambda b, g, j: (b * nb + j, g)),
        scratch_shapes=[
            pltpu.VMEM((nb, hpw, QBLOCK), BF16),
            pltpu.VMEM((nb, QBLOCK, hpw), BF16),
        ],
        compiler_params=_cparams(("parallel", "parallel", "arbitrary")),
        name="stickbreak",
    )(z, z, z, uo)


def _merge_kernel(a_ref, b_ref, c_ref, gz_ref, h_ref, wb_ref, wo_ref, gn_ref, ho_ref, v_ref):
    merged = None
    for n, br in enumerate((a_ref, b_ref, c_ref)):
        proj = _dot(br[...].astype(BF16), wb_ref[n])
        gate = _sigmoid(gz_ref[:, n * D_MODEL:(n + 1) * D_MODEL])
        merged = gate * proj if merged is None else merged + gate * proj
    hn = h_ref[...] + _dot(merged.astype(BF16), wo_ref[...])
    ho_ref[...] = hn
    ms = jnp.mean(hn * hn, axis=-1, keepdims=True)
    v_ref[...] = (hn * lax.rsqrt(ms + EPS) * gn_ref[...]).astype(BF16)


def _merge(a, b, c, z, h, wb, wo, gn, tm=256):
    T, D = h.shape
    bw = BRANCH_WIDTH
    return pl.pallas_call(
        _merge_kernel,
        out_shape=(jax.ShapeDtypeStruct((T, D), F32), jax.ShapeDtypeStruct((T, D), BF16)),
        grid=(T // tm,),
        in_specs=[
            pl.BlockSpec((tm, bw), lambda i: (i, 0)),
            pl.BlockSpec((tm, bw), lambda i: (i, 0)),
            pl.BlockSpec((tm, bw), lambda i: (i, 0)),
            pl.BlockSpec((tm, N_BRANCH * D), lambda i: (i, GZ_OFF // (N_BRANCH * D))),
            pl.BlockSpec((tm, D), lambda i: (i, 0)),
            pl.BlockSpec((N_BRANCH, bw, D), lambda i: (0, 0, 0)),
            pl.BlockSpec((D, D), lambda i: (0, 0)),
            pl.BlockSpec((1, D), lambda i: (0, 0)),
        ],
        out_specs=(pl.BlockSpec((tm, D), lambda i: (i, 0)), pl.BlockSpec((tm, D), lambda i: (i, 0))),
        compiler_params=_cparams(("parallel",)),
        name="merge",
    )(a, b, c, z, h, wb, wo, gn)


def _swiglu_kernel(v_ref, h_ref, wg_ref, wu_ref, wd_ref, o_ref, acc_ref):
    f = pl.program_id(1)

    @pl.when(f == 0)
    def _():
        acc_ref[...] = h_ref[...]

    v = v_ref[...]
    gt = _dot(v, wg_ref[...])
    up = _dot(v, wu_ref[...])
    act = (gt * _sigmoid(gt) * up).astype(BF16)
    acc_ref[...] += _dot(act, wd_ref[...])

    @pl.when(f == pl.num_programs(1) - 1)
    def _():
        o_ref[...] = acc_ref[...]


def _swiglu(v, h, wg, wu, wd, tm=1024, tf=256):
    T, D = h.shape
    F = wg.shape[1]
    return pl.pallas_call(
        _swiglu_kernel,
        out_shape=jax.ShapeDtypeStruct((T, D), F32),
        grid=(T // tm, F // tf),
        in_specs=[
            pl.BlockSpec((tm, D), lambda i, f: (i, 0)),
            pl.BlockSpec((tm, D), lambda i, f: (i, 0)),
            pl.BlockSpec((D, tf), lambda i, f: (0, f)),
            pl.BlockSpec((D, tf), lambda i, f: (0, f)),
            pl.BlockSpec((tf, D), lambda i, f: (f, 0)),
        ],
        out_specs=pl.BlockSpec((tm, D), lambda i, f: (i, 0)),
        scratch_shapes=[pltpu.VMEM((tm, D), F32)],
        compiler_params=_cparams(("parallel", "arbitrary")),
        name="swiglu",
    )(v, h, wg, wu, wd)


def _moe_kernel(v_ref, h_ref, gn_ref, r_ref, wg_ref, wu_ref, wd_ref, o_ref, acc_ref, cw_ref):
    e = pl.program_id(1)
    f = pl.program_id(2)
    first = (e == 0) & (f == 0)
    last = (e == pl.num_programs(1) - 1) & (f == pl.num_programs(2) - 1)

    @pl.when(first)
    def _():
        hh = h_ref[...]
        acc_ref[...] = hh
        ms = jnp.mean(hh * hh, axis=-1, keepdims=True)
        vf = hh * lax.rsqrt(ms + EPS) * gn_ref[...]
        v_hi, v_lo = _split_bf16(vf)
        r_hi, r_lo = _split_bf16(r_ref[...])
        logits = _dot(v_hi, r_hi) + _dot(v_hi, r_lo) + _dot(v_lo, r_hi)
        lane = lax.broadcasted_iota(I32, logits.shape, 1)
        logits = jnp.where(lane < N_EXPERTS, logits, -jnp.inf)
        m1 = jnp.max(logits, axis=-1, keepdims=True)
        i1 = jnp.min(jnp.where(logits == m1, lane, LANES), axis=-1, keepdims=True)
        rest = jnp.where(lane == i1, -jnp.inf, logits)
        m2 = jnp.max(rest, axis=-1, keepdims=True)
        i2 = jnp.min(jnp.where(rest == m2, lane, LANES), axis=-1, keepdims=True)
        e2 = jnp.exp(m2 - m1)
        w1 = 1.0 / (1.0 + e2)
        cw_ref[...] = jnp.where(lane == i1, w1, 0.0) + jnp.where(lane == i2, e2 * w1, 0.0)

    cw = cw_ref[...]
    lane = lax.broadcasted_iota(I32, cw.shape, 1)
    ce = jnp.sum(jnp.where(lane == e, cw, 0.0), axis=-1, keepdims=True)
    v = v_ref[...]
    gt = _dot(v, wg_ref[0])
    up = _dot(v, wu_ref[0])
    act = (gt * _sigmoid(gt) * up * ce).astype(BF16)
    acc_ref[...] += _dot(act, wd_ref[0])

    @pl.when(last)
    def _():
        o_ref[...] = acc_ref[...]


def _moe(v, h, gn, router, wg, wu, wd, tm=1024, tf=256):
    T, D = h.shape
    E, _, F = wg.shape
    return pl.pallas_call(
        _moe_kernel,
        out_shape=jax.ShapeDtypeStruct((T, D), F32),
        grid=(T // tm, E, F // tf),
        in_specs=[
            pl.BlockSpec((tm, D), lambda i, e, f: (i, 0)),
            pl.BlockSpec((tm, D), lambda i, e, f: (i, 0)),
            pl.BlockSpec((1, D), lambda i, e, f: (0, 0)),
            pl.BlockSpec((D, LANES), lambda i, e, f: (0, 0)),
            pl.BlockSpec((1, D, tf), lambda i, e, f: (e, 0, f)),
            pl.BlockSpec((1, D, tf), lambda i, e, f: (e, 0, f)),
            pl.BlockSpec((1, tf, D), lambda i, e, f: (e, f, 0)),
        ],
        out_specs=pl.BlockSpec((tm, D), lambda i, e, f: (i, 0)),
        scratch_shapes=[pltpu.VMEM((tm, D), F32), pltpu.VMEM((tm, LANES), F32)],
        compiler_params=_cparams(("parallel", "arbitrary", "arbitrary")),
        name="moe",
    )(v, h, gn, router, wg, wu, wd)


def _ple_kernel(final, h_ref, p_ref, gn_ref, wg_ref, wp_ref, fn_ref, o_ref):
    h = h_ref[...]
    ms = jnp.mean(h * h, axis=-1, keepdims=True)
    u = (h * lax.rsqrt(ms + EPS) * gn_ref[...]).astype(BF16)
    gate = _sigmoid(_dot(u, wg_ref[...]))
    hn = h + gate * _dot(p_ref[...].astype(BF16), wp_ref[...])
    if final:
        ms = jnp.mean(hn * hn, axis=-1, keepdims=True)
        hn = hn * lax.rsqrt(ms + EPS) * fn_ref[...]
    o_ref[...] = hn


def _ple(h, p, gn, wg, wp, fn, final, tm=512):
    T, D = h.shape
    return pl.pallas_call(
        functools.partial(_ple_kernel, final),
        out_shape=jax.ShapeDtypeStruct((T, D), F32),
        grid=(T // tm,),
        in_specs=[
            pl.BlockSpec((tm, D), lambda i: (i, 0)),
            pl.BlockSpec((tm, PLE_DIM), lambda i: (i, 0)),
            pl.BlockSpec((1, D), lambda i: (0, 0)),
            pl.BlockSpec((D, D), lambda i: (0, 0)),
            pl.BlockSpec((PLE_DIM, D), lambda i: (0, 0)),
            pl.BlockSpec((1, D), lambda i: (0, 0)),
        ],
        out_specs=pl.BlockSpec((tm, D), lambda i: (i, 0)),
        compiler_params=_cparams(("parallel",)),
        name="ple",
    )(h, p, gn, wg, wp, fn)


def _reorder_w_in(w):
    o_sq = 4 * HG_WIDTH
    o_sc = o_sq + SA_WIDTH
    o_iq = o_sc + SA_LATENT
    o_ik = o_iq + IDX_HEADS * IDX_DIM
    o_bq = o_ik + IDX_DIM + IDX_HEADS
    o_gz = o_bq + 3 * SB_WIDTH
    pad = jnp.zeros((w.shape[0], LANES - IDX_DIM - IDX_HEADS), w.dtype)
    return jnp.concatenate([
        w[:, o_gz:], w[:, :o_sq], w[:, o_sq:o_sc], w[:, o_iq:o_ik], w[:, o_sc:o_iq],
        w[:, o_ik:o_bq], pad, w[:, o_bq:o_gz]], axis=1).astype(BF16)


def kernel(x, p, lb_param, norm_mix, w_in, hg_onorm, sa_cnorm, sa_wk, sa_wv, w_branch, w_out,
           norm_ffn, dense_wg, dense_wu, dense_wd, moe_router, moe_wg, moe_wu, moe_wd, norm_ple,
           ple_gate, ple_proj, norm_final):
    B, S, D = x.shape
    T = B * S
    depth = w_in.shape[0]
    lb_sm = jax.nn.softmax(lb_param.astype(F32), axis=0)
    lb_all = jnp.cumsum(lb_sm, axis=0) - lb_sm[0:1]
    h = x.reshape(T, D)
    for i in range(depth):
        z = _inproj(h, norm_mix[i][None, :], _reorder_w_in(w_in[i]))
        a = _hgrn(z, lb_all[i][None, :], hg_onorm[i][None, :], B, S)
        b = _dsa(z, sa_cnorm[i][None, :], sa_wk[i].astype(BF16), sa_wv[i].T.astype(BF16), B, S)
        c = _sb(z, B, S)
        h, v = _merge(a, b, c, z, h, w_branch[i].astype(BF16), w_out[i].astype(BF16),
                      norm_ffn[i][None, :])
        jj = i // 2
        if i % 2 == 0:
            h = _swiglu(v, h, dense_wg[jj].astype(BF16), dense_wu[jj].astype(BF16),
                        dense_wd[jj].astype(BF16))
        else:
            router = jnp.pad(moe_router[jj], ((0, 0), (0, LANES - N_EXPERTS)))
            h = _moe(v, h, norm_ffn[i][None, :], router, moe_wg[jj].astype(BF16),
                     moe_wu[jj].astype(BF16), moe_wd[jj].astype(BF16))
        h = _ple(h, p[i].reshape(T, PLE_DIM), norm_ple[i][None, :], ple_gate[i].astype(BF16),
                 ple_proj[i].astype(BF16), norm_final[None, :], i == depth - 1)
    return h.reshape(B, S, D)
```

```python
import functools

import numpy as np
import jax
import jax.numpy as jnp
from jax import lax
from jax.experimental import pallas as pl
from jax.experimental.pallas import tpu as pltpu

F32 = jnp.float32
BF16 = jnp.bfloat16
I32 = jnp.int32

D_MODEL = 1024
CHUNK = 64
QBLOCK = 128
HG_HEADS = 4
HG_DK = 128
HG_WIDTH = HG_HEADS * HG_DK
SA_HEADS = 8
SA_DH = 64
SA_WIDTH = SA_HEADS * SA_DH
SA_LATENT = 128
IDX_HEADS = 4
IDX_DIM = 64
TOPK_MAX = 256
SB_HEADS = 8
SB_DH = 64
SB_WIDTH = SB_HEADS * SB_DH
N_BRANCH = 3
BRANCH_WIDTH = 512
N_EXPERTS = 8
PLE_DIM = 256
EPS = 1e-6

LANES = 128

GZ_OFF = 0
HG_OFF = 3 * D_MODEL
SQ_OFF = HG_OFF + 4 * HG_WIDTH
IQ_OFF = SQ_OFF + SA_WIDTH
SC_OFF = IQ_OFF + IDX_HEADS * IDX_DIM
IKW_OFF = SC_OFF + SA_LATENT
SB_OFF = IKW_OFF + LANES
ZW = SB_OFF + 3 * SB_WIDTH

VMEM_LIMIT = 56 * 1024 * 1024

NEG_BIG = -1e30
INT_MIN = -2147483648
KEY_GROUP = 4
SB_UNDERFLOW = -104.0


def _cparams(sem):
    return pltpu.CompilerParams(dimension_semantics=sem, vmem_limit_bytes=VMEM_LIMIT)


def _sigmoid(x):
    e = jnp.exp(-jnp.abs(x))
    inv = 1.0 / (1.0 + e)
    return jnp.where(x >= 0, inv, e * inv)


def _dot(a, b):
    return jnp.dot(a, b, preferred_element_type=F32)


def _dot_nt(a, b):
    return lax.dot_general(a, b, (((1,), (1,)), ((), ())), preferred_element_type=F32)


def _dot_tn(a, b):
    return lax.dot_general(a, b, (((0,), (0,)), ((), ())), preferred_element_type=F32)


def _split_bf16(x):
    hi = x.astype(BF16)
    lo = (x - hi.astype(F32)).astype(BF16)
    return hi, lo


def _inproj_kernel(x_ref, g_ref, w_ref, o_ref, u_ref):
    @pl.when(pl.program_id(1) == 0)
    def _():
        x = x_ref[...]
        ms = jnp.mean(x * x, axis=-1, keepdims=True)
        u_ref[...] = (x * lax.rsqrt(ms + EPS) * g_ref[...]).astype(BF16)

    o_ref[...] = _dot(u_ref[...], w_ref[...])


def _inproj(h, gain, w, tm=1024, tn=1536):
    T, D = h.shape
    N = w.shape[1]
    return pl.pallas_call(
        _inproj_kernel,
        out_shape=jax.ShapeDtypeStruct((T, N), F32),
        grid=(T // tm, N // tn),
        in_specs=[
            pl.BlockSpec((tm, D), lambda i, j: (i, 0)),
            pl.BlockSpec((1, D), lambda i, j: (0, 0)),
            pl.BlockSpec((D, tn), lambda i, j: (0, j)),
        ],
        out_specs=pl.BlockSpec((tm, tn), lambda i, j: (i, j)),
        scratch_shapes=[pltpu.VMEM((tm, D), BF16)],
        compiler_params=_cparams(("parallel", "arbitrary")),
        name="inproj",
    )(h, gain, w)


_HG_LEVELS = (64, 32, 16, 8, 4, 2)


def _hgrn_consts():
    C = CHUNK
    blocks, masks = [], []
    for n in _HG_LEVELS:
        half = n // 2
        L = np.zeros((C, C), np.float32)
        M = np.zeros((C, C), np.float32)
        for t in range(C):
            base = (t // n) * n
            mid = base + half
            if t >= mid:
                L[t, mid:t + 1] = 1.0
                M[t, base:mid] = 1.0
            else:
                L[t, t + 1:mid] = 1.0
        blocks.append(L)
        masks.append(M)
    masks.append(np.eye(C, dtype=np.float32))
    blocks.append(np.tril(np.ones((C, C), np.float32)))
    blocks.append(np.triu(np.ones((C, C), np.float32), 1))
    return np.concatenate(blocks, 0), np.stack(masks)


def _hgrn_kernel(q_ref, f_ref, i_ref, g_ref, lb_ref, on_ref, lc_ref, mc_ref, o_ref, st_ref):
    C = CHUNK
    nlev = len(_HG_LEVELS)

    @pl.when(pl.program_id(1) == 0)
    def _():
        st_ref[...] = jnp.zeros(st_ref.shape, F32)

    lc = lc_ref[...]
    onorm = on_ref[...]
    n_chunks = q_ref.shape[0] // C

    def chunk(c, carry):
        r0 = pl.multiple_of(c * C, C)
        rows = pl.ds(r0, C)
        for h in range(HG_HEADS):
            cs = slice(h * HG_DK, (h + 1) * HG_DK)
            qin = q_ref[rows, cs]
            fpre = f_ref[rows, cs]
            v = i_ref[rows, cs]
            g = g_ref[rows, cs]
            lb = lb_ref[:, cs]
            e = jnp.exp(-jnp.abs(fpre))
            inv = 1.0 / (1.0 + e)
            sg_pos = jnp.where(fpre >= 0, inv, e * inv)
            sg_neg = jnp.where(fpre >= 0, e * inv, inv)
            lf = jnp.log(lb + (1.0 - lb) * sg_pos)
            k = (1.0 - lb) * sg_neg
            q = qin * _sigmoid(qin)
            lf_hi, lf_lo = _split_bf16(lf)
            dsum = _dot(lc, lf_hi) + _dot(lc, lf_lo)
            ex = jnp.exp(dsum)
            scores = _dot_nt(q.astype(BF16), k.astype(BF16)) * mc_ref[nlev]
            for l in range(nlev):
                el = ex[l * C:(l + 1) * C]
                s_l = _dot_nt((q * el).astype(BF16), (k * el).astype(BF16))
                scores = scores + s_l * mc_ref[l]
            eb = ex[nlev * C:(nlev + 1) * C]
            er = ex[(nlev + 1) * C:(nlev + 2) * C]
            st = st_ref[h]
            vb = v.astype(BF16)
            o = _dot(scores.astype(BF16), vb) + _dot_nt((q * eb).astype(BF16), st.astype(BF16))
            st_ref[h] = st * eb[C - 1:C, :] + _dot_tn(vb, (k * er).astype(BF16))
            ms = jnp.mean(o * o, axis=-1, keepdims=True)
            on = o * lax.rsqrt(ms + EPS) * onorm
            o_ref[rows, cs] = on * (g * _sigmoid(g))
        return carry

    lax.fori_loop(0, n_chunks, chunk, 0)


def _hgrn(z, lb, onorm, B, S, sblk=1024):
    T = B * S
    sblk = min(sblk, S)
    ns = S // sblk
    lc_np, mc_np = _hgrn_consts()
    lc = jnp.asarray(lc_np, BF16)
    mc = jnp.asarray(mc_np, F32)
    cb = HG_OFF // HG_WIDTH

    def zspec(k):
        return pl.BlockSpec((sblk, HG_WIDTH), lambda b, s, k=k: (b * ns + s, cb + k))

    return pl.pallas_call(
        _hgrn_kernel,
        out_shape=jax.ShapeDtypeStruct((T, HG_WIDTH), F32),
        grid=(B, ns),
        in_specs=[
            zspec(0), zspec(1), zspec(2), zspec(3),
            pl.BlockSpec((1, HG_WIDTH), lambda b, s: (0, 0)),
            pl.BlockSpec((1, HG_DK), lambda b, s: (0, 0)),
            pl.BlockSpec(lc.shape, lambda b, s: (0, 0)),
            pl.BlockSpec(mc.shape, lambda b, s: (0, 0, 0)),
        ],
        out_specs=pl.BlockSpec((sblk, HG_WIDTH), lambda b, s: (b * ns + s, 0)),
        scratch_shapes=[pltpu.VMEM((HG_HEADS, HG_DK, HG_DK), F32)],
        compiler_params=_cparams(("parallel", "arbitrary")),
        name="hgrn2",
    )(z, z, z, z, lb, onorm, lc, mc)


def _dsa_kernel(topk, sq_ref, iq_ref, iwq_ref, sc_ref, ikw_ref, cn_ref, wk_ref, wvt_ref,
                o_ref, kk_ref, vt_ref, ik_ref, key_ref, qt_ref, m_ref, l_ref, acc_ref):
    QB = QBLOCK
    j = pl.program_id(1)
    nkb = j + 1
    S = sc_ref.shape[0]
    nb = S // QB
    idx_bits = (S - 1).bit_length()

    @pl.when(j == 0)
    def _():
        c = sc_ref[...]
        ms = jnp.mean(c * c, axis=-1, keepdims=True)
        cb = (c * lax.rsqrt(ms + EPS) * cn_ref[...]).astype(BF16)
        kfull = _dot(cb, wk_ref[...]).astype(BF16)
        vtfull = _dot_nt(wvt_ref[...], cb).astype(BF16)
        ikfull = ikw_ref[:, 0:IDX_DIM].astype(BF16)
        for kb in range(nb):
            kk_ref[kb] = kfull[kb * QB:(kb + 1) * QB]
            vt_ref[kb] = vtfull[:, kb * QB:(kb + 1) * QB]
            ik_ref[kb] = ikfull[kb * QB:(kb + 1) * QB]

    qt_ref[...] = (sq_ref[...] * (SA_DH ** -0.5)).T.astype(BF16)
    iqt = iq_ref[...].T.astype(BF16)
    iwt = iwq_ref[...].T * (IDX_HEADS ** -0.5)

    row = lax.broadcasted_iota(I32, (QB, QB), 0)
    lane = lax.broadcasted_iota(I32, (QB, QB), 1)
    diag_bad = (row >= CHUNK) & (lane < CHUNK)

    def score_block(kb, carry):
        ikb = ik_ref[kb]
        acc = jnp.zeros((QB, QB), F32)
        for h in range(IDX_HEADS):
            s_h = _dot(ikb, iqt[h * IDX_DIM:(h + 1) * IDX_DIM])
            acc = acc + jnp.maximum(s_h, 0.0) * iwt[IDX_DIM + h:IDX_DIM + h + 1, :]
        acc = acc + 0.0
        acc = jnp.where((kb == j) & diag_bad, -jnp.inf, acc)
        bits = pltpu.bitcast(acc, I32)
        key_ref[kb] = jnp.where(bits < 0, bits ^ jnp.int32(0x7FFFFFFF), bits)
        return carry

    lax.fori_loop(0, nkb, score_block, 0)

    n_groups = (nkb + (KEY_GROUP - 1)) // KEY_GROUP
    for d in range(1, KEY_GROUP):
        @pl.when(j + d < n_groups * KEY_GROUP)
        def _(d=d):
            key_ref[j + d] = jnp.full((QB, QB), INT_MIN, I32)

    def count(pred_fn):
        def body(g, cnt):
            for d in range(KEY_GROUP):
                kb = g * KEY_GROUP + d
                p = pred_fn(kb, key_ref[kb])
                cnt = cnt + jnp.sum(jnp.where(p, 1, 0).astype(I32).reshape(QB // 8, 8, QB), axis=0)
            return cnt
        c8 = lax.fori_loop(0, n_groups, body, jnp.zeros((8, QB), I32))
        return jnp.sum(c8, axis=0, keepdims=True)

    def search(_):
        sign = jnp.int32(INT_MIN)

        def vbit(i, u):
            cand_u = u | lax.shift_left(jnp.int32(1), 31 - i)
            cand_s = cand_u ^ sign
            cnt = count(lambda kb, key: key >= cand_s)
            return jnp.where(cnt >= topk, cand_u, u)

        u = lax.fori_loop(0, 32, vbit, jnp.zeros((1, QB), I32))
        thr = u ^ sign
        need = topk - count(lambda kb, key: key > thr)
        n_tied = count(lambda kb, key: key == thr)

        def break_ties(_):
            def ibit(i, jp):
                cand = jp + lax.shift_left(jnp.int32(1), idx_bits - 1 - i)
                cnt = count(lambda kb, key: (key == thr) & ((row + kb * QB) < cand))
                return jnp.where(cnt < need, cand, jp)

            return lax.fori_loop(0, idx_bits, ibit, jnp.zeros((1, QB), I32))

        def keep_ties(_):
            return jnp.full((1, QB), S, I32)

        jp = lax.cond(jnp.max(n_tied - need) > 0, break_ties, keep_ties, 0)
        return thr, jp

    def take_all(_):
        return (jnp.full((1, QB), INT_MIN, I32), jnp.full((1, QB), -1, I32))

    thr, jp = lax.cond(nkb * QB > topk, search, take_all, 0)

    m_ref[...] = jnp.full(m_ref.shape, NEG_BIG, F32)
    l_ref[...] = jnp.zeros(l_ref.shape, F32)
    acc_ref[...] = jnp.zeros(acc_ref.shape, F32)

    def attend(kb, carry):
        key = key_ref[kb]
        sel = (key > thr) | ((key == thr) & ((row + kb * QB) <= jp))
        sel = sel & jnp.logical_not((kb == j) & diag_bad)
        bias = jnp.where(sel, 0.0, -jnp.inf)
        kblk = kk_ref[kb]
        vtb = vt_ref[kb]
        for h in range(SA_HEADS):
            lg = _dot(kblk, qt_ref[h * SA_DH:(h + 1) * SA_DH, :]) + bias
            m_old = m_ref[h:h + 1, :]
            m_new = jnp.maximum(m_old, jnp.max(lg, axis=0, keepdims=True))
            alpha = jnp.exp(m_old - m_new)
            p = jnp.exp(lg - m_new)
            l_ref[h:h + 1, :] = alpha * l_ref[h:h + 1, :] + jnp.sum(p, axis=0, keepdims=True)
            acc_ref[h] = acc_ref[h] * alpha + _dot(vtb, p.astype(BF16))
            m_ref[h:h + 1, :] = m_new
        return carry

    lax.fori_loop(0, nkb, attend, 0)

    outs = []
    for h in range(SA_HEADS):
        outs.append(acc_ref[h] * (1.0 / l_ref[h:h + 1, :]))
    o_ref[...] = jnp.concatenate(outs, axis=0).T


def _dsa(z, cnorm, wk, wvt, B, S):
    T = B * S
    nb = S // QBLOCK
    topk = min(TOPK_MAX, S // 4)
    assert nb % KEY_GROUP == 0
    kern = functools.partial(_dsa_kernel, topk)
    return pl.pallas_call(
        kern,
        out_shape=jax.ShapeDtypeStruct((T, SA_WIDTH), F32),
        grid=(B, nb),
        in_specs=[
            pl.BlockSpec((QBLOCK, SA_WIDTH), lambda b, j: (b * nb + j, SQ_OFF // SA_WIDTH)),
            pl.BlockSpec((QBLOCK, IDX_HEADS * IDX_DIM),
                         lambda b, j: (b * nb + j, IQ_OFF // (IDX_HEADS * IDX_DIM))),
            pl.BlockSpec((QBLOCK, LANES), lambda b, j: (b * nb + j, IKW_OFF // LANES)),
            pl.BlockSpec((S, SA_LATENT), lambda b, j: (b, SC_OFF // SA_LATENT)),
            pl.BlockSpec((S, LANES), lambda b, j: (b, IKW_OFF // LANES)),
            pl.BlockSpec((1, SA_LATENT), lambda b, j: (0, 0)),
            pl.BlockSpec((SA_LATENT, SA_DH), lambda b, j: (0, 0)),
            pl.BlockSpec((SA_DH, SA_LATENT), lambda b, j: (0, 0)),
        ],
        out_specs=pl.BlockSpec((QBLOCK, SA_WIDTH), lambda b, j: (b * nb + j, 0)),
        scratch_shapes=[
            pltpu.VMEM((nb, QBLOCK, SA_DH), BF16),
            pltpu.VMEM((nb, SA_DH, QBLOCK), BF16),
            pltpu.VMEM((nb, QBLOCK, IDX_DIM), BF16),
            pltpu.VMEM((nb, QBLOCK, QBLOCK), I32),
            pltpu.VMEM((SA_WIDTH, QBLOCK), BF16),
            pltpu.VMEM((SA_HEADS, QBLOCK), F32),
            pltpu.VMEM((SA_HEADS, QBLOCK), F32),
            pltpu.VMEM((SA_HEADS, SA_DH, QBLOCK), F32),
        ],
        compiler_params=_cparams(("parallel", "arbitrary")),
        name="dsa",
    )(z, z, z, z, z, cnorm, wk, wvt)


def _sb_kernel(q_ref, k_ref, v_ref, uo_ref, o_ref, kt_ref, vb_ref, run_ref, acc_ref):
    QB = QBLOCK
    j = pl.program_id(1)
    S = k_ref.shape[0]
    nb = S // QB
    npair = SB_HEADS // 2
    lane = lax.broadcasted_iota(I32, (QB, QB), 1)
    row = lax.broadcasted_iota(I32, (QB, QB), 0)

    @pl.when(j == 0)
    def _():
        for p in range(npair):
            cols = slice(p * LANES, (p + 1) * LANES)
            kt = k_ref[:, cols].T
            vv = v_ref[:, cols]
            for hh in range(2):
                lo = hh * SB_DH
                ktz = jnp.where((row >= lo) & (row < lo + SB_DH), 1.0, 0.0)
                vz = jnp.where((lane >= lo) & (lane < lo + SB_DH), 1.0, 0.0)
                for kb in range(nb):
                    ks = slice(kb * QB, (kb + 1) * QB)
                    kt_ref[kb, 2 * p + hh] = (kt[:, ks] * ktz).astype(BF16)
                    vb_ref[kb, 2 * p + hh] = (vv[ks, :] * vz).astype(BF16)

    uo = uo_ref[...]
    strict = lane < row
    qs = (q_ref[...] * (SB_DH ** -0.5)).astype(BF16)

    def block(h, kb, diag):
        p = h // 2
        z = _dot(qs[:, p * LANES:(p + 1) * LANES], kt_ref[kb, h])
        l1p = jnp.log(1.0 + jnp.exp(-jnp.abs(z)))
        ls_pos = jnp.minimum(z, 0.0) - l1p
        lk = ls_pos - z
        if diag:
            lk = jnp.where(strict, lk, 0.0)
        lk_hi, lk_lo = _split_bf16(lk)
        cs = _dot(lk_hi, uo) + _dot(lk_lo, uo)
        if diag:
            a = jnp.where(strict, jnp.exp(ls_pos + cs[:, :QB]), 0.0)
            run = cs[:, QB:]
        else:
            run_old = run_ref[h]
            a = jnp.exp(ls_pos + cs[:, :QB] + run_old)
            run = run_old + cs[:, QB:]
        run_ref[h] = run
        return _dot(a.astype(BF16), vb_ref[kb, h]), run

    def all_heads(kb, diag):
        top = None
        for p in range(npair):
            pv0, r0 = block(2 * p, kb, diag)
            pv1, r1 = block(2 * p + 1, kb, diag)
            if diag:
                acc_ref[p] = pv0 + pv1
            else:
                acc_ref[p] += pv0 + pv1
            r = jnp.maximum(r0, r1)
            top = r if top is None else jnp.maximum(top, r)
        return jnp.max(top) > SB_UNDERFLOW

    alive = all_heads(j, True)

    def cond(c):
        return (c[0] >= 0) & c[1]

    def body(c):
        return c[0] - 1, all_heads(c[0], False)

    lax.while_loop(cond, body, (j - 1, alive))
    o_ref[...] = jnp.concatenate([acc_ref[p] for p in range(npair)], axis=1)


def _sb(z, B, S):
    T = B * S
    nb = S // QBLOCK
    u = np.tril(np.ones((QBLOCK, QBLOCK), np.float32), -1)
    uo = jnp.asarray(np.concatenate([u, np.ones((QBLOCK, QBLOCK), np.float32)], 1), BF16)
    qoff = SB_OFF // SB_WIDTH
    return pl.pallas_call(
        _sb_kernel,
        out_shape=jax.ShapeDtypeStruct((T, SB_WIDTH), F32),
        grid=(B, nb),
        in_specs=[
            pl.BlockSpec((QBLOCK, SB_WIDTH), lambda b, j: (b * nb + j, qoff)),
            pl.BlockSpec((S, SB_WIDTH), lambda b, j: (b, qoff + 1)),
            pl.BlockSpec((S, SB_WIDTH), lambda b, j: (b, qoff + 2)),
            pl.BlockSpec(uo.shape, lambda b, j: (0, 0)),
        ],
        out_specs=pl.BlockSpec((QBLOCK, SB_WIDTH), lambda b, j: (b * nb + j, 0)),
        scratch_shapes=[
            pltpu.VMEM((nb, SB_HEADS, LANES, QBLOCK), BF16),
            pltpu.VMEM((nb, SB_HEADS, QBLOCK, LANES), BF16),
            pltpu.VMEM((SB_HEADS, QBLOCK, QBLOCK), F32),
            pltpu.VMEM((SB_HEADS // 2, QBLOCK, LANES), F32),
        ],
        compiler_params=_cparams(("parallel", "arbitrary")),
        name="stickbreak",
    )(z, z, z, uo)


def _merge_kernel(a_ref, b_ref, c_ref, gz_ref, h_ref, wb_ref, wo_ref, gn_ref, ho_ref, v_ref):
    merged = None
    for n, br in enumerate((a_ref, b_ref, c_ref)):
        proj = _dot(br[...].astype(BF16), wb_ref[n])
        gate = _sigmoid(gz_ref[:, n * D_MODEL:(n + 1) * D_MODEL])
        merged = gate * proj if merged is None else merged + gate * proj
    hn = h_ref[...] + _dot(merged.astype(BF16), wo_ref[...])
    ho_ref[...] = hn
    ms = jnp.mean(hn * hn, axis=-1, keepdims=True)
    v_ref[...] = (hn * lax.rsqrt(ms + EPS) * gn_ref[...]).astype(BF16)


def _merge(a, b, c, z, h, wb, wo, gn, tm=256):
    T, D = h.shape
    bw = BRANCH_WIDTH
    return pl.pallas_call(
        _merge_kernel,
        out_shape=(jax.ShapeDtypeStruct((T, D), F32), jax.ShapeDtypeStruct((T, D), BF16)),
        grid=(T // tm,),
        in_specs=[
            pl.BlockSpec((tm, bw), lambda i: (i, 0)),
            pl.BlockSpec((tm, bw), lambda i: (i, 0)),
            pl.BlockSpec((tm, bw), lambda i: (i, 0)),
            pl.BlockSpec((tm, N_BRANCH * D), lambda i: (i, GZ_OFF // (N_BRANCH * D))),
            pl.BlockSpec((tm, D), lambda i: (i, 0)),
            pl.BlockSpec((N_BRANCH, bw, D), lambda i: (0, 0, 0)),
            pl.BlockSpec((D, D), lambda i: (0, 0)),
            pl.BlockSpec((1, D), lambda i: (0, 0)),
        ],
        out_specs=(pl.BlockSpec((tm, D), lambda i: (i, 0)), pl.BlockSpec((tm, D), lambda i: (i, 0))),
        compiler_params=_cparams(("parallel",)),
        name="merge",
    )(a, b, c, z, h, wb, wo, gn)


def _swiglu_kernel(v_ref, h_ref, wg_ref, wu_ref, wd_ref, o_ref, acc_ref):
    f = pl.program_id(1)

    @pl.when(f == 0)
    def _():
        acc_ref[...] = h_ref[...]

    v = v_ref[...]
    gt = _dot(v, wg_ref[...])
    up = _dot(v, wu_ref[...])
    act = (gt * _sigmoid(gt) * up).astype(BF16)
    acc_ref[...] += _dot(act, wd_ref[...])

    @pl.when(f == pl.num_programs(1) - 1)
    def _():
        o_ref[...] = acc_ref[...]


def _swiglu(v, h, wg, wu, wd, tm=1024, tf=256):
    T, D = h.shape
    F = wg.shape[1]
    return pl.pallas_call(
        _swiglu_kernel,
        out_shape=jax.ShapeDtypeStruct((T, D), F32),
        grid=(T // tm, F // tf),
        in_specs=[
            pl.BlockSpec((tm, D), lambda i, f: (i, 0)),
            pl.BlockSpec((tm, D), lambda i, f: (i, 0)),
            pl.BlockSpec((D, tf), lambda i, f: (0, f)),
            pl.BlockSpec((D, tf), lambda i, f: (0, f)),
            pl.BlockSpec((tf, D), lambda i, f: (f, 0)),
        ],
        out_specs=pl.BlockSpec((tm, D), lambda i, f: (i, 0)),
        scratch_shapes=[pltpu.VMEM((tm, D), F32)],
        compiler_params=_cparams(("parallel", "arbitrary")),
        name="swiglu",
    )(v, h, wg, wu, wd)


MOE_TILE = 1024
MOE_CHUNK = 288


def _router_kernel(h_ref, gn_ref, r_ref, us_ref, cw_ref, cwt_ref, rkt_ref):
    hh = h_ref[...]
    ms = jnp.mean(hh * hh, axis=-1, keepdims=True)
    vf = hh * lax.rsqrt(ms + EPS) * gn_ref[...]
    v_hi, v_lo = _split_bf16(vf)
    r_hi, r_lo = _split_bf16(r_ref[...])
    logits = _dot(v_hi, r_hi) + _dot(v_hi, r_lo) + _dot(v_lo, r_hi)
    lane = lax.broadcasted_iota(I32, logits.shape, 1)
    logits = jnp.where(lane < N_EXPERTS, logits, -jnp.inf)
    m1 = jnp.max(logits, axis=-1, keepdims=True)
    i1 = jnp.min(jnp.where(logits == m1, lane, LANES), axis=-1, keepdims=True)
    rest = jnp.where(lane == i1, -jnp.inf, logits)
    m2 = jnp.max(rest, axis=-1, keepdims=True)
    i2 = jnp.min(jnp.where(rest == m2, lane, LANES), axis=-1, keepdims=True)
    e2 = jnp.exp(m2 - m1)
    w1 = 1.0 / (1.0 + e2)
    cw = jnp.where(lane == i1, w1, 0.0) + jnp.where(lane == i2, e2 * w1, 0.0)
    cw_ref[...] = cw
    cwt = cw.T
    cwt_ref[...] = cwt[0:N_EXPERTS]
    sel = jnp.where(cwt != 0.0, 1.0, 0.0).astype(BF16)
    rkt_ref[...] = _dot(sel, us_ref[...])[0:N_EXPERTS]


def _router(h, gn, router, tm=MOE_TILE):
    T, D = h.shape
    us = jnp.asarray(np.triu(np.ones((tm, tm), np.float32), 1), BF16)
    return pl.pallas_call(
        _router_kernel,
        out_shape=(jax.ShapeDtypeStruct((T, LANES), F32),
                   jax.ShapeDtypeStruct((N_EXPERTS, T), F32),
                   jax.ShapeDtypeStruct((N_EXPERTS, T), F32)),
        grid=(T // tm,),
        in_specs=[
            pl.BlockSpec((tm, D), lambda i: (i, 0)),
            pl.BlockSpec((1, D), lambda i: (0, 0)),
            pl.BlockSpec((D, LANES), lambda i: (0, 0)),
            pl.BlockSpec((tm, tm), lambda i: (0, 0)),
        ],
        out_specs=(pl.BlockSpec((tm, LANES), lambda i: (i, 0)),
                   pl.BlockSpec((N_EXPERTS, tm), lambda i: (0, i)),
                   pl.BlockSpec((N_EXPERTS, tm), lambda i: (0, i))),
        compiler_params=_cparams(("parallel",)),
        name="router",
    )(h, gn, router, us)


def _moe_kernel(v_ref, h_ref, cw_ref, cwt_ref, rkt_ref, wg_ref, wu_ref, wd_ref, o_ref,
                xg_ref, ce_ref, ya_ref):
    e = pl.program_id(1)
    f = pl.program_id(2)
    nf = pl.num_programs(2)
    tm = v_ref.shape[0]
    CH = MOE_CHUNK

    @pl.when((e == 0) & (f == 0))
    def _():
        o_ref[...] = h_ref[...]

    ce_row = cwt_ref[pl.ds(e, 1), :]
    routed = ce_row != 0.0
    slot_row = jnp.where(routed, rkt_ref[pl.ds(e, 1), :], -1.0)
    count = jnp.sum(jnp.where(routed, 1, 0).astype(I32))
    n_chunks = (count + (CH - 1)) // CH
    riota = lax.broadcasted_iota(I32, (CH, tm), 0).astype(F32)

    def pick_rows(i):
        off = (i * CH).astype(F32)
        return jnp.where(slot_row == riota + off, 1.0, 0.0).astype(BF16)

    @pl.when(f == 0)
    def _():
        c_hi, c_lo = _split_bf16(cw_ref[...])
        lane = lax.broadcasted_iota(I32, (CH, LANES), 1)

        def gather(i, carry):
            pick = pick_rows(i)
            xg_ref[i] = _dot(pick, v_ref[...]).astype(BF16)
            cg = _dot(pick, c_hi) + _dot(pick, c_lo)
            ce = jnp.sum(jnp.where(lane == e, cg, 0.0), axis=-1, keepdims=True)
            ce_ref[i] = jnp.broadcast_to(ce, (CH, LANES))
            ya_ref[i] = jnp.zeros((CH, ya_ref.shape[2]), F32)
            return carry

        lax.fori_loop(0, n_chunks, gather, 0)

    def expert(i, carry):
        xg = xg_ref[i]
        gt = _dot(xg, wg_ref[0])
        up = _dot(xg, wu_ref[0])
        act = (gt * _sigmoid(gt) * up * ce_ref[i][:, 0:1]).astype(BF16)
        ya_ref[i] += _dot(act, wd_ref[0])
        return carry

    lax.fori_loop(0, n_chunks, expert, 0)

    @pl.when(f == nf - 1)
    def _():
        def scatter(i, carry):
            o_ref[...] += _dot_tn(pick_rows(i), ya_ref[i].astype(BF16))
            return carry

        lax.fori_loop(0, n_chunks, scatter, 0)


def _moe(v, h, cw, cwt, rkt, wg, wu, wd, tm=MOE_TILE, fsplit=2):
    T, D = h.shape
    E, _, F = wg.shape
    tf = F // fsplit
    max_chunks = -(-tm // MOE_CHUNK)
    return pl.pallas_call(
        _moe_kernel,
        out_shape=jax.ShapeDtypeStruct((T, D), F32),
        grid=(T // tm, E, fsplit),
        in_specs=[
            pl.BlockSpec((tm, D), lambda i, e, f: (i, 0), pipeline_mode=pl.Buffered(1)),
            pl.BlockSpec((tm, D), lambda i, e, f: (i, 0), pipeline_mode=pl.Buffered(1)),
            pl.BlockSpec((tm, LANES), lambda i, e, f: (i, 0), pipeline_mode=pl.Buffered(1)),
            pl.BlockSpec((N_EXPERTS, tm), lambda i, e, f: (0, i)),
            pl.BlockSpec((N_EXPERTS, tm), lambda i, e, f: (0, i)),
            pl.BlockSpec((1, D, tf), lambda i, e, f: (e, 0, f)),
            pl.BlockSpec((1, D, tf), lambda i, e, f: (e, 0, f)),
            pl.BlockSpec((1, tf, D), lambda i, e, f: (e, f, 0)),
        ],
        out_specs=pl.BlockSpec((tm, D), lambda i, e, f: (i, 0)),
        scratch_shapes=[
            pltpu.VMEM((max_chunks, MOE_CHUNK, D), BF16),
            pltpu.VMEM((max_chunks, MOE_CHUNK, LANES), F32),
            pltpu.VMEM((max_chunks, MOE_CHUNK, D), F32),
        ],
        compiler_params=_cparams(("parallel", "arbitrary", "arbitrary")),
        name="moe",
    )(v, h, cw, cwt, rkt, wg, wu, wd)


def _ple_kernel(final, h_ref, p_ref, gn_ref, wg_ref, wp_ref, fn_ref, o_ref):
    h = h_ref[...]
    ms = jnp.mean(h * h, axis=-1, keepdims=True)
    u = (h * lax.rsqrt(ms + EPS) * gn_ref[...]).astype(BF16)
    gate = _sigmoid(_dot(u, wg_ref[...]))
    hn = h + gate * _dot(p_ref[...].astype(BF16), wp_ref[...])
    if final:
        ms = jnp.mean(hn * hn, axis=-1, keepdims=True)
        hn = hn * lax.rsqrt(ms + EPS) * fn_ref[...]
    o_ref[...] = hn


def _ple(h, p, gn, wg, wp, fn, final, tm=512):
    T, D = h.shape
    return pl.pallas_call(
        functools.partial(_ple_kernel, final),
        out_shape=jax.ShapeDtypeStruct((T, D), F32),
        grid=(T // tm,),
        in_specs=[
            pl.BlockSpec((tm, D), lambda i: (i, 0)),
            pl.BlockSpec((tm, PLE_DIM), lambda i: (i, 0)),
            pl.BlockSpec((1, D), lambda i: (0, 0)),
            pl.BlockSpec((D, D), lambda i: (0, 0)),
            pl.BlockSpec((PLE_DIM, D), lambda i: (0, 0)),
            pl.BlockSpec((1, D), lambda i: (0, 0)),
        ],
        out_specs=pl.BlockSpec((tm, D), lambda i: (i, 0)),
        compiler_params=_cparams(("parallel",)),
        name="ple",
    )(h, p, gn, wg, wp, fn)


def _reorder_w_in(w):
    o_sq = 4 * HG_WIDTH
    o_sc = o_sq + SA_WIDTH
    o_iq = o_sc + SA_LATENT
    o_ik = o_iq + IDX_HEADS * IDX_DIM
    o_bq = o_ik + IDX_DIM + IDX_HEADS
    o_gz = o_bq + 3 * SB_WIDTH
    pad = jnp.zeros((w.shape[0], LANES - IDX_DIM - IDX_HEADS), w.dtype)
    return jnp.concatenate([
        w[:, o_gz:], w[:, :o_sq], w[:, o_sq:o_sc], w[:, o_iq:o_ik], w[:, o_sc:o_iq],
        w[:, o_ik:o_bq], pad, w[:, o_bq:o_gz]], axis=1).astype(BF16)


def kernel(x, p, lb_param, norm_mix, w_in, hg_onorm, sa_cnorm, sa_wk, sa_wv, w_branch, w_out,
           norm_ffn, dense_wg, dense_wu, dense_wd, moe_router, moe_wg, moe_wu, moe_wd, norm_ple,
           ple_gate, ple_proj, norm_final):
    B, S, D = x.shape
    T = B * S
    depth = w_in.shape[0]
    lb_sm = jax.nn.softmax(lb_param.astype(F32), axis=0)
    lb_all = jnp.cumsum(lb_sm, axis=0) - lb_sm[0:1]
    h = x.reshape(T, D)
    for i in range(depth):
        z = _inproj(h, norm_mix[i][None, :], _reorder_w_in(w_in[i]))
        a = _hgrn(z, lb_all[i][None, :], hg_onorm[i][None, :], B, S)
        b = _dsa(z, sa_cnorm[i][None, :], sa_wk[i].astype(BF16), sa_wv[i].T.astype(BF16), B, S)
        c = _sb(z, B, S)
        h, v = _merge(a, b, c, z, h, w_branch[i].astype(BF16), w_out[i].astype(BF16),
                      norm_ffn[i][None, :])
        jj = i // 2
        if i % 2 == 0:
            h = _swiglu(v, h, dense_wg[jj].astype(BF16), dense_wu[jj].astype(BF16),
                        dense_wd[jj].astype(BF16))
        else:
            router = jnp.pad(moe_router[jj], ((0, 0), (0, LANES - N_EXPERTS)))
            cw, cwt, rkt = _router(h, norm_ffn[i][None, :], router)
            h = _moe(v, h, cw, cwt, rkt, moe_wg[jj].astype(BF16), moe_wu[jj].astype(BF16),
                     moe_wd[jj].astype(BF16))
        h = _ple(h, p[i].reshape(T, PLE_DIM), norm_ple[i][None, :], ple_gate[i].astype(BF16),
                 ple_proj[i].astype(BF16), norm_final[None, :], i == depth - 1)
    return h.reshape(B, S, D)
```

```python
import functools

import numpy as np
import jax
import jax.numpy as jnp
from jax import lax
from jax.experimental import pallas as pl
from jax.experimental.pallas import tpu as pltpu

F32 = jnp.float32
BF16 = jnp.bfloat16
I32 = jnp.int32

D_MODEL = 1024
CHUNK = 64
QBLOCK = 128
HG_HEADS = 4
HG_DK = 128
HG_WIDTH = HG_HEADS * HG_DK
SA_HEADS = 8
SA_DH = 64
SA_WIDTH = SA_HEADS * SA_DH
SA_LATENT = 128
IDX_HEADS = 4
IDX_DIM = 64
TOPK_MAX = 256
SB_HEADS = 8
SB_DH = 64
SB_WIDTH = SB_HEADS * SB_DH
N_BRANCH = 3
BRANCH_WIDTH = 512
N_EXPERTS = 8
PLE_DIM = 256
EPS = 1e-6

LANES = 128

GZ_OFF = 0
HG_OFF = 3 * D_MODEL
SQ_OFF = HG_OFF + 4 * HG_WIDTH
IQ_OFF = SQ_OFF + SA_WIDTH
SC_OFF = IQ_OFF + IDX_HEADS * IDX_DIM
IKW_OFF = SC_OFF + SA_LATENT
SB_OFF = IKW_OFF + LANES
ZW = SB_OFF + 3 * SB_WIDTH

VMEM_LIMIT = 56 * 1024 * 1024

NEG_BIG = -1e30
INT_MIN = -2147483648
SB_UNDERFLOW = -104.0


def _cparams(sem):
    return pltpu.CompilerParams(dimension_semantics=sem, vmem_limit_bytes=VMEM_LIMIT)


def _sigmoid(x):
    e = jnp.exp(-jnp.abs(x))
    inv = 1.0 / (1.0 + e)
    return jnp.where(x >= 0, inv, e * inv)


def _dot(a, b):
    return jnp.dot(a, b, preferred_element_type=F32)


def _dot_nt(a, b):
    return lax.dot_general(a, b, (((1,), (1,)), ((), ())), preferred_element_type=F32)


def _dot_tn(a, b):
    return lax.dot_general(a, b, (((0,), (0,)), ((), ())), preferred_element_type=F32)


def _split_bf16(x):
    hi = x.astype(BF16)
    lo = (x - hi.astype(F32)).astype(BF16)
    return hi, lo


def _inproj_kernel(x_ref, g_ref, w_ref, o_ref, u_ref):
    @pl.when(pl.program_id(1) == 0)
    def _():
        x = x_ref[...]
        ms = jnp.mean(x * x, axis=-1, keepdims=True)
        u_ref[...] = (x * lax.rsqrt(ms + EPS) * g_ref[...]).astype(BF16)

    o_ref[...] = _dot(u_ref[...], w_ref[...])


def _inproj(h, gain, w, tm=1024, tn=1536):
    T, D = h.shape
    N = w.shape[1]
    return pl.pallas_call(
        _inproj_kernel,
        out_shape=jax.ShapeDtypeStruct((T, N), F32),
        grid=(T // tm, N // tn),
        in_specs=[
            pl.BlockSpec((tm, D), lambda i, j: (i, 0)),
            pl.BlockSpec((1, D), lambda i, j: (0, 0)),
            pl.BlockSpec((D, tn), lambda i, j: (0, j)),
        ],
        out_specs=pl.BlockSpec((tm, tn), lambda i, j: (i, j)),
        scratch_shapes=[pltpu.VMEM((tm, D), BF16)],
        compiler_params=_cparams(("parallel", "arbitrary")),
        name="inproj",
    )(h, gain, w)


_HG_LEVELS = (64, 32, 16, 8, 4, 2)


def _hgrn_consts():
    C = CHUNK
    blocks, masks = [], []
    for n in _HG_LEVELS:
        half = n // 2
        L = np.zeros((C, C), np.float32)
        M = np.zeros((C, C), np.float32)
        for t in range(C):
            base = (t // n) * n
            mid = base + half
            if t >= mid:
                L[t, mid:t + 1] = 1.0
                M[t, base:mid] = 1.0
            else:
                L[t, t + 1:mid] = 1.0
        blocks.append(L)
        masks.append(M)
    masks.append(np.eye(C, dtype=np.float32))
    blocks.append(np.tril(np.ones((C, C), np.float32)))
    blocks.append(np.triu(np.ones((C, C), np.float32), 1))
    return np.concatenate(blocks, 0), np.stack(masks)


def _hgrn_kernel(q_ref, f_ref, i_ref, g_ref, lb_ref, on_ref, lc_ref, mc_ref, o_ref, st_ref):
    C = CHUNK
    nlev = len(_HG_LEVELS)

    @pl.when(pl.program_id(1) == 0)
    def _():
        st_ref[...] = jnp.zeros(st_ref.shape, F32)

    lc = lc_ref[...]
    onorm = on_ref[...]
    n_chunks = q_ref.shape[0] // C

    def chunk(c, carry):
        r0 = pl.multiple_of(c * C, C)
        rows = pl.ds(r0, C)
        for h in range(HG_HEADS):
            cs = slice(h * HG_DK, (h + 1) * HG_DK)
            qin = q_ref[rows, cs]
            fpre = f_ref[rows, cs]
            v = i_ref[rows, cs]
            g = g_ref[rows, cs]
            lb = lb_ref[:, cs]
            e = jnp.exp(-jnp.abs(fpre))
            inv = 1.0 / (1.0 + e)
            sg_pos = jnp.where(fpre >= 0, inv, e * inv)
            sg_neg = jnp.where(fpre >= 0, e * inv, inv)
            lf = jnp.log(lb + (1.0 - lb) * sg_pos)
            k = (1.0 - lb) * sg_neg
            q = qin * _sigmoid(qin)
            lf_hi, lf_lo = _split_bf16(lf)
            dsum = _dot(lc, lf_hi) + _dot(lc, lf_lo)
            ex = jnp.exp(dsum)
            scores = _dot_nt(q.astype(BF16), k.astype(BF16)) * mc_ref[nlev]
            for l in range(nlev):
                el = ex[l * C:(l + 1) * C]
                s_l = _dot_nt((q * el).astype(BF16), (k * el).astype(BF16))
                scores = scores + s_l * mc_ref[l]
            eb = ex[nlev * C:(nlev + 1) * C]
            er = ex[(nlev + 1) * C:(nlev + 2) * C]
            st = st_ref[h]
            vb = v.astype(BF16)
            o = _dot(scores.astype(BF16), vb) + _dot_nt((q * eb).astype(BF16), st.astype(BF16))
            st_ref[h] = st * eb[C - 1:C, :] + _dot_tn(vb, (k * er).astype(BF16))
            ms = jnp.mean(o * o, axis=-1, keepdims=True)
            on = o * lax.rsqrt(ms + EPS) * onorm
            o_ref[rows, cs] = on * (g * _sigmoid(g))
        return carry

    lax.fori_loop(0, n_chunks, chunk, 0)


def _hgrn(z, lb, onorm, B, S, sblk=1024):
    T = B * S
    sblk = min(sblk, S)
    ns = S // sblk
    lc_np, mc_np = _hgrn_consts()
    lc = jnp.asarray(lc_np, BF16)
    mc = jnp.asarray(mc_np, F32)
    cb = HG_OFF // HG_WIDTH

    def zspec(k):
        return pl.BlockSpec((sblk, HG_WIDTH), lambda b, s, k=k: (b * ns + s, cb + k))

    return pl.pallas_call(
        _hgrn_kernel,
        out_shape=jax.ShapeDtypeStruct((T, HG_WIDTH), F32),
        grid=(B, ns),
        in_specs=[
            zspec(0), zspec(1), zspec(2), zspec(3),
            pl.BlockSpec((1, HG_WIDTH), lambda b, s: (0, 0)),
            pl.BlockSpec((1, HG_DK), lambda b, s: (0, 0)),
            pl.BlockSpec(lc.shape, lambda b, s: (0, 0)),
            pl.BlockSpec(mc.shape, lambda b, s: (0, 0, 0)),
        ],
        out_specs=pl.BlockSpec((sblk, HG_WIDTH), lambda b, s: (b * ns + s, 0)),
        scratch_shapes=[pltpu.VMEM((HG_HEADS, HG_DK, HG_DK), F32)],
        compiler_params=_cparams(("parallel", "arbitrary")),
        name="hgrn2",
    )(z, z, z, z, lb, onorm, lc, mc)


def _dsa_kernel(topk, sq_ref, iq_ref, iwq_ref, sc_ref, ikw_ref, cn_ref, wk_ref, wvt_ref,
                o_ref, kk_ref, vt_ref, ik_ref, key_ref, plane_ref, qt_ref, m_ref, l_ref, acc_ref):
    QB = QBLOCK
    j = pl.program_id(1)
    nkb = j + 1
    S = sc_ref.shape[0]
    nb = S // QB
    idx_bits = (S - 1).bit_length()

    @pl.when(j == 0)
    def _():
        c = sc_ref[...]
        ms = jnp.mean(c * c, axis=-1, keepdims=True)
        cb = (c * lax.rsqrt(ms + EPS) * cn_ref[...]).astype(BF16)
        kfull = _dot(cb, wk_ref[...]).astype(BF16)
        vtfull = _dot_nt(wvt_ref[...], cb).astype(BF16)
        ikfull = ikw_ref[:, 0:IDX_DIM].astype(BF16)
        for kb in range(nb):
            kk_ref[kb] = kfull[kb * QB:(kb + 1) * QB]
            vt_ref[kb] = vtfull[:, kb * QB:(kb + 1) * QB]
            ik_ref[kb] = ikfull[kb * QB:(kb + 1) * QB]

    qt_ref[...] = (sq_ref[...] * (SA_DH ** -0.5)).T.astype(BF16)
    iqt = iq_ref[...].T.astype(BF16)
    iwt = iwq_ref[...].T * (IDX_HEADS ** -0.5)

    row = lax.broadcasted_iota(I32, (QB, QB), 0)
    lane = lax.broadcasted_iota(I32, (QB, QB), 1)
    diag_bad = (row >= CHUNK) & (lane < CHUNK)

    n_pairs = (nkb + 1) // 2

    def score_block(kb):
        ikb = ik_ref[kb]
        acc = jnp.zeros((QB, QB), F32)
        for h in range(IDX_HEADS):
            s_h = _dot(ikb, iqt[h * IDX_DIM:(h + 1) * IDX_DIM])
            acc = acc + jnp.maximum(s_h, 0.0) * iwt[IDX_DIM + h:IDX_DIM + h + 1, :]
        acc = acc + 0.0
        acc = jnp.where((kb == j) & diag_bad, -jnp.inf, acc)
        bits = pltpu.bitcast(acc, I32)
        key_ref[kb] = jnp.where(bits < 0, bits ^ jnp.int32(0x7FFFFFFF), bits)

    def score_pair(g, carry):
        score_block(2 * g)
        score_block(2 * g + 1)
        return carry

    lax.fori_loop(0, n_pairs, score_pair, 0)

    @pl.when(j + 1 < 2 * n_pairs)
    def _():
        key_ref[j + 1] = jnp.full((QB, QB), INT_MIN, I32)

    def build_planes(g, carry):
        a = []
        for half in range(2):
            blk = key_ref[2 * g + half] ^ jnp.int32(INT_MIN)
            a.extend(blk[r * 8:(r + 1) * 8, :] for r in range(QB // 8))
        for sh, m in ((16, 0x0000FFFF), (8, 0x00FF00FF), (4, 0x0F0F0F0F), (2, 0x33333333),
                      (1, 0x55555555)):
            for k in range(32):
                if k & sh == 0:
                    t = (lax.shift_right_logical(a[k], jnp.int32(sh)) ^ a[k + sh]) & jnp.int32(m)
                    a[k + sh] = a[k + sh] ^ t
                    a[k] = a[k] ^ lax.shift_left(t, jnp.int32(sh))
        rows = pl.ds(pl.multiple_of(g * 8, 8), 8)
        for b in range(32):
            plane_ref[b, rows, :] = a[b]
        return carry

    def search(_):
        lax.fori_loop(0, n_pairs, build_planes, 0)
        wrow = lax.broadcasted_iota(I32, (nb // 2 * 8, QB), 0)
        cand = jnp.where((wrow >> 3) < n_pairs, jnp.int32(-1), jnp.int32(0))
        above = jnp.zeros((1, QB), I32)
        u = jnp.zeros((1, QB), I32)
        for b in range(31, -1, -1):
            ones = cand & plane_ref[b]
            cnt1 = jnp.sum(lax.population_count(ones), axis=0, keepdims=True)
            take = (above + cnt1) >= topk
            cand = jnp.where(take, ones, cand ^ ones)
            above = jnp.where(take, above, above + cnt1)
            u = u | jnp.where(take, jnp.int32(INT_MIN if b == 31 else 1 << b), jnp.int32(0))
        thr = u ^ jnp.int32(INT_MIN)
        need = topk - above
        n_tied = jnp.sum(lax.population_count(cand), axis=0, keepdims=True)

        def break_ties(_):
            base = (wrow >> 3) * (2 * QB) + (wrow & 7)

            def ibit(i, jp):
                c = jp + lax.shift_left(jnp.int32(1), idx_bits - 1 - i)
                nbits = jnp.clip((c - base + 7) >> 3, 0, 32)
                below = jnp.where(nbits >= 32, jnp.int32(-1),
                                  lax.shift_left(jnp.int32(1), nbits) - 1)
                cnt = jnp.sum(lax.population_count(cand & below), axis=0, keepdims=True)
                return jnp.where(cnt < need, c, jp)

            return lax.fori_loop(0, idx_bits, ibit, jnp.zeros((1, QB), I32))

        def keep_ties(_):
            return jnp.full((1, QB), S, I32)

        jp = lax.cond(jnp.max(n_tied - need) > 0, break_ties, keep_ties, 0)
        return thr, jp

    def take_all(_):
        return (jnp.full((1, QB), INT_MIN, I32), jnp.full((1, QB), -1, I32))

    thr, jp = lax.cond(nkb * QB > topk, search, take_all, 0)

    m_ref[...] = jnp.full(m_ref.shape, NEG_BIG, F32)
    l_ref[...] = jnp.zeros(l_ref.shape, F32)
    acc_ref[...] = jnp.zeros(acc_ref.shape, F32)

    def attend_block(kb):
        key = key_ref[kb]
        kidx = row + kb * QB
        sel = (key > thr) | ((key == thr) & (kidx <= jp))
        hidden = ((kidx >= j * QB + CHUNK) & (lane < CHUNK)) | (kidx >= nkb * QB)
        bias = jnp.where(sel & jnp.logical_not(hidden), 0.0, -jnp.inf)
        kblk = kk_ref[kb]
        vtb = vt_ref[kb]
        for h in range(SA_HEADS):
            lg = _dot(kblk, qt_ref[h * SA_DH:(h + 1) * SA_DH, :]) + bias
            m_old = m_ref[h:h + 1, :]
            m_new = jnp.maximum(m_old, jnp.max(lg, axis=0, keepdims=True))
            alpha = jnp.exp(m_old - m_new)
            p = jnp.exp(lg - m_new)
            l_ref[h:h + 1, :] = alpha * l_ref[h:h + 1, :] + jnp.sum(p, axis=0, keepdims=True)
            acc_ref[h] = acc_ref[h] * alpha + _dot(vtb, p.astype(BF16))
            m_ref[h:h + 1, :] = m_new

    def attend(g, carry):
        attend_block(2 * g)
        attend_block(2 * g + 1)
        return carry

    lax.fori_loop(0, n_pairs, attend, 0)

    outs = []
    for h in range(SA_HEADS):
        outs.append(acc_ref[h] * (1.0 / l_ref[h:h + 1, :]))
    o_ref[...] = jnp.concatenate(outs, axis=0).T


def _dsa(z, cnorm, wk, wvt, B, S):
    T = B * S
    nb = S // QBLOCK
    topk = min(TOPK_MAX, S // 4)
    assert nb % 2 == 0
    kern = functools.partial(_dsa_kernel, topk)
    return pl.pallas_call(
        kern,
        out_shape=jax.ShapeDtypeStruct((T, SA_WIDTH), F32),
        grid=(B, nb),
        in_specs=[
            pl.BlockSpec((QBLOCK, SA_WIDTH), lambda b, j: (b * nb + j, SQ_OFF // SA_WIDTH)),
            pl.BlockSpec((QBLOCK, IDX_HEADS * IDX_DIM),
                         lambda b, j: (b * nb + j, IQ_OFF // (IDX_HEADS * IDX_DIM))),
            pl.BlockSpec((QBLOCK, LANES), lambda b, j: (b * nb + j, IKW_OFF // LANES)),
            pl.BlockSpec((S, SA_LATENT), lambda b, j: (b, SC_OFF // SA_LATENT)),
            pl.BlockSpec((S, LANES), lambda b, j: (b, IKW_OFF // LANES)),
            pl.BlockSpec((1, SA_LATENT), lambda b, j: (0, 0)),
            pl.BlockSpec((SA_LATENT, SA_DH), lambda b, j: (0, 0)),
            pl.BlockSpec((SA_DH, SA_LATENT), lambda b, j: (0, 0)),
        ],
        out_specs=pl.BlockSpec((QBLOCK, SA_WIDTH), lambda b, j: (b * nb + j, 0)),
        scratch_shapes=[
            pltpu.VMEM((nb, QBLOCK, SA_DH), BF16),
            pltpu.VMEM((nb, SA_DH, QBLOCK), BF16),
            pltpu.VMEM((nb, QBLOCK, IDX_DIM), BF16),
            pltpu.VMEM((nb, QBLOCK, QBLOCK), I32),
            pltpu.VMEM((32, nb // 2 * 8, QBLOCK), I32),
            pltpu.VMEM((SA_WIDTH, QBLOCK), BF16),
            pltpu.VMEM((SA_HEADS, QBLOCK), F32),
            pltpu.VMEM((SA_HEADS, QBLOCK), F32),
            pltpu.VMEM((SA_HEADS, SA_DH, QBLOCK), F32),
        ],
        compiler_params=_cparams(("parallel", "arbitrary")),
        name="dsa",
    )(z, z, z, z, z, cnorm, wk, wvt)


def _sb_kernel(q_ref, k_ref, v_ref, uo_ref, o_ref, kt_ref, vb_ref, run_ref, acc_ref,
               ls_ref, lkh_ref, lkl_ref, tot_ref, a_ref):
    QB = QBLOCK
    j = pl.program_id(1)
    S = k_ref.shape[0]
    nb = S // QB
    npair = SB_HEADS // 2
    lane = lax.broadcasted_iota(I32, (QB, QB), 1)
    row = lax.broadcasted_iota(I32, (QB, QB), 0)

    @pl.when(j == 0)
    def _():
        for p in range(npair):
            cols = slice(p * LANES, (p + 1) * LANES)
            kt = k_ref[:, cols].T
            vv = v_ref[:, cols]
            for hh in range(2):
                lo = hh * SB_DH
                ktz = jnp.where((row >= lo) & (row < lo + SB_DH), 1.0, 0.0)
                vz = jnp.where((lane >= lo) & (lane < lo + SB_DH), 1.0, 0.0)
                for kb in range(nb):
                    ks = slice(kb * QB, (kb + 1) * QB)
                    kt_ref[kb, 2 * p + hh] = (kt[:, ks] * ktz).astype(BF16)
                    vb_ref[kb, 2 * p + hh] = (vv[ks, :] * vz).astype(BF16)

    uo = uo_ref[...]
    strict = lane < row
    qs = (q_ref[...] * (SB_DH ** -0.5)).astype(BF16)

    def all_heads(kbs, diag):
        nk = len(kbs)
        for h in range(SB_HEADS):
            p = h // 2
            for i, kb in enumerate(kbs):
                s = h * nk + i
                z = _dot(qs[:, p * LANES:(p + 1) * LANES], kt_ref[kb, h])
                l1p = jnp.log(1.0 + jnp.exp(-jnp.abs(z)))
                ls_pos = jnp.minimum(z, 0.0) - l1p
                lk = ls_pos - z
                if diag:
                    lk = jnp.where(strict, lk, 0.0)
                lk_hi, lk_lo = _split_bf16(lk)
                ls_ref[s] = ls_pos
                lkh_ref[s] = lk_hi
                lkl_ref[s] = lk_lo
                tot_ref[s] = jnp.sum(lk, axis=1, keepdims=True)
        top = None
        for h in range(SB_HEADS):
            run = None if diag else run_ref[h]
            for i in range(nk):
                s = h * nk + i
                after = _dot(lkh_ref[s], uo) + _dot(lkl_ref[s], uo)
                if diag:
                    a = jnp.where(strict, jnp.exp(ls_ref[s] + after), 0.0)
                    run = tot_ref[s]
                else:
                    a = jnp.exp(ls_ref[s] + after + run)
                    run = run + tot_ref[s]
                a_ref[s] = a.astype(BF16)
            run_ref[h] = run
            top = run if top is None else jnp.maximum(top, run)
        for p in range(npair):
            pv = None
            for h in (2 * p, 2 * p + 1):
                for i, kb in enumerate(kbs):
                    out = _dot(a_ref[h * nk + i], vb_ref[kb, h])
                    pv = out if pv is None else pv + out
            if diag:
                acc_ref[p] = pv
            else:
                acc_ref[p] += pv
        return jnp.max(top) > SB_UNDERFLOW

    alive = all_heads([j], True)

    def cond(c):
        return (c[0] >= 1) & c[1]

    def body(c):
        return c[0] - 2, all_heads([c[0], c[0] - 1], False)

    kb_left, alive = lax.while_loop(cond, body, (j - 1, alive))

    @pl.when((kb_left == 0) & alive)
    def _():
        all_heads([0], False)

    o_ref[...] = jnp.concatenate([acc_ref[p] for p in range(npair)], axis=1)


def _sb(z, B, S):
    T = B * S
    nb = S // QBLOCK
    u = np.tril(np.ones((QBLOCK, QBLOCK), np.float32), -1)
    uo = jnp.asarray(u, BF16)
    qoff = SB_OFF // SB_WIDTH
    return pl.pallas_call(
        _sb_kernel,
        out_shape=jax.ShapeDtypeStruct((T, SB_WIDTH), F32),
        grid=(B, nb),
        in_specs=[
            pl.BlockSpec((QBLOCK, SB_WIDTH), lambda b, j: (b * nb + j, qoff)),
            pl.BlockSpec((S, SB_WIDTH), lambda b, j: (b, qoff + 1)),
            pl.BlockSpec((S, SB_WIDTH), lambda b, j: (b, qoff + 2)),
            pl.BlockSpec(uo.shape, lambda b, j: (0, 0)),
        ],
        out_specs=pl.BlockSpec((QBLOCK, SB_WIDTH), lambda b, j: (b * nb + j, 0)),
        scratch_shapes=[
            pltpu.VMEM((nb, SB_HEADS, LANES, QBLOCK), BF16),
            pltpu.VMEM((nb, SB_HEADS, QBLOCK, LANES), BF16),
            pltpu.VMEM((SB_HEADS, QBLOCK, 1), F32),
            pltpu.VMEM((SB_HEADS // 2, QBLOCK, LANES), F32),
            pltpu.VMEM((2 * SB_HEADS, QBLOCK, QBLOCK), F32),
            pltpu.VMEM((2 * SB_HEADS, QBLOCK, QBLOCK), BF16),
            pltpu.VMEM((2 * SB_HEADS, QBLOCK, QBLOCK), BF16),
            pltpu.VMEM((2 * SB_HEADS, QBLOCK, 1), F32),
            pltpu.VMEM((2 * SB_HEADS, QBLOCK, QBLOCK), BF16),
        ],
        compiler_params=_cparams(("parallel", "arbitrary")),
        name="stickbreak",
    )(z, z, z, uo)


def _merge_kernel(a_ref, b_ref, c_ref, gz_ref, h_ref, wb_ref, wo_ref, gn_ref, ho_ref, v_ref):
    merged = None
    for n, br in enumerate((a_ref, b_ref, c_ref)):
        proj = _dot(br[...].astype(BF16), wb_ref[n])
        gate = _sigmoid(gz_ref[:, n * D_MODEL:(n + 1) * D_MODEL])
        merged = gate * proj if merged is None else merged + gate * proj
    hn = h_ref[...] + _dot(merged.astype(BF16), wo_ref[...])
    ho_ref[...] = hn
    ms = jnp.mean(hn * hn, axis=-1, keepdims=True)
    v_ref[...] = (hn * lax.rsqrt(ms + EPS) * gn_ref[...]).astype(BF16)


def _merge(a, b, c, z, h, wb, wo, gn, tm=256):
    T, D = h.shape
    bw = BRANCH_WIDTH
    return pl.pallas_call(
        _merge_kernel,
        out_shape=(jax.ShapeDtypeStruct((T, D), F32), jax.ShapeDtypeStruct((T, D), BF16)),
        grid=(T // tm,),
        in_specs=[
            pl.BlockSpec((tm, bw), lambda i: (i, 0)),
            pl.BlockSpec((tm, bw), lambda i: (i, 0)),
            pl.BlockSpec((tm, bw), lambda i: (i, 0)),
            pl.BlockSpec((tm, N_BRANCH * D), lambda i: (i, GZ_OFF // (N_BRANCH * D))),
            pl.BlockSpec((tm, D), lambda i: (i, 0)),
            pl.BlockSpec((N_BRANCH, bw, D), lambda i: (0, 0, 0)),
            pl.BlockSpec((D, D), lambda i: (0, 0)),
            pl.BlockSpec((1, D), lambda i: (0, 0)),
        ],
        out_specs=(pl.BlockSpec((tm, D), lambda i: (i, 0)), pl.BlockSpec((tm, D), lambda i: (i, 0))),
        compiler_params=_cparams(("parallel",)),
        name="merge",
    )(a, b, c, z, h, wb, wo, gn)


def _swiglu_kernel(v_ref, h_ref, wg_ref, wu_ref, wd_ref, o_ref, acc_ref):
    f = pl.program_id(1)

    @pl.when(f == 0)
    def _():
        acc_ref[...] = h_ref[...]

    v = v_ref[...]
    gt = _dot(v, wg_ref[...])
    up = _dot(v, wu_ref[...])
    act = (gt * _sigmoid(gt) * up).astype(BF16)
    acc_ref[...] += _dot(act, wd_ref[...])

    @pl.when(f == pl.num_programs(1) - 1)
    def _():
        o_ref[...] = acc_ref[...]


def _swiglu(v, h, wg, wu, wd, tm=512, tf=1408):
    T, D = h.shape
    F = wg.shape[1]
    return pl.pallas_call(
        _swiglu_kernel,
        out_shape=jax.ShapeDtypeStruct((T, D), F32),
        grid=(T // tm, F // tf),
        in_specs=[
            pl.BlockSpec((tm, D), lambda i, f: (i, 0)),
            pl.BlockSpec((tm, D), lambda i, f: (i, 0)),
            pl.BlockSpec((D, tf), lambda i, f: (0, f)),
            pl.BlockSpec((D, tf), lambda i, f: (0, f)),
            pl.BlockSpec((tf, D), lambda i, f: (f, 0)),
        ],
        out_specs=pl.BlockSpec((tm, D), lambda i, f: (i, 0)),
        scratch_shapes=[pltpu.VMEM((tm, D), F32)],
        compiler_params=_cparams(("parallel", "arbitrary")),
        name="swiglu",
    )(v, h, wg, wu, wd)


MOE_TILE = 1024
MOE_CHUNK = 288


def _router_kernel(h_ref, gn_ref, r_ref, us_ref, cw_ref, cwt_ref, rkt_ref):
    hh = h_ref[...]
    ms = jnp.mean(hh * hh, axis=-1, keepdims=True)
    vf = hh * lax.rsqrt(ms + EPS) * gn_ref[...]
    v_hi, v_lo = _split_bf16(vf)
    r_hi, r_lo = _split_bf16(r_ref[...])
    logits = _dot(v_hi, r_hi) + _dot(v_hi, r_lo) + _dot(v_lo, r_hi)
    lane = lax.broadcasted_iota(I32, logits.shape, 1)
    logits = jnp.where(lane < N_EXPERTS, logits, -jnp.inf)
    m1 = jnp.max(logits, axis=-1, keepdims=True)
    i1 = jnp.min(jnp.where(logits == m1, lane, LANES), axis=-1, keepdims=True)
    rest = jnp.where(lane == i1, -jnp.inf, logits)
    m2 = jnp.max(rest, axis=-1, keepdims=True)
    i2 = jnp.min(jnp.where(rest == m2, lane, LANES), axis=-1, keepdims=True)
    e2 = jnp.exp(m2 - m1)
    w1 = 1.0 / (1.0 + e2)
    cw = jnp.where(lane == i1, w1, 0.0) + jnp.where(lane == i2, e2 * w1, 0.0)
    cw_ref[...] = cw
    cwt = cw.T
    cwt_ref[...] = cwt[0:N_EXPERTS]
    sel = jnp.where(cwt != 0.0, 1.0, 0.0).astype(BF16)
    rkt_ref[...] = _dot(sel, us_ref[...])[0:N_EXPERTS]


def _router(h, gn, router, tm=MOE_TILE):
    T, D = h.shape
    us = jnp.asarray(np.triu(np.ones((tm, tm), np.float32), 1), BF16)
    return pl.pallas_call(
        _router_kernel,
        out_shape=(jax.ShapeDtypeStruct((T, LANES), F32),
                   jax.ShapeDtypeStruct((N_EXPERTS, T), F32),
                   jax.ShapeDtypeStruct((N_EXPERTS, T), F32)),
        grid=(T // tm,),
        in_specs=[
            pl.BlockSpec((tm, D), lambda i: (i, 0)),
            pl.BlockSpec((1, D), lambda i: (0, 0)),
            pl.BlockSpec((D, LANES), lambda i: (0, 0)),
            pl.BlockSpec((tm, tm), lambda i: (0, 0)),
        ],
        out_specs=(pl.BlockSpec((tm, LANES), lambda i: (i, 0)),
                   pl.BlockSpec((N_EXPERTS, tm), lambda i: (0, i)),
                   pl.BlockSpec((N_EXPERTS, tm), lambda i: (0, i))),
        compiler_params=_cparams(("parallel",)),
        name="router",
    )(h, gn, router, us)


def _moe_kernel(v_ref, h_ref, cw_ref, cwt_ref, rkt_ref, wg_ref, wu_ref, wd_ref, o_ref,
                xg_ref, ce_ref, ya_ref):
    e = pl.program_id(1)
    f = pl.program_id(2)
    nf = pl.num_programs(2)
    tm = v_ref.shape[0]
    CH = MOE_CHUNK

    @pl.when((e == 0) & (f == 0))
    def _():
        o_ref[...] = h_ref[...]

    ce_row = cwt_ref[pl.ds(e, 1), :]
    routed = ce_row != 0.0
    slot_row = jnp.where(routed, rkt_ref[pl.ds(e, 1), :], -1.0)
    count = jnp.sum(jnp.where(routed, 1, 0).astype(I32))
    n_chunks = (count + (CH - 1)) // CH
    riota = lax.broadcasted_iota(I32, (CH, tm), 0).astype(F32)

    def pick_rows(i):
        off = (i * CH).astype(F32)
        return jnp.where(slot_row == riota + off, 1.0, 0.0).astype(BF16)

    @pl.when(f == 0)
    def _():
        c_hi, c_lo = _split_bf16(cw_ref[...])
        lane = lax.broadcasted_iota(I32, (CH, LANES), 1)

        def gather(i, carry):
            pick = pick_rows(i)
            xg_ref[i] = _dot(pick, v_ref[...]).astype(BF16)
            cg = _dot(pick, c_hi) + _dot(pick, c_lo)
            ce = jnp.sum(jnp.where(lane == e, cg, 0.0), axis=-1, keepdims=True)
            ce_ref[i] = jnp.broadcast_to(ce, (CH, LANES))
            ya_ref[i] = jnp.zeros((CH, ya_ref.shape[2]), F32)
            return carry

        lax.fori_loop(0, n_chunks, gather, 0)

    def expert(i, carry):
        xg = xg_ref[i]
        gt = _dot(xg, wg_ref[0])
        up = _dot(xg, wu_ref[0])
        act = (gt * _sigmoid(gt) * up * ce_ref[i][:, 0:1]).astype(BF16)
        ya_ref[i] += _dot(act, wd_ref[0])
        return carry

    lax.fori_loop(0, n_chunks, expert, 0)

    @pl.when(f == nf - 1)
    def _():
        def scatter(i, carry):
            o_ref[...] += _dot_tn(pick_rows(i), ya_ref[i].astype(BF16))
            return carry

        lax.fori_loop(0, n_chunks, scatter, 0)


def _moe(v, h, cw, cwt, rkt, wg, wu, wd, tm=MOE_TILE, fsplit=2):
    T, D = h.shape
    E, _, F = wg.shape
    tf = F // fsplit
    max_chunks = -(-tm // MOE_CHUNK)
    return pl.pallas_call(
        _moe_kernel,
        out_shape=jax.ShapeDtypeStruct((T, D), F32),
        grid=(T // tm, E, fsplit),
        in_specs=[
            pl.BlockSpec((tm, D), lambda i, e, f: (i, 0), pipeline_mode=pl.Buffered(1)),
            pl.BlockSpec((tm, D), lambda i, e, f: (i, 0), pipeline_mode=pl.Buffered(1)),
            pl.BlockSpec((tm, LANES), lambda i, e, f: (i, 0), pipeline_mode=pl.Buffered(1)),
            pl.BlockSpec((N_EXPERTS, tm), lambda i, e, f: (0, i)),
            pl.BlockSpec((N_EXPERTS, tm), lambda i, e, f: (0, i)),
            pl.BlockSpec((1, D, tf), lambda i, e, f: (e, 0, f)),
            pl.BlockSpec((1, D, tf), lambda i, e, f: (e, 0, f)),
            pl.BlockSpec((1, tf, D), lambda i, e, f: (e, f, 0)),
        ],
        out_specs=pl.BlockSpec((tm, D), lambda i, e, f: (i, 0)),
        scratch_shapes=[
            pltpu.VMEM((max_chunks, MOE_CHUNK, D), BF16),
            pltpu.VMEM((max_chunks, MOE_CHUNK, LANES), F32),
            pltpu.VMEM((max_chunks, MOE_CHUNK, D), F32),
        ],
        compiler_params=_cparams(("parallel", "arbitrary", "arbitrary")),
        name="moe",
    )(v, h, cw, cwt, rkt, wg, wu, wd)


def _ple_kernel(final, h_ref, p_ref, gn_ref, wg_ref, wp_ref, fn_ref, o_ref):
    h = h_ref[...]
    ms = jnp.mean(h * h, axis=-1, keepdims=True)
    u = (h * lax.rsqrt(ms + EPS) * gn_ref[...]).astype(BF16)
    gate = _sigmoid(_dot(u, wg_ref[...]))
    hn = h + gate * _dot(p_ref[...].astype(BF16), wp_ref[...])
    if final:
        ms = jnp.mean(hn * hn, axis=-1, keepdims=True)
        hn = hn * lax.rsqrt(ms + EPS) * fn_ref[...]
    o_ref[...] = hn


def _ple(h, p, gn, wg, wp, fn, final, tm=512):
    T, D = h.shape
    return pl.pallas_call(
        functools.partial(_ple_kernel, final),
        out_shape=jax.ShapeDtypeStruct((T, D), F32),
        grid=(T // tm,),
        in_specs=[
            pl.BlockSpec((tm, D), lambda i: (i, 0)),
            pl.BlockSpec((tm, PLE_DIM), lambda i: (i, 0)),
            pl.BlockSpec((1, D), lambda i: (0, 0)),
            pl.BlockSpec((D, D), lambda i: (0, 0)),
            pl.BlockSpec((PLE_DIM, D), lambda i: (0, 0)),
            pl.BlockSpec((1, D), lambda i: (0, 0)),
        ],
        out_specs=pl.BlockSpec((tm, D), lambda i: (i, 0)),
        compiler_params=_cparams(("parallel",)),
        name="ple",
    )(h, p, gn, wg, wp, fn)


def _reorder_w_in(w):
    o_sq = 4 * HG_WIDTH
    o_sc = o_sq + SA_WIDTH
    o_iq = o_sc + SA_LATENT
    o_ik = o_iq + IDX_HEADS * IDX_DIM
    o_bq = o_ik + IDX_DIM + IDX_HEADS
    o_gz = o_bq + 3 * SB_WIDTH
    pad = jnp.zeros((w.shape[0], LANES - IDX_DIM - IDX_HEADS), w.dtype)
    return jnp.concatenate([
        w[:, o_gz:], w[:, :o_sq], w[:, o_sq:o_sc], w[:, o_iq:o_ik], w[:, o_sc:o_iq],
        w[:, o_ik:o_bq], pad, w[:, o_bq:o_gz]], axis=1).astype(BF16)


def kernel(x, p, lb_param, norm_mix, w_in, hg_onorm, sa_cnorm, sa_wk, sa_wv, w_branch, w_out,
           norm_ffn, dense_wg, dense_wu, dense_wd, moe_router, moe_wg, moe_wu, moe_wd, norm_ple,
           ple_gate, ple_proj, norm_final):
    B, S, D = x.shape
    T = B * S
    depth = w_in.shape[0]
    lb_sm = jax.nn.softmax(lb_param.astype(F32), axis=0)
    lb_all = jnp.cumsum(lb_sm, axis=0) - lb_sm[0:1]
    h = x.reshape(T, D)
    for i in range(depth):
        z = _inproj(h, norm_mix[i][None, :], _reorder_w_in(w_in[i]))
        a = _hgrn(z, lb_all[i][None, :], hg_onorm[i][None, :], B, S)
        b = _dsa(z, sa_cnorm[i][None, :], sa_wk[i].astype(BF16), sa_wv[i].T.astype(BF16), B, S)
        c = _sb(z, B, S)
        h, v = _merge(a, b, c, z, h, w_branch[i].astype(BF16), w_out[i].astype(BF16),
                      norm_ffn[i][None, :])
        jj = i // 2
        if i % 2 == 0:
            h = _swiglu(v, h, dense_wg[jj].astype(BF16), dense_wu[jj].astype(BF16),
                        dense_wd[jj].astype(BF16))
        else:
            router = jnp.pad(moe_router[jj], ((0, 0), (0, LANES - N_EXPERTS)))
            cw, cwt, rkt = _router(h, norm_ffn[i][None, :], router)
            h = _moe(v, h, cw, cwt, rkt, moe_wg[jj].astype(BF16), moe_wu[jj].astype(BF16),
                     moe_wd[jj].astype(BF16))
        h = _ple(h, p[i].reshape(T, PLE_DIM), norm_ple[i][None, :], ple_gate[i].astype(BF16),
                 ple_proj[i].astype(BF16), norm_final[None, :], i == depth - 1)
    return h.reshape(B, S, D)
```

```python
import functools

import numpy as np
import jax
import jax.numpy as jnp
from jax import lax
from jax.experimental import pallas as pl
from jax.experimental.pallas import tpu as pltpu

F32 = jnp.float32
BF16 = jnp.bfloat16
I32 = jnp.int32

D_MODEL = 1024
CHUNK = 64
QBLOCK = 128
HG_HEADS = 4
HG_DK = 128
HG_WIDTH = HG_HEADS * HG_DK
SA_HEADS = 8
SA_DH = 64
SA_WIDTH = SA_HEADS * SA_DH
SA_LATENT = 128
IDX_HEADS = 4
IDX_DIM = 64
TOPK_MAX = 256
SB_HEADS = 8
SB_DH = 64
SB_WIDTH = SB_HEADS * SB_DH
N_BRANCH = 3
BRANCH_WIDTH = 512
N_EXPERTS = 8
PLE_DIM = 256
EPS = 1e-6

LANES = 128

GZ_OFF = 0
HG_OFF = 3 * D_MODEL
SQ_OFF = HG_OFF + 4 * HG_WIDTH
IQ_OFF = SQ_OFF + SA_WIDTH
SC_OFF = IQ_OFF + IDX_HEADS * IDX_DIM
IKW_OFF = SC_OFF + SA_LATENT
SB_OFF = IKW_OFF + LANES
ZW = SB_OFF + 3 * SB_WIDTH

VMEM_LIMIT = 56 * 1024 * 1024

NEG_BIG = -1e30
INT_MIN = -2147483648
SB_UNDERFLOW = -104.0


def _cparams(sem):
    return pltpu.CompilerParams(dimension_semantics=sem, vmem_limit_bytes=VMEM_LIMIT)


def _sigmoid(x):
    e = jnp.exp(-jnp.abs(x))
    inv = 1.0 / (1.0 + e)
    return jnp.where(x >= 0, inv, e * inv)


def _dot(a, b):
    return jnp.dot(a, b, preferred_element_type=F32)


def _dot_nt(a, b):
    return lax.dot_general(a, b, (((1,), (1,)), ((), ())), preferred_element_type=F32)


def _dot_tn(a, b):
    return lax.dot_general(a, b, (((0,), (0,)), ((), ())), preferred_element_type=F32)


def _split_bf16(x):
    hi = x.astype(BF16)
    lo = (x - hi.astype(F32)).astype(BF16)
    return hi, lo


def _inproj_kernel(x_ref, g_ref, w_ref, o_ref, u_ref):
    @pl.when(pl.program_id(1) == 0)
    def _():
        x = x_ref[...]
        ms = jnp.mean(x * x, axis=-1, keepdims=True)
        u_ref[...] = (x * lax.rsqrt(ms + EPS) * g_ref[...]).astype(BF16)

    o_ref[...] = _dot(u_ref[...], w_ref[...])


def _inproj(h, gain, w, tm=1024, tn=1536):
    T, D = h.shape
    N = w.shape[1]
    return pl.pallas_call(
        _inproj_kernel,
        out_shape=jax.ShapeDtypeStruct((T, N), F32),
        grid=(T // tm, N // tn),
        in_specs=[
            pl.BlockSpec((tm, D), lambda i, j: (i, 0)),
            pl.BlockSpec((1, D), lambda i, j: (0, 0)),
            pl.BlockSpec((D, tn), lambda i, j: (0, j)),
        ],
        out_specs=pl.BlockSpec((tm, tn), lambda i, j: (i, j)),
        scratch_shapes=[pltpu.VMEM((tm, D), BF16)],
        compiler_params=_cparams(("parallel", "arbitrary")),
        name="inproj",
    )(h, gain, w)


_HG_LEVELS = (64, 32, 16, 8, 4, 2)


def _hgrn_consts():
    C = CHUNK
    blocks, masks = [], []
    for n in _HG_LEVELS:
        half = n // 2
        L = np.zeros((C, C), np.float32)
        M = np.zeros((C, C), np.float32)
        for t in range(C):
            base = (t // n) * n
            mid = base + half
            if t >= mid:
                L[t, mid:t + 1] = 1.0
                M[t, base:mid] = 1.0
            else:
                L[t, t + 1:mid] = 1.0
        blocks.append(L)
        masks.append(M)
    masks.append(np.eye(C, dtype=np.float32))
    blocks.append(np.tril(np.ones((C, C), np.float32)))
    blocks.append(np.triu(np.ones((C, C), np.float32), 1))
    return np.concatenate(blocks, 0), np.stack(masks)


def _hgrn_kernel(q_ref, f_ref, i_ref, g_ref, lb_ref, on_ref, lc_ref, mc_ref, o_ref, st_ref,
                 qq_ref, kk_ref, ex_ref, sc_ref):
    C = CHUNK
    nlev = len(_HG_LEVELS)

    @pl.when(pl.program_id(1) == 0)
    def _():
        st_ref[...] = jnp.zeros(st_ref.shape, F32)

    lc = lc_ref[...]
    onorm = on_ref[...]
    n_chunks = q_ref.shape[0] // C

    def chunk(c, carry):
        r0 = pl.multiple_of(c * C, C)
        rows = pl.ds(r0, C)
        fpre = f_ref[rows, :]
        lb = lb_ref[...]
        e = jnp.exp(-jnp.abs(fpre))
        inv = 1.0 / (1.0 + e)
        sg_pos = jnp.where(fpre >= 0, inv, e * inv)
        sg_neg = jnp.where(fpre >= 0, e * inv, inv)
        lf = jnp.log(lb + (1.0 - lb) * sg_pos)
        kk_ref[...] = (1.0 - lb) * sg_neg
        qin = q_ref[rows, :]
        qq_ref[...] = qin * _sigmoid(qin)
        lf_hi, lf_lo = _split_bf16(lf)
        ex_ref[...] = jnp.exp(_dot(lc, lf_hi) + _dot(lc, lf_lo))
        for h in range(HG_HEADS):
            cs = slice(h * HG_DK, (h + 1) * HG_DK)
            q = qq_ref[:, cs]
            k = kk_ref[:, cs]
            scores = _dot_nt(q.astype(BF16), k.astype(BF16)) * mc_ref[nlev]
            for l in range(nlev):
                el = ex_ref[l * C:(l + 1) * C, cs]
                s_l = _dot_nt((q * el).astype(BF16), (k * el).astype(BF16))
                scores = scores + s_l * mc_ref[l]
            sc_ref[h] = scores.astype(BF16)
        for h in range(HG_HEADS):
            cs = slice(h * HG_DK, (h + 1) * HG_DK)
            eb = ex_ref[nlev * C:(nlev + 1) * C, cs]
            er = ex_ref[(nlev + 1) * C:(nlev + 2) * C, cs]
            st = st_ref[h]
            vb = i_ref[rows, cs].astype(BF16)
            o = _dot(sc_ref[h], vb) + _dot_nt((qq_ref[:, cs] * eb).astype(BF16), st.astype(BF16))
            st_ref[h] = st * eb[C - 1:C, :] + _dot_tn(vb, (kk_ref[:, cs] * er).astype(BF16))
            ms = jnp.mean(o * o, axis=-1, keepdims=True)
            on = o * lax.rsqrt(ms + EPS) * onorm
            g = g_ref[rows, cs]
            o_ref[rows, cs] = on * (g * _sigmoid(g))
        return carry

    lax.fori_loop(0, n_chunks, chunk, 0)


def _hgrn(z, lb, onorm, B, S, sblk=1024):
    T = B * S
    sblk = min(sblk, S)
    ns = S // sblk
    lc_np, mc_np = _hgrn_consts()
    lc = jnp.asarray(lc_np, BF16)
    mc = jnp.asarray(mc_np, F32)
    cb = HG_OFF // HG_WIDTH

    def zspec(k):
        return pl.BlockSpec((sblk, HG_WIDTH), lambda b, s, k=k: (b * ns + s, cb + k))

    return pl.pallas_call(
        _hgrn_kernel,
        out_shape=jax.ShapeDtypeStruct((T, HG_WIDTH), F32),
        grid=(B, ns),
        in_specs=[
            zspec(0), zspec(1), zspec(2), zspec(3),
            pl.BlockSpec((1, HG_WIDTH), lambda b, s: (0, 0)),
            pl.BlockSpec((1, HG_DK), lambda b, s: (0, 0)),
            pl.BlockSpec(lc.shape, lambda b, s: (0, 0)),
            pl.BlockSpec(mc.shape, lambda b, s: (0, 0, 0)),
        ],
        out_specs=pl.BlockSpec((sblk, HG_WIDTH), lambda b, s: (b * ns + s, 0)),
        scratch_shapes=[
            pltpu.VMEM((HG_HEADS, HG_DK, HG_DK), F32),
            pltpu.VMEM((CHUNK, HG_WIDTH), F32),
            pltpu.VMEM((CHUNK, HG_WIDTH), F32),
            pltpu.VMEM(((len(_HG_LEVELS) + 2) * CHUNK, HG_WIDTH), F32),
            pltpu.VMEM((HG_HEADS, CHUNK, CHUNK), BF16),
        ],
        compiler_params=_cparams(("parallel", "arbitrary")),
        name="hgrn2",
    )(z, z, z, z, lb, onorm, lc, mc)


def _dsa_kernel(topk, sq_ref, iq_ref, iwq_ref, sc_ref, ikw_ref, cn_ref, wk_ref, wvt_ref,
                o_ref, kk_ref, vt_ref, ik_ref, key_ref, plane_ref, qt_ref, m_ref, l_ref, acc_ref):
    QB = QBLOCK
    j = pl.program_id(1)
    nkb = j + 1
    S = sc_ref.shape[0]
    nb = S // QB
    idx_bits = (S - 1).bit_length()

    @pl.when(j == 0)
    def _():
        c = sc_ref[...]
        ms = jnp.mean(c * c, axis=-1, keepdims=True)
        cb = (c * lax.rsqrt(ms + EPS) * cn_ref[...]).astype(BF16)
        kfull = _dot(cb, wk_ref[...]).astype(BF16)
        vtfull = _dot_nt(wvt_ref[...], cb).astype(BF16)
        ikfull = ikw_ref[:, 0:IDX_DIM].astype(BF16)
        for kb in range(nb):
            kk_ref[kb] = kfull[kb * QB:(kb + 1) * QB]
            vt_ref[kb] = vtfull[:, kb * QB:(kb + 1) * QB]
            ik_ref[kb] = ikfull[kb * QB:(kb + 1) * QB]

    qt_ref[...] = (sq_ref[...] * (SA_DH ** -0.5)).T.astype(BF16)
    iqt = iq_ref[...].T.astype(BF16)
    iwt = iwq_ref[...].T * (IDX_HEADS ** -0.5)

    row = lax.broadcasted_iota(I32, (QB, QB), 0)
    lane = lax.broadcasted_iota(I32, (QB, QB), 1)
    diag_bad = (row >= CHUNK) & (lane < CHUNK)

    n_pairs = (nkb + 1) // 2

    def score_block(kb):
        ikb = ik_ref[kb]
        acc = jnp.zeros((QB, QB), F32)
        for h in range(IDX_HEADS):
            s_h = _dot(ikb, iqt[h * IDX_DIM:(h + 1) * IDX_DIM])
            acc = acc + jnp.maximum(s_h, 0.0) * iwt[IDX_DIM + h:IDX_DIM + h + 1, :]
        acc = acc + 0.0
        acc = jnp.where((kb == j) & diag_bad, -jnp.inf, acc)
        bits = pltpu.bitcast(acc, I32)
        key_ref[kb] = jnp.where(bits < 0, bits ^ jnp.int32(0x7FFFFFFF), bits)

    def score_pair(g, carry):
        score_block(2 * g)
        score_block(2 * g + 1)
        return carry

    lax.fori_loop(0, n_pairs, score_pair, 0)

    @pl.when(j + 1 < 2 * n_pairs)
    def _():
        key_ref[j + 1] = jnp.full((QB, QB), INT_MIN, I32)

    def build_planes(g, carry):
        a = []
        for half in range(2):
            blk = key_ref[2 * g + half] ^ jnp.int32(INT_MIN)
            a.extend(blk[r * 8:(r + 1) * 8, :] for r in range(QB // 8))
        for sh, m in ((16, 0x0000FFFF), (8, 0x00FF00FF), (4, 0x0F0F0F0F), (2, 0x33333333),
                      (1, 0x55555555)):
            for k in range(32):
                if k & sh == 0:
                    t = (lax.shift_right_logical(a[k], jnp.int32(sh)) ^ a[k + sh]) & jnp.int32(m)
                    a[k + sh] = a[k + sh] ^ t
                    a[k] = a[k] ^ lax.shift_left(t, jnp.int32(sh))
        rows = pl.ds(pl.multiple_of(g * 8, 8), 8)
        for b in range(32):
            plane_ref[b, rows, :] = a[b]
        return carry

    def search(_):
        lax.fori_loop(0, n_pairs, build_planes, 0)
        wrow = lax.broadcasted_iota(I32, (nb // 2 * 8, QB), 0)
        cand = jnp.where((wrow >> 3) < n_pairs, jnp.int32(-1), jnp.int32(0))
        above = jnp.zeros((1, QB), I32)
        u = jnp.zeros((1, QB), I32)
        for b in range(31, -1, -1):
            ones = cand & plane_ref[b]
            cnt1 = jnp.sum(lax.population_count(ones), axis=0, keepdims=True)
            take = (above + cnt1) >= topk
            cand = jnp.where(take, ones, cand ^ ones)
            above = jnp.where(take, above, above + cnt1)
            u = u | jnp.where(take, jnp.int32(INT_MIN if b == 31 else 1 << b), jnp.int32(0))
        thr = u ^ jnp.int32(INT_MIN)
        need = topk - above
        n_tied = jnp.sum(lax.population_count(cand), axis=0, keepdims=True)

        def break_ties(_):
            base = (wrow >> 3) * (2 * QB) + (wrow & 7)

            def ibit(i, jp):
                c = jp + lax.shift_left(jnp.int32(1), idx_bits - 1 - i)
                nbits = jnp.clip((c - base + 7) >> 3, 0, 32)
                below = jnp.where(nbits >= 32, jnp.int32(-1),
                                  lax.shift_left(jnp.int32(1), nbits) - 1)
                cnt = jnp.sum(lax.population_count(cand & below), axis=0, keepdims=True)
                return jnp.where(cnt < need, c, jp)

            return lax.fori_loop(0, idx_bits, ibit, jnp.zeros((1, QB), I32))

        def keep_ties(_):
            return jnp.full((1, QB), S, I32)

        jp = lax.cond(jnp.max(n_tied - need) > 0, break_ties, keep_ties, 0)
        return thr, jp

    def take_all(_):
        return (jnp.full((1, QB), INT_MIN, I32), jnp.full((1, QB), -1, I32))

    thr, jp = lax.cond(nkb * QB > topk, search, take_all, 0)

    m_ref[...] = jnp.full(m_ref.shape, NEG_BIG, F32)
    l_ref[...] = jnp.zeros(l_ref.shape, F32)
    acc_ref[...] = jnp.zeros(acc_ref.shape, F32)

    def attend_block(kb):
        key = key_ref[kb]
        kidx = row + kb * QB
        sel = (key > thr) | ((key == thr) & (kidx <= jp))
        hidden = ((kidx >= j * QB + CHUNK) & (lane < CHUNK)) | (kidx >= nkb * QB)
        bias = jnp.where(sel & jnp.logical_not(hidden), 0.0, -jnp.inf)
        kblk = kk_ref[kb]
        vtb = vt_ref[kb]
        for h in range(SA_HEADS):
            lg = _dot(kblk, qt_ref[h * SA_DH:(h + 1) * SA_DH, :]) + bias
            m_old = m_ref[h:h + 1, :]
            m_new = jnp.maximum(m_old, jnp.max(lg, axis=0, keepdims=True))
            alpha = jnp.exp(m_old - m_new)
            p = jnp.exp(lg - m_new)
            l_ref[h:h + 1, :] = alpha * l_ref[h:h + 1, :] + jnp.sum(p, axis=0, keepdims=True)
            acc_ref[h] = acc_ref[h] * alpha + _dot(vtb, p.astype(BF16))
            m_ref[h:h + 1, :] = m_new

    def attend(g, carry):
        attend_block(2 * g)
        attend_block(2 * g + 1)
        return carry

    lax.fori_loop(0, n_pairs, attend, 0)

    outs = []
    for h in range(SA_HEADS):
        outs.append(acc_ref[h] * (1.0 / l_ref[h:h + 1, :]))
    o_ref[...] = jnp.concatenate(outs, axis=0).T


def _dsa(z, cnorm, wk, wvt, B, S):
    T = B * S
    nb = S // QBLOCK
    topk = min(TOPK_MAX, S // 4)
    assert nb % 2 == 0
    kern = functools.partial(_dsa_kernel, topk)
    return pl.pallas_call(
        kern,
        out_shape=jax.ShapeDtypeStruct((T, SA_WIDTH), F32),
        grid=(B, nb),
        in_specs=[
            pl.BlockSpec((QBLOCK, SA_WIDTH), lambda b, j: (b * nb + j, SQ_OFF // SA_WIDTH)),
            pl.BlockSpec((QBLOCK, IDX_HEADS * IDX_DIM),
                         lambda b, j: (b * nb + j, IQ_OFF // (IDX_HEADS * IDX_DIM))),
            pl.BlockSpec((QBLOCK, LANES), lambda b, j: (b * nb + j, IKW_OFF // LANES)),
            pl.BlockSpec((S, SA_LATENT), lambda b, j: (b, SC_OFF // SA_LATENT)),
            pl.BlockSpec((S, LANES), lambda b, j: (b, IKW_OFF // LANES)),
            pl.BlockSpec((1, SA_LATENT), lambda b, j: (0, 0)),
            pl.BlockSpec((SA_LATENT, SA_DH), lambda b, j: (0, 0)),
            pl.BlockSpec((SA_DH, SA_LATENT), lambda b, j: (0, 0)),
        ],
        out_specs=pl.BlockSpec((QBLOCK, SA_WIDTH), lambda b, j: (b * nb + j, 0)),
        scratch_shapes=[
            pltpu.VMEM((nb, QBLOCK, SA_DH), BF16),
            pltpu.VMEM((nb, SA_DH, QBLOCK), BF16),
            pltpu.VMEM((nb, QBLOCK, IDX_DIM), BF16),
            pltpu.VMEM((nb, QBLOCK, QBLOCK), I32),
            pltpu.VMEM((32, nb // 2 * 8, QBLOCK), I32),
            pltpu.VMEM((SA_WIDTH, QBLOCK), BF16),
            pltpu.VMEM((SA_HEADS, QBLOCK), F32),
            pltpu.VMEM((SA_HEADS, QBLOCK), F32),
            pltpu.VMEM((SA_HEADS, SA_DH, QBLOCK), F32),
        ],
        compiler_params=_cparams(("parallel", "arbitrary")),
        name="dsa",
    )(z, z, z, z, z, cnorm, wk, wvt)


def _sb_kernel(q_ref, k_ref, v_ref, uo_ref, o_ref, kt_ref, vb_ref, run_ref, acc_ref,
               ls_ref, lkh_ref, lkl_ref, tot_ref, a_ref):
    QB = QBLOCK
    j = pl.program_id(1)
    S = k_ref.shape[0]
    nb = S // QB
    npair = SB_HEADS // 2
    lane = lax.broadcasted_iota(I32, (QB, QB), 1)
    row = lax.broadcasted_iota(I32, (QB, QB), 0)

    @pl.when(j == 0)
    def _():
        for p in range(npair):
            cols = slice(p * LANES, (p + 1) * LANES)
            kt = k_ref[:, cols].T
            vv = v_ref[:, cols]
            for hh in range(2):
                lo = hh * SB_DH
                ktz = jnp.where((row >= lo) & (row < lo + SB_DH), 1.0, 0.0)
                vz = jnp.where((lane >= lo) & (lane < lo + SB_DH), 1.0, 0.0)
                for kb in range(nb):
                    ks = slice(kb * QB, (kb + 1) * QB)
                    kt_ref[kb, 2 * p + hh] = (kt[:, ks] * ktz).astype(BF16)
                    vb_ref[kb, 2 * p + hh] = (vv[ks, :] * vz).astype(BF16)

    uo = uo_ref[...]
    strict = lane < row
    qs = (q_ref[...] * (SB_DH ** -0.5)).astype(BF16)

    def all_heads(kbs, diag):
        nk = len(kbs)
        for h in range(SB_HEADS):
            p = h // 2
            for i, kb in enumerate(kbs):
                s = h * nk + i
                z = _dot(qs[:, p * LANES:(p + 1) * LANES], kt_ref[kb, h])
                l1p = jnp.log(1.0 + jnp.exp(-jnp.abs(z)))
                ls_pos = jnp.minimum(z, 0.0) - l1p
                lk = ls_pos - z
                if diag:
                    lk = jnp.where(strict, lk, 0.0)
                lk_hi, lk_lo = _split_bf16(lk)
                ls_ref[s] = ls_pos
                lkh_ref[s] = lk_hi
                lkl_ref[s] = lk_lo
                tot_ref[s] = jnp.sum(lk, axis=1, keepdims=True)
        top = None
        for h in range(SB_HEADS):
            run = None if diag else run_ref[h]
            for i in range(nk):
                s = h * nk + i
                after = _dot(lkh_ref[s], uo) + _dot(lkl_ref[s], uo)
                if diag:
                    a = jnp.where(strict, jnp.exp(ls_ref[s] + after), 0.0)
                    run = tot_ref[s]
                else:
                    a = jnp.exp(ls_ref[s] + after + run)
                    run = run + tot_ref[s]
                a_ref[s] = a.astype(BF16)
            run_ref[h] = run
            top = run if top is None else jnp.maximum(top, run)
        for p in range(npair):
            pv = None
            for h in (2 * p, 2 * p + 1):
                for i, kb in enumerate(kbs):
                    out = _dot(a_ref[h * nk + i], vb_ref[kb, h])
                    pv = out if pv is None else pv + out
            if diag:
                acc_ref[p] = pv
            else:
                acc_ref[p] += pv
        return jnp.max(top) > SB_UNDERFLOW

    alive = all_heads([j], True)

    def cond(c):
        return (c[0] >= 1) & c[1]

    def body(c):
        return c[0] - 2, all_heads([c[0], c[0] - 1], False)

    kb_left, alive = lax.while_loop(cond, body, (j - 1, alive))

    @pl.when((kb_left == 0) & alive)
    def _():
        all_heads([0], False)

    o_ref[...] = jnp.concatenate([acc_ref[p] for p in range(npair)], axis=1)


def _sb(z, B, S):
    T = B * S
    nb = S // QBLOCK
    u = np.tril(np.ones((QBLOCK, QBLOCK), np.float32), -1)
    uo = jnp.asarray(u, BF16)
    qoff = SB_OFF // SB_WIDTH
    return pl.pallas_call(
        _sb_kernel,
        out_shape=jax.ShapeDtypeStruct((T, SB_WIDTH), F32),
        grid=(B, nb),
        in_specs=[
            pl.BlockSpec((QBLOCK, SB_WIDTH), lambda b, j: (b * nb + j, qoff)),
            pl.BlockSpec((S, SB_WIDTH), lambda b, j: (b, qoff + 1)),
            pl.BlockSpec((S, SB_WIDTH), lambda b, j: (b, qoff + 2)),
            pl.BlockSpec(uo.shape, lambda b, j: (0, 0)),
        ],
        out_specs=pl.BlockSpec((QBLOCK, SB_WIDTH), lambda b, j: (b * nb + j, 0)),
        scratch_shapes=[
            pltpu.VMEM((nb, SB_HEADS, LANES, QBLOCK), BF16),
            pltpu.VMEM((nb, SB_HEADS, QBLOCK, LANES), BF16),
            pltpu.VMEM((SB_HEADS, QBLOCK, 1), F32),
            pltpu.VMEM((SB_HEADS // 2, QBLOCK, LANES), F32),
            pltpu.VMEM((2 * SB_HEADS, QBLOCK, QBLOCK), F32),
            pltpu.VMEM((2 * SB_HEADS, QBLOCK, QBLOCK), BF16),
            pltpu.VMEM((2 * SB_HEADS, QBLOCK, QBLOCK), BF16),
            pltpu.VMEM((2 * SB_HEADS, QBLOCK, 1), F32),
            pltpu.VMEM((2 * SB_HEADS, QBLOCK, QBLOCK), BF16),
        ],
        compiler_params=_cparams(("parallel", "arbitrary")),
        name="stickbreak",
    )(z, z, z, uo)


def _merge_kernel(a_ref, b_ref, c_ref, gz_ref, h_ref, wb_ref, wo_ref, gn_ref, ho_ref, v_ref):
    merged = None
    for n, br in enumerate((a_ref, b_ref, c_ref)):
        proj = _dot(br[...].astype(BF16), wb_ref[n])
        gate = _sigmoid(gz_ref[:, n * D_MODEL:(n + 1) * D_MODEL])
        merged = gate * proj if merged is None else merged + gate * proj
    hn = h_ref[...] + _dot(merged.astype(BF16), wo_ref[...])
    ho_ref[...] = hn
    ms = jnp.mean(hn * hn, axis=-1, keepdims=True)
    v_ref[...] = (hn * lax.rsqrt(ms + EPS) * gn_ref[...]).astype(BF16)


def _merge(a, b, c, z, h, wb, wo, gn, tm=256):
    T, D = h.shape
    bw = BRANCH_WIDTH
    return pl.pallas_call(
        _merge_kernel,
        out_shape=(jax.ShapeDtypeStruct((T, D), F32), jax.ShapeDtypeStruct((T, D), BF16)),
        grid=(T // tm,),
        in_specs=[
            pl.BlockSpec((tm, bw), lambda i: (i, 0)),
            pl.BlockSpec((tm, bw), lambda i: (i, 0)),
            pl.BlockSpec((tm, bw), lambda i: (i, 0)),
            pl.BlockSpec((tm, N_BRANCH * D), lambda i: (i, GZ_OFF // (N_BRANCH * D))),
            pl.BlockSpec((tm, D), lambda i: (i, 0)),
            pl.BlockSpec((N_BRANCH, bw, D), lambda i: (0, 0, 0)),
            pl.BlockSpec((D, D), lambda i: (0, 0)),
            pl.BlockSpec((1, D), lambda i: (0, 0)),
        ],
        out_specs=(pl.BlockSpec((tm, D), lambda i: (i, 0)), pl.BlockSpec((tm, D), lambda i: (i, 0))),
        compiler_params=_cparams(("parallel",)),
        name="merge",
    )(a, b, c, z, h, wb, wo, gn)


def _swiglu_kernel(v_ref, h_ref, wg_ref, wu_ref, wd_ref, o_ref, acc_ref):
    f = pl.program_id(1)

    @pl.when(f == 0)
    def _():
        acc_ref[...] = h_ref[...]

    v = v_ref[...]
    gt = _dot(v, wg_ref[...])
    up = _dot(v, wu_ref[...])
    act = (gt * _sigmoid(gt) * up).astype(BF16)
    acc_ref[...] += _dot(act, wd_ref[...])

    @pl.when(f == pl.num_programs(1) - 1)
    def _():
        o_ref[...] = acc_ref[...]


def _swiglu(v, h, wg, wu, wd, tm=512, tf=1408):
    T, D = h.shape
    F = wg.shape[1]
    return pl.pallas_call(
        _swiglu_kernel,
        out_shape=jax.ShapeDtypeStruct((T, D), F32),
        grid=(T // tm, F // tf),
        in_specs=[
            pl.BlockSpec((tm, D), lambda i, f: (i, 0)),
            pl.BlockSpec((tm, D), lambda i, f: (i, 0)),
            pl.BlockSpec((D, tf), lambda i, f: (0, f)),
            pl.BlockSpec((D, tf), lambda i, f: (0, f)),
            pl.BlockSpec((tf, D), lambda i, f: (f, 0)),
        ],
        out_specs=pl.BlockSpec((tm, D), lambda i, f: (i, 0)),
        scratch_shapes=[pltpu.VMEM((tm, D), F32)],
        compiler_params=_cparams(("parallel", "arbitrary")),
        name="swiglu",
    )(v, h, wg, wu, wd)


MOE_TILE = 1024
MOE_CHUNK = 288


def _router_kernel(h_ref, gn_ref, r_ref, us_ref, cw_ref, cwt_ref, rkt_ref):
    hh = h_ref[...]
    ms = jnp.mean(hh * hh, axis=-1, keepdims=True)
    vf = hh * lax.rsqrt(ms + EPS) * gn_ref[...]
    v_hi, v_lo = _split_bf16(vf)
    r_hi, r_lo = _split_bf16(r_ref[...])
    logits = _dot(v_hi, r_hi) + _dot(v_hi, r_lo) + _dot(v_lo, r_hi)
    lane = lax.broadcasted_iota(I32, logits.shape, 1)
    logits = jnp.where(lane < N_EXPERTS, logits, -jnp.inf)
    m1 = jnp.max(logits, axis=-1, keepdims=True)
    i1 = jnp.min(jnp.where(logits == m1, lane, LANES), axis=-1, keepdims=True)
    rest = jnp.where(lane == i1, -jnp.inf, logits)
    m2 = jnp.max(rest, axis=-1, keepdims=True)
    i2 = jnp.min(jnp.where(rest == m2, lane, LANES), axis=-1, keepdims=True)
    e2 = jnp.exp(m2 - m1)
    w1 = 1.0 / (1.0 + e2)
    cw = jnp.where(lane == i1, w1, 0.0) + jnp.where(lane == i2, e2 * w1, 0.0)
    cw_ref[...] = cw
    cwt = cw.T
    cwt_ref[...] = cwt[0:N_EXPERTS]
    sel = jnp.where(cwt != 0.0, 1.0, 0.0).astype(BF16)
    rkt_ref[...] = _dot(sel, us_ref[...])[0:N_EXPERTS]


def _router(h, gn, router, tm=MOE_TILE):
    T, D = h.shape
    us = jnp.asarray(np.triu(np.ones((tm, tm), np.float32), 1), BF16)
    return pl.pallas_call(
        _router_kernel,
        out_shape=(jax.ShapeDtypeStruct((T, LANES), F32),
                   jax.ShapeDtypeStruct((N_EXPERTS, T), F32),
                   jax.ShapeDtypeStruct((N_EXPERTS, T), F32)),
        grid=(T // tm,),
        in_specs=[
            pl.BlockSpec((tm, D), lambda i: (i, 0)),
            pl.BlockSpec((1, D), lambda i: (0, 0)),
            pl.BlockSpec((D, LANES), lambda i: (0, 0)),
            pl.BlockSpec((tm, tm), lambda i: (0, 0)),
        ],
        out_specs=(pl.BlockSpec((tm, LANES), lambda i: (i, 0)),
                   pl.BlockSpec((N_EXPERTS, tm), lambda i: (0, i)),
                   pl.BlockSpec((N_EXPERTS, tm), lambda i: (0, i))),
        compiler_params=_cparams(("parallel",)),
        name="router",
    )(h, gn, router, us)


def _moe_kernel(v_ref, h_ref, cw_ref, cwt_ref, rkt_ref, wg_ref, wu_ref, wd_ref, o_ref,
                xg_ref, ce_ref, ya_ref):
    e = pl.program_id(1)
    f = pl.program_id(2)
    nf = pl.num_programs(2)
    tm = v_ref.shape[0]
    CH = MOE_CHUNK

    @pl.when((e == 0) & (f == 0))
    def _():
        o_ref[...] = h_ref[...]

    ce_row = cwt_ref[pl.ds(e, 1), :]
    routed = ce_row != 0.0
    slot_row = jnp.where(routed, rkt_ref[pl.ds(e, 1), :], -1.0)
    count = jnp.sum(jnp.where(routed, 1, 0).astype(I32))
    n_chunks = (count + (CH - 1)) // CH
    riota = lax.broadcasted_iota(I32, (CH, tm), 0).astype(F32)

    def pick_rows(i):
        off = (i * CH).astype(F32)
        return jnp.where(slot_row == riota + off, 1.0, 0.0).astype(BF16)

    @pl.when(f == 0)
    def _():
        c_hi, c_lo = _split_bf16(cw_ref[...])
        lane = lax.broadcasted_iota(I32, (CH, LANES), 1)

        def gather(i, carry):
            pick = pick_rows(i)
            xg_ref[i] = _dot(pick, v_ref[...]).astype(BF16)
            cg = _dot(pick, c_hi) + _dot(pick, c_lo)
            ce = jnp.sum(jnp.where(lane == e, cg, 0.0), axis=-1, keepdims=True)
            ce_ref[i] = jnp.broadcast_to(ce, (CH, LANES))
            ya_ref[i] = jnp.zeros((CH, ya_ref.shape[2]), F32)
            return carry

        lax.fori_loop(0, n_chunks, gather, 0)

    def expert(i, carry):
        xg = xg_ref[i]
        gt = _dot(xg, wg_ref[0])
        up = _dot(xg, wu_ref[0])
        act = (gt * _sigmoid(gt) * up * ce_ref[i][:, 0:1]).astype(BF16)
        ya_ref[i] += _dot(act, wd_ref[0])
        return carry

    lax.fori_loop(0, n_chunks, expert, 0)

    @pl.when(f == nf - 1)
    def _():
        def scatter(i, carry):
            o_ref[...] += _dot_tn(pick_rows(i), ya_ref[i].astype(BF16))
            return carry

        lax.fori_loop(0, n_chunks, scatter, 0)


def _moe(v, h, cw, cwt, rkt, wg, wu, wd, tm=MOE_TILE, fsplit=2):
    T, D = h.shape
    E, _, F = wg.shape
    tf = F // fsplit
    max_chunks = -(-tm // MOE_CHUNK)
    return pl.pallas_call(
        _moe_kernel,
        out_shape=jax.ShapeDtypeStruct((T, D), F32),
        grid=(T // tm, E, fsplit),
        in_specs=[
            pl.BlockSpec((tm, D), lambda i, e, f: (i, 0), pipeline_mode=pl.Buffered(1)),
            pl.BlockSpec((tm, D), lambda i, e, f: (i, 0), pipeline_mode=pl.Buffered(1)),
            pl.BlockSpec((tm, LANES), lambda i, e, f: (i, 0), pipeline_mode=pl.Buffered(1)),
            pl.BlockSpec((N_EXPERTS, tm), lambda i, e, f: (0, i)),
            pl.BlockSpec((N_EXPERTS, tm), lambda i, e, f: (0, i)),
            pl.BlockSpec((1, D, tf), lambda i, e, f: (e, 0, f)),
            pl.BlockSpec((1, D, tf), lambda i, e, f: (e, 0, f)),
            pl.BlockSpec((1, tf, D), lambda i, e, f: (e, f, 0)),
        ],
        out_specs=pl.BlockSpec((tm, D), lambda i, e, f: (i, 0)),
        scratch_shapes=[
            pltpu.VMEM((max_chunks, MOE_CHUNK, D), BF16),
            pltpu.VMEM((max_chunks, MOE_CHUNK, LANES), F32),
            pltpu.VMEM((max_chunks, MOE_CHUNK, D), F32),
        ],
        compiler_params=_cparams(("parallel", "arbitrary", "arbitrary")),
        name="moe",
    )(v, h, cw, cwt, rkt, wg, wu, wd)


def _ple_kernel(final, h_ref, p_ref, gn_ref, wg_ref, wp_ref, fn_ref, o_ref):
    h = h_ref[...]
    ms = jnp.mean(h * h, axis=-1, keepdims=True)
    u = (h * lax.rsqrt(ms + EPS) * gn_ref[...]).astype(BF16)
    gate = _sigmoid(_dot(u, wg_ref[...]))
    hn = h + gate * _dot(p_ref[0].astype(BF16), wp_ref[...])
    if final:
        ms = jnp.mean(hn * hn, axis=-1, keepdims=True)
        hn = hn * lax.rsqrt(ms + EPS) * fn_ref[...]
    o_ref[...] = hn


def _ple(h, p, layer, gn, wg, wp, fn, final, tm=512):
    T, D = h.shape
    return pl.pallas_call(
        functools.partial(_ple_kernel, final),
        out_shape=jax.ShapeDtypeStruct((T, D), F32),
        grid=(T // tm,),
        in_specs=[
            pl.BlockSpec((tm, D), lambda i: (i, 0)),
            pl.BlockSpec((1, tm, PLE_DIM), lambda i: (layer, i, 0)),
            pl.BlockSpec((1, D), lambda i: (0, 0)),
            pl.BlockSpec((D, D), lambda i: (0, 0)),
            pl.BlockSpec((PLE_DIM, D), lambda i: (0, 0)),
            pl.BlockSpec((1, D), lambda i: (0, 0)),
        ],
        out_specs=pl.BlockSpec((tm, D), lambda i: (i, 0)),
        compiler_params=_cparams(("parallel",)),
        name="ple",
    )(h, p, gn, wg, wp, fn)


def _reorder_kernel(w_ref, o_ref):
    o_sq = 4 * HG_WIDTH
    o_sc = o_sq + SA_WIDTH
    o_iq = o_sc + SA_LATENT
    o_ik = o_iq + IDX_HEADS * IDX_DIM
    o_bq = o_ik + IDX_DIM + IDX_HEADS
    o_gz = o_bq + 3 * SB_WIDTH

    def put(dst, lo, hi):
        o_ref[:, dst:dst + (hi - lo)] = w_ref[0, :, lo:hi].astype(BF16)

    put(GZ_OFF, o_gz, o_gz + N_BRANCH * D_MODEL)
    put(HG_OFF, 0, o_sq)
    put(SQ_OFF, o_sq, o_sc)
    put(IQ_OFF, o_iq, o_ik)
    put(SC_OFF, o_sc, o_iq)
    put(IKW_OFF, o_ik, o_bq)
    used = IDX_DIM + IDX_HEADS
    o_ref[:, IKW_OFF + used:SB_OFF] = jnp.zeros((o_ref.shape[0], LANES - used), BF16)
    put(SB_OFF, o_bq, o_gz)


def _reorder_w_in(w_in, layer, tr=256):
    _, R, C = w_in.shape
    return pl.pallas_call(
        _reorder_kernel,
        out_shape=jax.ShapeDtypeStruct((R, ZW), BF16),
        grid=(R // tr,),
        in_specs=[pl.BlockSpec((1, tr, C), lambda i: (layer, i, 0))],
        out_specs=pl.BlockSpec((tr, ZW), lambda i: (i, 0)),
        compiler_params=_cparams(("parallel",)),
        name="reorder_w_in",
    )(w_in)


def kernel(x, p, lb_param, norm_mix, w_in, hg_onorm, sa_cnorm, sa_wk, sa_wv, w_branch, w_out,
           norm_ffn, dense_wg, dense_wu, dense_wd, moe_router, moe_wg, moe_wu, moe_wd, norm_ple,
           ple_gate, ple_proj, norm_final):
    B, S, D = x.shape
    T = B * S
    depth = w_in.shape[0]
    lb_sm = jax.nn.softmax(lb_param.astype(F32), axis=0)
    lb_all = jnp.cumsum(lb_sm, axis=0) - lb_sm[0:1]
    h = x.reshape(T, D)
    for i in range(depth):
        z = _inproj(h, norm_mix[i][None, :], _reorder_w_in(w_in, i))
        a = _hgrn(z, lb_all[i][None, :], hg_onorm[i][None, :], B, S)
        b = _dsa(z, sa_cnorm[i][None, :], sa_wk[i].astype(BF16), sa_wv[i].T.astype(BF16), B, S)
        c = _sb(z, B, S)
        h, v = _merge(a, b, c, z, h, w_branch[i].astype(BF16), w_out[i].astype(BF16),
                      norm_ffn[i][None, :])
        jj = i // 2
        if i % 2 == 0:
            h = _swiglu(v, h, dense_wg[jj].astype(BF16), dense_wu[jj].astype(BF16),
                        dense_wd[jj].astype(BF16))
        else:
            router = jnp.pad(moe_router[jj], ((0, 0), (0, LANES - N_EXPERTS)))
            cw, cwt, rkt = _router(h, norm_ffn[i][None, :], router)
            h = _moe(v, h, cw, cwt, rkt, moe_wg[jj].astype(BF16), moe_wu[jj].astype(BF16),
                     moe_wd[jj].astype(BF16))
        h = _ple(h, p.reshape(depth, T, PLE_DIM), i, norm_ple[i][None, :],
                 ple_gate[i].astype(BF16), ple_proj[i].astype(BF16), norm_final[None, :],
                 i == depth - 1)
    return h.reshape(B, S, D)
```

```python
import functools

import numpy as np
import jax
import jax.numpy as jnp
from jax import lax
from jax.experimental import pallas as pl
from jax.experimental.pallas import tpu as pltpu

F32 = jnp.float32
BF16 = jnp.bfloat16
I32 = jnp.int32

D_MODEL = 1024
CHUNK = 64
QBLOCK = 128
HG_HEADS = 4
HG_DK = 128
HG_WIDTH = HG_HEADS * HG_DK
SA_HEADS = 8
SA_DH = 64
SA_WIDTH = SA_HEADS * SA_DH
SA_LATENT = 128
IDX_HEADS = 4
IDX_DIM = 64
TOPK_MAX = 256
SB_HEADS = 8
SB_DH = 64
SB_WIDTH = SB_HEADS * SB_DH
N_BRANCH = 3
BRANCH_WIDTH = 512
N_EXPERTS = 8
PLE_DIM = 256
EPS = 1e-6

LANES = 128

GZ_OFF = 0
HG_OFF = 3 * D_MODEL
SQ_OFF = HG_OFF + 4 * HG_WIDTH
IQ_OFF = SQ_OFF + SA_WIDTH
SC_OFF = IQ_OFF + IDX_HEADS * IDX_DIM
IKW_OFF = SC_OFF + SA_LATENT
SB_OFF = IKW_OFF + LANES
ZW = SB_OFF + 3 * SB_WIDTH

VMEM_LIMIT = 56 * 1024 * 1024

NEG_BIG = -1e30
INT_MIN = -2147483648
LOG2E = 1.4426950408889634
SB_UNDERFLOW = -104.0


def _cparams(sem):
    return pltpu.CompilerParams(dimension_semantics=sem, vmem_limit_bytes=VMEM_LIMIT)


def _sigmoid(x):
    e = jnp.exp(-jnp.abs(x))
    inv = 1.0 / (1.0 + e)
    return jnp.where(x >= 0, inv, e * inv)


def _dot(a, b):
    return jnp.dot(a, b, preferred_element_type=F32)


def _dot_nt(a, b):
    return lax.dot_general(a, b, (((1,), (1,)), ((), ())), preferred_element_type=F32)


def _dot_tn(a, b):
    return lax.dot_general(a, b, (((0,), (0,)), ((), ())), preferred_element_type=F32)


def _split_bf16(x):
    hi = x.astype(BF16)
    lo = (x - hi.astype(F32)).astype(BF16)
    return hi, lo


def _inproj_kernel(x_ref, g_ref, w_ref, o_ref, u_ref):
    @pl.when(pl.program_id(1) == 0)
    def _():
        x = x_ref[...]
        ms = jnp.mean(x * x, axis=-1, keepdims=True)
        u_ref[...] = (x * lax.rsqrt(ms + EPS) * g_ref[...]).astype(BF16)

    o_ref[...] = _dot(u_ref[...], w_ref[...])


def _inproj(h, gain, w, tm=1024, tn=1536):
    T, D = h.shape
    N = w.shape[1]
    return pl.pallas_call(
        _inproj_kernel,
        out_shape=jax.ShapeDtypeStruct((T, N), F32),
        grid=(T // tm, N // tn),
        in_specs=[
            pl.BlockSpec((tm, D), lambda i, j: (i, 0)),
            pl.BlockSpec((1, D), lambda i, j: (0, 0)),
            pl.BlockSpec((D, tn), lambda i, j: (0, j)),
        ],
        out_specs=pl.BlockSpec((tm, tn), lambda i, j: (i, j)),
        scratch_shapes=[pltpu.VMEM((tm, D), BF16)],
        compiler_params=_cparams(("parallel", "arbitrary")),
        name="inproj",
    )(h, gain, w)


_HG_LEVELS = (64, 32, 16, 8, 4, 2)


def _hgrn_consts():
    C = CHUNK
    blocks, masks = [], []
    for n in _HG_LEVELS:
        half = n // 2
        L = np.zeros((C, C), np.float32)
        M = np.zeros((C, C), np.float32)
        for t in range(C):
            base = (t // n) * n
            mid = base + half
            if t >= mid:
                L[t, mid:t + 1] = 1.0
                M[t, base:mid] = 1.0
            else:
                L[t, t + 1:mid] = 1.0
        blocks.append(L)
        masks.append(M)
    masks.append(np.eye(C, dtype=np.float32))
    blocks.append(np.tril(np.ones((C, C), np.float32)))
    blocks.append(np.triu(np.ones((C, C), np.float32), 1))
    return np.concatenate(blocks, 0), np.stack(masks)


def _hgrn_kernel(q_ref, f_ref, i_ref, g_ref, lb_ref, on_ref, lc_ref, mc_ref, o_ref, st_ref,
                 qq_ref, kk_ref, ex_ref, sc_ref):
    C = CHUNK
    nlev = len(_HG_LEVELS)

    @pl.when(pl.program_id(1) == 0)
    def _():
        st_ref[...] = jnp.zeros(st_ref.shape, F32)

    lc = lc_ref[...]
    onorm = on_ref[...]
    n_chunks = q_ref.shape[0] // C

    def chunk(c, carry):
        r0 = pl.multiple_of(c * C, C)
        rows = pl.ds(r0, C)
        fpre = f_ref[rows, :]
        lb = lb_ref[...]
        e = jnp.exp(-jnp.abs(fpre))
        inv = 1.0 / (1.0 + e)
        sg_pos = jnp.where(fpre >= 0, inv, e * inv)
        sg_neg = jnp.where(fpre >= 0, e * inv, inv)
        lf = jnp.log(lb + (1.0 - lb) * sg_pos)
        kk_ref[...] = (1.0 - lb) * sg_neg
        qin = q_ref[rows, :]
        qq_ref[...] = qin * _sigmoid(qin)
        lf_hi, lf_lo = _split_bf16(lf)
        ex_ref[...] = jnp.exp(_dot(lc, lf_hi) + _dot(lc, lf_lo))
        for h in range(HG_HEADS):
            cs = slice(h * HG_DK, (h + 1) * HG_DK)
            q = qq_ref[:, cs]
            k = kk_ref[:, cs]
            scores = _dot_nt(q.astype(BF16), k.astype(BF16)) * mc_ref[nlev]
            for l in range(nlev):
                el = ex_ref[l * C:(l + 1) * C, cs]
                s_l = _dot_nt((q * el).astype(BF16), (k * el).astype(BF16))
                scores = scores + s_l * mc_ref[l]
            sc_ref[h] = scores.astype(BF16)
        for h in range(HG_HEADS):
            cs = slice(h * HG_DK, (h + 1) * HG_DK)
            eb = ex_ref[nlev * C:(nlev + 1) * C, cs]
            er = ex_ref[(nlev + 1) * C:(nlev + 2) * C, cs]
            st = st_ref[h]
            vb = i_ref[rows, cs].astype(BF16)
            o = _dot(sc_ref[h], vb) + _dot_nt((qq_ref[:, cs] * eb).astype(BF16), st.astype(BF16))
            st_ref[h] = st * eb[C - 1:C, :] + _dot_tn(vb, (kk_ref[:, cs] * er).astype(BF16))
            ms = jnp.mean(o * o, axis=-1, keepdims=True)
            on = o * lax.rsqrt(ms + EPS) * onorm
            g = g_ref[rows, cs]
            o_ref[rows, cs] = on * (g * _sigmoid(g))
        return carry

    lax.fori_loop(0, n_chunks, chunk, 0)


def _hgrn(z, lb, onorm, B, S, sblk=1024):
    T = B * S
    sblk = min(sblk, S)
    ns = S // sblk
    lc_np, mc_np = _hgrn_consts()
    lc = jnp.asarray(lc_np, BF16)
    mc = jnp.asarray(mc_np, F32)
    cb = HG_OFF // HG_WIDTH

    def zspec(k):
        return pl.BlockSpec((sblk, HG_WIDTH), lambda b, s, k=k: (b * ns + s, cb + k))

    return pl.pallas_call(
        _hgrn_kernel,
        out_shape=jax.ShapeDtypeStruct((T, HG_WIDTH), F32),
        grid=(B, ns),
        in_specs=[
            zspec(0), zspec(1), zspec(2), zspec(3),
            pl.BlockSpec((1, HG_WIDTH), lambda b, s: (0, 0)),
            pl.BlockSpec((1, HG_DK), lambda b, s: (0, 0)),
            pl.BlockSpec(lc.shape, lambda b, s: (0, 0)),
            pl.BlockSpec(mc.shape, lambda b, s: (0, 0, 0)),
        ],
        out_specs=pl.BlockSpec((sblk, HG_WIDTH), lambda b, s: (b * ns + s, 0)),
        scratch_shapes=[
            pltpu.VMEM((HG_HEADS, HG_DK, HG_DK), F32),
            pltpu.VMEM((CHUNK, HG_WIDTH), F32),
            pltpu.VMEM((CHUNK, HG_WIDTH), F32),
            pltpu.VMEM(((len(_HG_LEVELS) + 2) * CHUNK, HG_WIDTH), F32),
            pltpu.VMEM((HG_HEADS, CHUNK, CHUNK), BF16),
        ],
        compiler_params=_cparams(("parallel", "arbitrary")),
        name="hgrn2",
    )(z, z, z, z, lb, onorm, lc, mc)


def _dsa_kernel(topk, sq_ref, iq_ref, iwq_ref, sc_ref, ikw_ref, cn_ref, wk_ref, wvt_ref,
                o_ref, kk_ref, vt_ref, ik_ref, key_ref, plane_ref, qt_ref, m_ref, l_ref, acc_ref):
    QB = QBLOCK
    j = pl.program_id(1)
    nkb = j + 1
    S = sc_ref.shape[0]
    nb = S // QB
    idx_bits = (S - 1).bit_length()

    @pl.when(j == 0)
    def _():
        c = sc_ref[...]
        ms = jnp.mean(c * c, axis=-1, keepdims=True)
        cb = (c * lax.rsqrt(ms + EPS) * cn_ref[...]).astype(BF16)
        kfull = _dot(cb, wk_ref[...]).astype(BF16)
        vtfull = _dot_nt(wvt_ref[...], cb).astype(BF16)
        ikfull = ikw_ref[:, 0:IDX_DIM].astype(BF16)
        for kb in range(nb):
            kk_ref[kb] = kfull[kb * QB:(kb + 1) * QB]
            vt_ref[kb] = vtfull[:, kb * QB:(kb + 1) * QB]
            ik_ref[kb] = ikfull[kb * QB:(kb + 1) * QB]

    qt_ref[...] = (sq_ref[...] * (SA_DH ** -0.5 * LOG2E)).T.astype(BF16)
    iqt = iq_ref[...].T.astype(BF16)
    iwt = iwq_ref[...].T * (IDX_HEADS ** -0.5)

    row = lax.broadcasted_iota(I32, (QB, QB), 0)
    lane = lax.broadcasted_iota(I32, (QB, QB), 1)
    diag_bad = (row >= CHUNK) & (lane < CHUNK)

    n_pairs = (nkb + 1) // 2

    def score_block(kb):
        ikb = ik_ref[kb]
        acc = jnp.zeros((QB, QB), F32)
        for h in range(IDX_HEADS):
            s_h = _dot(ikb, iqt[h * IDX_DIM:(h + 1) * IDX_DIM])
            acc = acc + jnp.maximum(s_h, 0.0) * iwt[IDX_DIM + h:IDX_DIM + h + 1, :]
        acc = acc + 0.0
        acc = jnp.where((kb == j) & diag_bad, -jnp.inf, acc)
        bits = pltpu.bitcast(acc, I32)
        key_ref[kb] = jnp.where(bits < 0, bits ^ jnp.int32(0x7FFFFFFF), bits)

    def score_pair(g, carry):
        score_block(2 * g)
        score_block(2 * g + 1)
        return carry

    lax.fori_loop(0, n_pairs, score_pair, 0)

    @pl.when(j + 1 < 2 * n_pairs)
    def _():
        key_ref[j + 1] = jnp.full((QB, QB), INT_MIN, I32)

    def build_planes(g, carry):
        a = []
        for half in range(2):
            blk = key_ref[2 * g + half] ^ jnp.int32(INT_MIN)
            a.extend(blk[r * 8:(r + 1) * 8, :] for r in range(QB // 8))
        for sh, m in ((16, 0x0000FFFF), (8, 0x00FF00FF), (4, 0x0F0F0F0F), (2, 0x33333333),
                      (1, 0x55555555)):
            for k in range(32):
                if k & sh == 0:
                    t = (lax.shift_right_logical(a[k], jnp.int32(sh)) ^ a[k + sh]) & jnp.int32(m)
                    a[k + sh] = a[k + sh] ^ t
                    a[k] = a[k] ^ lax.shift_left(t, jnp.int32(sh))
        rows = pl.ds(pl.multiple_of(g * 8, 8), 8)
        for b in range(32):
            plane_ref[b, rows, :] = a[b]
        return carry

    def search(_):
        lax.fori_loop(0, n_pairs, build_planes, 0)
        wrow = lax.broadcasted_iota(I32, (nb // 2 * 8, QB), 0)
        cand = jnp.where((wrow >> 3) < n_pairs, jnp.int32(-1), jnp.int32(0))
        above = jnp.zeros((1, QB), I32)
        u = jnp.zeros((1, QB), I32)
        for b in range(31, -1, -1):
            ones = cand & plane_ref[b]
            cnt1 = jnp.sum(lax.population_count(ones), axis=0, keepdims=True)
            take = (above + cnt1) >= topk
            cand = jnp.where(take, ones, cand ^ ones)
            above = jnp.where(take, above, above + cnt1)
            u = u | jnp.where(take, jnp.int32(INT_MIN if b == 31 else 1 << b), jnp.int32(0))
        thr = u ^ jnp.int32(INT_MIN)
        need = topk - above
        n_tied = jnp.sum(lax.population_count(cand), axis=0, keepdims=True)

        def break_ties(_):
            base = (wrow >> 3) * (2 * QB) + (wrow & 7)

            def ibit(i, jp):
                c = jp + lax.shift_left(jnp.int32(1), idx_bits - 1 - i)
                nbits = jnp.clip((c - base + 7) >> 3, 0, 32)
                below = jnp.where(nbits >= 32, jnp.int32(-1),
                                  lax.shift_left(jnp.int32(1), nbits) - 1)
                cnt = jnp.sum(lax.population_count(cand & below), axis=0, keepdims=True)
                return jnp.where(cnt < need, c, jp)

            return lax.fori_loop(0, idx_bits, ibit, jnp.zeros((1, QB), I32))

        def keep_ties(_):
            return jnp.full((1, QB), S, I32)

        jp = lax.cond(jnp.max(n_tied - need) > 0, break_ties, keep_ties, 0)
        return thr, jp

    def take_all(_):
        return (jnp.full((1, QB), INT_MIN, I32), jnp.full((1, QB), -1, I32))

    thr, jp = lax.cond(nkb * QB > topk, search, take_all, 0)

    m_ref[...] = jnp.full(m_ref.shape, NEG_BIG, F32)
    l_ref[...] = jnp.zeros(l_ref.shape, F32)
    acc_ref[...] = jnp.zeros(acc_ref.shape, F32)

    def attend_block(kb):
        key = key_ref[kb]
        kidx = row + kb * QB
        sel = (key > thr) | ((key == thr) & (kidx <= jp))
        hidden = ((kidx >= j * QB + CHUNK) & (lane < CHUNK)) | (kidx >= nkb * QB)
        bias = jnp.where(sel & jnp.logical_not(hidden), 0.0, -jnp.inf)
        kblk = kk_ref[kb]
        vtb = vt_ref[kb]
        for h in range(SA_HEADS):
            lg = _dot(kblk, qt_ref[h * SA_DH:(h + 1) * SA_DH, :]) + bias
            m_old = m_ref[h:h + 1, :]
            m_new = jnp.maximum(m_old, jnp.max(lg, axis=0, keepdims=True))
            alpha = jnp.exp2(m_old - m_new)
            p = jnp.exp2(lg - m_new)
            l_ref[h:h + 1, :] = alpha * l_ref[h:h + 1, :] + jnp.sum(p, axis=0, keepdims=True)
            acc_ref[h] = acc_ref[h] * alpha + _dot(vtb, p.astype(BF16))
            m_ref[h:h + 1, :] = m_new

    def attend(g, carry):
        attend_block(2 * g)
        attend_block(2 * g + 1)
        return carry

    lax.fori_loop(0, n_pairs, attend, 0)

    outs = []
    for h in range(SA_HEADS):
        outs.append(acc_ref[h] * (1.0 / l_ref[h:h + 1, :]))
    o_ref[...] = jnp.concatenate(outs, axis=0).T


def _dsa(z, cnorm, wk, wvt, B, S):
    T = B * S
    nb = S // QBLOCK
    topk = min(TOPK_MAX, S // 4)
    assert nb % 2 == 0
    kern = functools.partial(_dsa_kernel, topk)
    return pl.pallas_call(
        kern,
        out_shape=jax.ShapeDtypeStruct((T, SA_WIDTH), F32),
        grid=(B, nb),
        in_specs=[
            pl.BlockSpec((QBLOCK, SA_WIDTH), lambda b, j: (b * nb + j, SQ_OFF // SA_WIDTH)),
            pl.BlockSpec((QBLOCK, IDX_HEADS * IDX_DIM),
                         lambda b, j: (b * nb + j, IQ_OFF // (IDX_HEADS * IDX_DIM))),
            pl.BlockSpec((QBLOCK, LANES), lambda b, j: (b * nb + j, IKW_OFF // LANES)),
            pl.BlockSpec((S, SA_LATENT), lambda b, j: (b, SC_OFF // SA_LATENT)),
            pl.BlockSpec((S, LANES), lambda b, j: (b, IKW_OFF // LANES)),
            pl.BlockSpec((1, SA_LATENT), lambda b, j: (0, 0)),
            pl.BlockSpec((SA_LATENT, SA_DH), lambda b, j: (0, 0)),
            pl.BlockSpec((SA_DH, SA_LATENT), lambda b, j: (0, 0)),
        ],
        out_specs=pl.BlockSpec((QBLOCK, SA_WIDTH), lambda b, j: (b * nb + j, 0)),
        scratch_shapes=[
            pltpu.VMEM((nb, QBLOCK, SA_DH), BF16),
            pltpu.VMEM((nb, SA_DH, QBLOCK), BF16),
            pltpu.VMEM((nb, QBLOCK, IDX_DIM), BF16),
            pltpu.VMEM((nb, QBLOCK, QBLOCK), I32),
            pltpu.VMEM((32, nb // 2 * 8, QBLOCK), I32),
            pltpu.VMEM((SA_WIDTH, QBLOCK), BF16),
            pltpu.VMEM((SA_HEADS, QBLOCK), F32),
            pltpu.VMEM((SA_HEADS, QBLOCK), F32),
            pltpu.VMEM((SA_HEADS, SA_DH, QBLOCK), F32),
        ],
        compiler_params=_cparams(("parallel", "arbitrary")),
        name="dsa",
    )(z, z, z, z, z, cnorm, wk, wvt)


def _sb_kernel(q_ref, k_ref, v_ref, uo_ref, o_ref, kt_ref, vb_ref, run_ref, acc_ref,
               ls_ref, lkh_ref, lkl_ref, tot_ref, a_ref):
    QB = QBLOCK
    j = pl.program_id(1)
    S = k_ref.shape[0]
    nb = S // QB
    npair = SB_HEADS // 2
    lane = lax.broadcasted_iota(I32, (QB, QB), 1)
    row = lax.broadcasted_iota(I32, (QB, QB), 0)

    @pl.when(j == 0)
    def _():
        for p in range(npair):
            cols = slice(p * LANES, (p + 1) * LANES)
            kt = k_ref[:, cols].T
            vv = v_ref[:, cols]
            for hh in range(2):
                lo = hh * SB_DH
                ktz = jnp.where((row >= lo) & (row < lo + SB_DH), 1.0, 0.0)
                vz = jnp.where((lane >= lo) & (lane < lo + SB_DH), 1.0, 0.0)
                for kb in range(nb):
                    ks = slice(kb * QB, (kb + 1) * QB)
                    kt_ref[kb, 2 * p + hh] = (kt[:, ks] * ktz).astype(BF16)
                    vb_ref[kb, 2 * p + hh] = (vv[ks, :] * vz).astype(BF16)

    uo = uo_ref[...]
    strict = lane < row
    qs = (q_ref[...] * (SB_DH ** -0.5)).astype(BF16)

    def all_heads(kbs, diag):
        nk = len(kbs)
        for h in range(SB_HEADS):
            p = h // 2
            for i, kb in enumerate(kbs):
                s = h * nk + i
                z = _dot(qs[:, p * LANES:(p + 1) * LANES], kt_ref[kb, h])
                l1p = jnp.log(1.0 + jnp.exp(-jnp.abs(z)))
                ls_pos = jnp.minimum(z, 0.0) - l1p
                lk = ls_pos - z
                if diag:
                    lk = jnp.where(strict, lk, 0.0)
                lk_hi, lk_lo = _split_bf16(lk)
                ls_ref[s] = ls_pos
                lkh_ref[s] = lk_hi
                lkl_ref[s] = lk_lo
                tot_ref[s] = jnp.sum(lk, axis=1, keepdims=True)
        top = None
        for h in range(SB_HEADS):
            run = None if diag else run_ref[h]
            for i in range(nk):
                s = h * nk + i
                after = _dot(lkh_ref[s], uo) + _dot(lkl_ref[s], uo)
                if diag:
                    a = jnp.where(strict, jnp.exp(ls_ref[s] + after), 0.0)
                    run = tot_ref[s]
                else:
                    a = jnp.exp(ls_ref[s] + after + run)
                    run = run + tot_ref[s]
                a_ref[s] = a.astype(BF16)
            run_ref[h] = run
            top = run if top is None else jnp.maximum(top, run)
        for p in range(npair):
            pv = None
            for h in (2 * p, 2 * p + 1):
                for i, kb in enumerate(kbs):
                    out = _dot(a_ref[h * nk + i], vb_ref[kb, h])
                    pv = out if pv is None else pv + out
            if diag:
                acc_ref[p] = pv
            else:
                acc_ref[p] += pv
        return jnp.max(top) > SB_UNDERFLOW

    alive = all_heads([j], True)

    def cond(c):
        return (c[0] >= 1) & c[1]

    def body(c):
        return c[0] - 2, all_heads([c[0], c[0] - 1], False)

    kb_left, alive = lax.while_loop(cond, body, (j - 1, alive))

    @pl.when((kb_left == 0) & alive)
    def _():
        all_heads([0], False)

    o_ref[...] = jnp.concatenate([acc_ref[p] for p in range(npair)], axis=1)


def _sb(z, B, S):
    T = B * S
    nb = S // QBLOCK
    u = np.tril(np.ones((QBLOCK, QBLOCK), np.float32), -1)
    uo = jnp.asarray(u, BF16)
    qoff = SB_OFF // SB_WIDTH
    return pl.pallas_call(
        _sb_kernel,
        out_shape=jax.ShapeDtypeStruct((T, SB_WIDTH), F32),
        grid=(B, nb),
        in_specs=[
            pl.BlockSpec((QBLOCK, SB_WIDTH), lambda b, j: (b * nb + j, qoff)),
            pl.BlockSpec((S, SB_WIDTH), lambda b, j: (b, qoff + 1)),
            pl.BlockSpec((S, SB_WIDTH), lambda b, j: (b, qoff + 2)),
            pl.BlockSpec(uo.shape, lambda b, j: (0, 0)),
        ],
        out_specs=pl.BlockSpec((QBLOCK, SB_WIDTH), lambda b, j: (b * nb + j, 0)),
        scratch_shapes=[
            pltpu.VMEM((nb, SB_HEADS, LANES, QBLOCK), BF16),
            pltpu.VMEM((nb, SB_HEADS, QBLOCK, LANES), BF16),
            pltpu.VMEM((SB_HEADS, QBLOCK, 1), F32),
            pltpu.VMEM((SB_HEADS // 2, QBLOCK, LANES), F32),
            pltpu.VMEM((2 * SB_HEADS, QBLOCK, QBLOCK), F32),
            pltpu.VMEM((2 * SB_HEADS, QBLOCK, QBLOCK), BF16),
            pltpu.VMEM((2 * SB_HEADS, QBLOCK, QBLOCK), BF16),
            pltpu.VMEM((2 * SB_HEADS, QBLOCK, 1), F32),
            pltpu.VMEM((2 * SB_HEADS, QBLOCK, QBLOCK), BF16),
        ],
        compiler_params=_cparams(("parallel", "arbitrary")),
        name="stickbreak",
    )(z, z, z, uo)


def _merge_kernel(a_ref, b_ref, c_ref, gz_ref, h_ref, wb_ref, wo_ref, gn_ref, ho_ref, v_ref):
    merged = None
    for n, br in enumerate((a_ref, b_ref, c_ref)):
        proj = _dot(br[...].astype(BF16), wb_ref[n])
        gate = _sigmoid(gz_ref[:, n * D_MODEL:(n + 1) * D_MODEL])
        merged = gate * proj if merged is None else merged + gate * proj
    hn = h_ref[...] + _dot(merged.astype(BF16), wo_ref[...])
    ho_ref[...] = hn
    ms = jnp.mean(hn * hn, axis=-1, keepdims=True)
    v_ref[...] = (hn * lax.rsqrt(ms + EPS) * gn_ref[...]).astype(BF16)


def _merge(a, b, c, z, h, wb, wo, gn, tm=256):
    T, D = h.shape
    bw = BRANCH_WIDTH
    return pl.pallas_call(
        _merge_kernel,
        out_shape=(jax.ShapeDtypeStruct((T, D), F32), jax.ShapeDtypeStruct((T, D), BF16)),
        grid=(T // tm,),
        in_specs=[
            pl.BlockSpec((tm, bw), lambda i: (i, 0)),
            pl.BlockSpec((tm, bw), lambda i: (i, 0)),
            pl.BlockSpec((tm, bw), lambda i: (i, 0)),
            pl.BlockSpec((tm, N_BRANCH * D), lambda i: (i, GZ_OFF // (N_BRANCH * D))),
            pl.BlockSpec((tm, D), lambda i: (i, 0)),
            pl.BlockSpec((N_BRANCH, bw, D), lambda i: (0, 0, 0)),
            pl.BlockSpec((D, D), lambda i: (0, 0)),
            pl.BlockSpec((1, D), lambda i: (0, 0)),
        ],
        out_specs=(pl.BlockSpec((tm, D), lambda i: (i, 0)), pl.BlockSpec((tm, D), lambda i: (i, 0))),
        compiler_params=_cparams(("parallel",)),
        name="merge",
    )(a, b, c, z, h, wb, wo, gn)


def _swiglu_kernel(v_ref, h_ref, wg_ref, wu_ref, wd_ref, o_ref, acc_ref):
    f = pl.program_id(1)

    @pl.when(f == 0)
    def _():
        acc_ref[...] = h_ref[...]

    v = v_ref[...]
    gt = _dot(v, wg_ref[...])
    up = _dot(v, wu_ref[...])
    act = (gt * _sigmoid(gt) * up).astype(BF16)
    acc_ref[...] += _dot(act, wd_ref[...])

    @pl.when(f == pl.num_programs(1) - 1)
    def _():
        o_ref[...] = acc_ref[...]


def _swiglu(v, h, wg, wu, wd, tm=512, tf=1408):
    T, D = h.shape
    F = wg.shape[1]
    return pl.pallas_call(
        _swiglu_kernel,
        out_shape=jax.ShapeDtypeStruct((T, D), F32),
        grid=(T // tm, F // tf),
        in_specs=[
            pl.BlockSpec((tm, D), lambda i, f: (i, 0)),
            pl.BlockSpec((tm, D), lambda i, f: (i, 0)),
            pl.BlockSpec((D, tf), lambda i, f: (0, f)),
            pl.BlockSpec((D, tf), lambda i, f: (0, f)),
            pl.BlockSpec((tf, D), lambda i, f: (f, 0)),
        ],
        out_specs=pl.BlockSpec((tm, D), lambda i, f: (i, 0)),
        scratch_shapes=[pltpu.VMEM((tm, D), F32)],
        compiler_params=_cparams(("parallel", "arbitrary")),
        name="swiglu",
    )(v, h, wg, wu, wd)


MOE_TILE = 1024
MOE_CHUNK = 288


def _router_kernel(h_ref, gn_ref, r_ref, us_ref, cw_ref, cwt_ref, rkt_ref):
    hh = h_ref[...]
    ms = jnp.mean(hh * hh, axis=-1, keepdims=True)
    vf = hh * lax.rsqrt(ms + EPS) * gn_ref[...]
    v_hi, v_lo = _split_bf16(vf)
    r_hi, r_lo = _split_bf16(r_ref[...])
    logits = _dot(v_hi, r_hi) + _dot(v_hi, r_lo) + _dot(v_lo, r_hi)
    lane = lax.broadcasted_iota(I32, logits.shape, 1)
    logits = jnp.where(lane < N_EXPERTS, logits, -jnp.inf)
    m1 = jnp.max(logits, axis=-1, keepdims=True)
    i1 = jnp.min(jnp.where(logits == m1, lane, LANES), axis=-1, keepdims=True)
    rest = jnp.where(lane == i1, -jnp.inf, logits)
    m2 = jnp.max(rest, axis=-1, keepdims=True)
    i2 = jnp.min(jnp.where(rest == m2, lane, LANES), axis=-1, keepdims=True)
    e2 = jnp.exp(m2 - m1)
    w1 = 1.0 / (1.0 + e2)
    cw = jnp.where(lane == i1, w1, 0.0) + jnp.where(lane == i2, e2 * w1, 0.0)
    cw_ref[...] = cw
    cwt = cw.T
    cwt_ref[...] = cwt[0:N_EXPERTS]
    sel = jnp.where(cwt != 0.0, 1.0, 0.0).astype(BF16)
    rkt_ref[...] = _dot(sel, us_ref[...])[0:N_EXPERTS]


def _router(h, gn, router, tm=MOE_TILE):
    T, D = h.shape
    us = jnp.asarray(np.triu(np.ones((tm, tm), np.float32), 1), BF16)
    return pl.pallas_call(
        _router_kernel,
        out_shape=(jax.ShapeDtypeStruct((T, LANES), F32),
                   jax.ShapeDtypeStruct((N_EXPERTS, T), F32),
                   jax.ShapeDtypeStruct((N_EXPERTS, T), F32)),
        grid=(T // tm,),
        in_specs=[
            pl.BlockSpec((tm, D), lambda i: (i, 0)),
            pl.BlockSpec((1, D), lambda i: (0, 0)),
            pl.BlockSpec((D, LANES), lambda i: (0, 0)),
            pl.BlockSpec((tm, tm), lambda i: (0, 0)),
        ],
        out_specs=(pl.BlockSpec((tm, LANES), lambda i: (i, 0)),
                   pl.BlockSpec((N_EXPERTS, tm), lambda i: (0, i)),
                   pl.BlockSpec((N_EXPERTS, tm), lambda i: (0, i))),
        compiler_params=_cparams(("parallel",)),
        name="router",
    )(h, gn, router, us)


MOE_SUB = 256
MOE_WIN = 112


def _moe_kernel(v_ref, h_ref, cw_ref, cwt_ref, rkt_ref, wg_ref, wu_ref, wd_ref, o_ref,
                xg_ref, ce_ref, ya_ref):
    e = pl.program_id(1)
    f = pl.program_id(2)
    nf = pl.num_programs(2)
    tm = v_ref.shape[0]
    CH, SUB, WIN = MOE_CHUNK, MOE_SUB, MOE_WIN
    nsub = tm // SUB

    @pl.when((e == 0) & (f == 0))
    def _():
        o_ref[...] = h_ref[...]

    ce_row = cwt_ref[pl.ds(e, 1), :]
    routed = ce_row != 0.0
    rank_row = rkt_ref[pl.ds(e, 1), :]
    slot_row = jnp.where(routed, rank_row, -1.0)
    count = jnp.sum(jnp.where(routed, 1, 0).astype(I32))
    n_chunks = (count + (CH - 1)) // CH
    bounds = [jnp.int32(0)]
    for s in range(1, nsub):
        bounds.append(rank_row[0, s * SUB].astype(I32))
    bounds.append(count)
    wiota = lax.broadcasted_iota(I32, (WIN, SUB), 0).astype(F32)

    def for_windows(fn):
        def window(s, r0):
            slots = slot_row[:, s * SUB:(s + 1) * SUB]
            pick = jnp.where(slots == wiota + r0.astype(F32), 1.0, 0.0).astype(BF16)
            fn(s, r0, pick)

        starts = [pl.multiple_of((bounds[s] // 16) * 16, 16) for s in range(nsub)]
        for s in range(nsub):
            window(s, starts[s])
        for s in range(nsub):
            nw = (bounds[s + 1] - starts[s] + (WIN - 1)) // WIN

            def body(m, carry, s=s):
                window(s, pl.multiple_of(starts[s] + m * WIN, 16))
                return carry

            lax.fori_loop(1, nw, body, 0)

    @pl.when(f == 0)
    def _():
        def clear(i, carry):
            rows = pl.ds(pl.multiple_of(i * CH, 16), CH)
            xg_ref[rows, :] = jnp.zeros((CH, xg_ref.shape[1]), BF16)
            ce_ref[rows, :] = jnp.zeros((CH, LANES), F32)
            ya_ref[rows, :] = jnp.zeros((CH, ya_ref.shape[1]), F32)
            return carry

        lax.fori_loop(0, jnp.minimum(n_chunks + 1, xg_ref.shape[0] // CH), clear, 0)
        lane = lax.broadcasted_iota(I32, (WIN, LANES), 1)

        def gather(s, r0, pick):
            toks = slice(s * SUB, (s + 1) * SUB)
            rows = pl.ds(r0, WIN)
            got = _dot(pick, v_ref[toks, :])
            xg_ref[rows, :] = (xg_ref[rows, :].astype(F32) + got).astype(BF16)
            c_hi, c_lo = _split_bf16(cw_ref[toks, :])
            cg = _dot(pick, c_hi) + _dot(pick, c_lo)
            ce = jnp.sum(jnp.where(lane == e, cg, 0.0), axis=-1, keepdims=True)
            ce_ref[rows, :] += jnp.broadcast_to(ce, (WIN, LANES))

        for_windows(gather)

    def expert(i, carry):
        rows = pl.ds(pl.multiple_of(i * CH, 16), CH)
        xg = xg_ref[rows, :]
        gt = _dot(xg, wg_ref[0])
        up = _dot(xg, wu_ref[0])
        act = (gt * _sigmoid(gt) * up * ce_ref[rows, 0:1]).astype(BF16)
        ya_ref[rows, :] += _dot(act, wd_ref[0])
        return carry

    lax.fori_loop(0, n_chunks, expert, 0)

    @pl.when(f == nf - 1)
    def _():
        def scatter(s, r0, pick):
            ye = ya_ref[pl.ds(r0, WIN), :].astype(BF16)
            o_ref[s * SUB:(s + 1) * SUB, :] += _dot_tn(pick, ye)

        for_windows(scatter)


def _moe(v, h, cw, cwt, rkt, wg, wu, wd, tm=MOE_TILE, fsplit=2):
    T, D = h.shape
    E, _, F = wg.shape
    tf = F // fsplit
    rows = (-(-tm // MOE_CHUNK)) * MOE_CHUNK
    assert rows >= tm + MOE_WIN and tm % MOE_SUB == 0
    return pl.pallas_call(
        _moe_kernel,
        out_shape=jax.ShapeDtypeStruct((T, D), F32),
        grid=(T // tm, E, fsplit),
        in_specs=[
            pl.BlockSpec((tm, D), lambda i, e, f: (i, 0), pipeline_mode=pl.Buffered(1)),
            pl.BlockSpec((tm, D), lambda i, e, f: (i, 0), pipeline_mode=pl.Buffered(1)),
            pl.BlockSpec((tm, LANES), lambda i, e, f: (i, 0), pipeline_mode=pl.Buffered(1)),
            pl.BlockSpec((N_EXPERTS, tm), lambda i, e, f: (0, i)),
            pl.BlockSpec((N_EXPERTS, tm), lambda i, e, f: (0, i)),
            pl.BlockSpec((1, D, tf), lambda i, e, f: (e, 0, f)),
            pl.BlockSpec((1, D, tf), lambda i, e, f: (e, 0, f)),
            pl.BlockSpec((1, tf, D), lambda i, e, f: (e, f, 0)),
        ],
        out_specs=pl.BlockSpec((tm, D), lambda i, e, f: (i, 0)),
        scratch_shapes=[
            pltpu.VMEM((rows, D), BF16),
            pltpu.VMEM((rows, LANES), F32),
            pltpu.VMEM((rows, D), F32),
        ],
        compiler_params=_cparams(("parallel", "arbitrary", "arbitrary")),
        name="moe",
    )(v, h, cw, cwt, rkt, wg, wu, wd)


def _ple_kernel(final, h_ref, p_ref, gn_ref, wg_ref, wp_ref, fn_ref, o_ref):
    h = h_ref[...]
    ms = jnp.mean(h * h, axis=-1, keepdims=True)
    u = (h * lax.rsqrt(ms + EPS) * gn_ref[...]).astype(BF16)
    gate = _sigmoid(_dot(u, wg_ref[...]))
    hn = h + gate * _dot(p_ref[0].astype(BF16), wp_ref[...])
    if final:
        ms = jnp.mean(hn * hn, axis=-1, keepdims=True)
        hn = hn * lax.rsqrt(ms + EPS) * fn_ref[...]
    o_ref[...] = hn


def _ple(h, p, layer, gn, wg, wp, fn, final, tm=512):
    T, D = h.shape
    return pl.pallas_call(
        functools.partial(_ple_kernel, final),
        out_shape=jax.ShapeDtypeStruct((T, D), F32),
        grid=(T // tm,),
        in_specs=[
            pl.BlockSpec((tm, D), lambda i: (i, 0)),
            pl.BlockSpec((1, tm, PLE_DIM), lambda i: (layer, i, 0)),
            pl.BlockSpec((1, D), lambda i: (0, 0)),
            pl.BlockSpec((D, D), lambda i: (0, 0)),
            pl.BlockSpec((PLE_DIM, D), lambda i: (0, 0)),
            pl.BlockSpec((1, D), lambda i: (0, 0)),
        ],
        out_specs=pl.BlockSpec((tm, D), lambda i: (i, 0)),
        compiler_params=_cparams(("parallel",)),
        name="ple",
    )(h, p, gn, wg, wp, fn)


def _reorder_kernel(w_ref, o_ref):
    o_sq = 4 * HG_WIDTH
    o_sc = o_sq + SA_WIDTH
    o_iq = o_sc + SA_LATENT
    o_ik = o_iq + IDX_HEADS * IDX_DIM
    o_bq = o_ik + IDX_DIM + IDX_HEADS
    o_gz = o_bq + 3 * SB_WIDTH

    def put(dst, lo, hi):
        o_ref[:, dst:dst + (hi - lo)] = w_ref[0, :, lo:hi].astype(BF16)

    put(GZ_OFF, o_gz, o_gz + N_BRANCH * D_MODEL)
    put(HG_OFF, 0, o_sq)
    put(SQ_OFF, o_sq, o_sc)
    put(IQ_OFF, o_iq, o_ik)
    put(SC_OFF, o_sc, o_iq)
    put(IKW_OFF, o_ik, o_bq)
    used = IDX_DIM + IDX_HEADS
    o_ref[:, IKW_OFF + used:SB_OFF] = jnp.zeros((o_ref.shape[0], LANES - used), BF16)
    put(SB_OFF, o_bq, o_gz)


def _reorder_w_in(w_in, layer, tr=256):
    _, R, C = w_in.shape
    return pl.pallas_call(
        _reorder_kernel,
        out_shape=jax.ShapeDtypeStruct((R, ZW), BF16),
        grid=(R // tr,),
        in_specs=[pl.BlockSpec((1, tr, C), lambda i: (layer, i, 0))],
        out_specs=pl.BlockSpec((tr, ZW), lambda i: (i, 0)),
        compiler_params=_cparams(("parallel",)),
        name="reorder_w_in",
    )(w_in)


def kernel(x, p, lb_param, norm_mix, w_in, hg_onorm, sa_cnorm, sa_wk, sa_wv, w_branch, w_out,
           norm_ffn, dense_wg, dense_wu, dense_wd, moe_router, moe_wg, moe_wu, moe_wd, norm_ple,
           ple_gate, ple_proj, norm_final):
    B, S, D = x.shape
    T = B * S
    depth = w_in.shape[0]
    lb_sm = jax.nn.softmax(lb_param.astype(F32), axis=0)
    lb_all = jnp.cumsum(lb_sm, axis=0) - lb_sm[0:1]
    h = x.reshape(T, D)
    for i in range(depth):
        z = _inproj(h, norm_mix[i][None, :], _reorder_w_in(w_in, i))
        a = _hgrn(z, lb_all[i][None, :], hg_onorm[i][None, :], B, S)
        b = _dsa(z, sa_cnorm[i][None, :], sa_wk[i].astype(BF16), sa_wv[i].T.astype(BF16), B, S)
        c = _sb(z, B, S)
        h, v = _merge(a, b, c, z, h, w_branch[i].astype(BF16), w_out[i].astype(BF16),
                      norm_ffn[i][None, :])
        jj = i // 2
        if i % 2 == 0:
            h = _swiglu(v, h, dense_wg[jj].astype(BF16), dense_wu[jj].astype(BF16),
                        dense_wd[jj].astype(BF16))
        else:
            router = jnp.pad(moe_router[jj], ((0, 0), (0, LANES - N_EXPERTS)))
            cw, cwt, rkt = _router(h, norm_ffn[i][None, :], router)
            h = _moe(v, h, cw, cwt, rkt, moe_wg[jj].astype(BF16), moe_wu[jj].astype(BF16),
                     moe_wd[jj].astype(BF16))
        h = _ple(h, p.reshape(depth, T, PLE_DIM), i, norm_ple[i][None, :],
                 ple_gate[i].astype(BF16), ple_proj[i].astype(BF16), norm_final[None, :],
                 i == depth - 1)
    return h.reshape(B, S, D)
```

```python
import functools

import numpy as np
import jax
import jax.numpy as jnp
from jax import lax
from jax.experimental import pallas as pl
from jax.experimental.pallas import tpu as pltpu

F32 = jnp.float32
BF16 = jnp.bfloat16
I32 = jnp.int32

D_MODEL = 1024
CHUNK = 64
QBLOCK = 128
HG_HEADS = 4
HG_DK = 128
HG_WIDTH = HG_HEADS * HG_DK
SA_HEADS = 8
SA_DH = 64
SA_WIDTH = SA_HEADS * SA_DH
SA_LATENT = 128
IDX_HEADS = 4
IDX_DIM = 64
TOPK_MAX = 256
SB_HEADS = 8
SB_DH = 64
SB_WIDTH = SB_HEADS * SB_DH
N_BRANCH = 3
BRANCH_WIDTH = 512
N_EXPERTS = 8
PLE_DIM = 256
EPS = 1e-6

LANES = 128

HG_OFF = 0
SQ_OFF = HG_OFF + 4 * HG_WIDTH
IQ_OFF = SQ_OFF + SA_WIDTH
SC_OFF = IQ_OFF + IDX_HEADS * IDX_DIM
IKW_OFF = SC_OFF + SA_LATENT
SB_OFF = IKW_OFF + LANES
ZW = SB_OFF + 3 * SB_WIDTH
GZW = N_BRANCH * D_MODEL

VMEM_LIMIT = 56 * 1024 * 1024

NEG_BIG = -1e30
INT_MIN = -2147483648
LOG2E = 1.4426950408889634
SB_UNDERFLOW = -104.0


def _cparams(sem):
    return pltpu.CompilerParams(dimension_semantics=sem, vmem_limit_bytes=VMEM_LIMIT)


def _sigmoid(x):
    e = jnp.exp(-jnp.abs(x))
    inv = 1.0 / (1.0 + e)
    return jnp.where(x >= 0, inv, e * inv)


def _dot(a, b):
    return jnp.dot(a, b, preferred_element_type=F32)


def _dot_nt(a, b):
    return lax.dot_general(a, b, (((1,), (1,)), ((), ())), preferred_element_type=F32)


def _dot_tn(a, b):
    return lax.dot_general(a, b, (((0,), (0,)), ((), ())), preferred_element_type=F32)


def _split_bf16(x):
    hi = x.astype(BF16)
    lo = (x - hi.astype(F32)).astype(BF16)
    return hi, lo


def _inproj_kernel(x_ref, g_ref, w_ref, o_ref, u_ref):
    @pl.when(pl.program_id(1) == 0)
    def _():
        x = x_ref[...]
        ms = jnp.mean(x * x, axis=-1, keepdims=True)
        u_ref[...] = (x * lax.rsqrt(ms + EPS) * g_ref[...]).astype(BF16)

    o_ref[...] = _dot(u_ref[...], w_ref[...])


def _inproj(h, gain, w, tm=1024, tn=1536):
    T, D = h.shape
    N = w.shape[1]
    return pl.pallas_call(
        _inproj_kernel,
        out_shape=(jax.ShapeDtypeStruct((T, N), F32), jax.ShapeDtypeStruct((T, D), BF16)),
        grid=(T // tm, N // tn),
        in_specs=[
            pl.BlockSpec((tm, D), lambda i, j: (i, 0)),
            pl.BlockSpec((1, D), lambda i, j: (0, 0)),
            pl.BlockSpec((D, tn), lambda i, j: (0, j)),
        ],
        out_specs=(pl.BlockSpec((tm, tn), lambda i, j: (i, j)),
                   pl.BlockSpec((tm, D), lambda i, j: (i, 0))),
        compiler_params=_cparams(("parallel", "arbitrary")),
        name="inproj",
    )(h, gain, w)


_HG_LEVELS = (64, 32, 16, 8, 4, 2)


def _hgrn_consts():
    C = CHUNK
    blocks, masks = [], []
    for n in _HG_LEVELS:
        half = n // 2
        L = np.zeros((C, C), np.float32)
        M = np.zeros((C, C), np.float32)
        for t in range(C):
            base = (t // n) * n
            mid = base + half
            if t >= mid:
                L[t, mid:t + 1] = 1.0
                M[t, base:mid] = 1.0
            else:
                L[t, t + 1:mid] = 1.0
        blocks.append(L)
        masks.append(M)
    masks.append(np.eye(C, dtype=np.float32))
    blocks.append(np.tril(np.ones((C, C), np.float32)))
    blocks.append(np.triu(np.ones((C, C), np.float32), 1))
    return np.concatenate(blocks, 0), np.stack(masks)


def _hgrn_kernel(q_ref, f_ref, i_ref, g_ref, lb_ref, on_ref, lc_ref, mc_ref, o_ref, st_ref,
                 qq_ref, kk_ref, ex_ref, sc_ref):
    C = CHUNK
    nlev = len(_HG_LEVELS)

    @pl.when(pl.program_id(1) == 0)
    def _():
        st_ref[...] = jnp.zeros(st_ref.shape, F32)

    lc = lc_ref[...]
    onorm = on_ref[...]
    n_chunks = q_ref.shape[0] // C

    def chunk(c, carry):
        r0 = pl.multiple_of(c * C, C)
        rows = pl.ds(r0, C)
        fpre = f_ref[rows, :]
        lb = lb_ref[...]
        e = jnp.exp(-jnp.abs(fpre))
        inv = 1.0 / (1.0 + e)
        sg_pos = jnp.where(fpre >= 0, inv, e * inv)
        sg_neg = jnp.where(fpre >= 0, e * inv, inv)
        lf = jnp.log(lb + (1.0 - lb) * sg_pos)
        kk_ref[...] = (1.0 - lb) * sg_neg
        qin = q_ref[rows, :]
        qq_ref[...] = qin * _sigmoid(qin)
        lf_hi, lf_lo = _split_bf16(lf)
        ex_ref[...] = jnp.exp(_dot(lc, lf_hi) + _dot(lc, lf_lo))
        for h in range(HG_HEADS):
            cs = slice(h * HG_DK, (h + 1) * HG_DK)
            q = qq_ref[:, cs]
            k = kk_ref[:, cs]
            scores = _dot_nt(q.astype(BF16), k.astype(BF16)) * mc_ref[nlev]
            for l in range(nlev):
                el = ex_ref[l * C:(l + 1) * C, cs]
                s_l = _dot_nt((q * el).astype(BF16), (k * el).astype(BF16))
                scores = scores + s_l * mc_ref[l]
            sc_ref[h] = scores.astype(BF16)
        for h in range(HG_HEADS):
            cs = slice(h * HG_DK, (h + 1) * HG_DK)
            eb = ex_ref[nlev * C:(nlev + 1) * C, cs]
            er = ex_ref[(nlev + 1) * C:(nlev + 2) * C, cs]
            st = st_ref[h]
            vb = i_ref[rows, cs].astype(BF16)
            o = _dot(sc_ref[h], vb) + _dot_nt((qq_ref[:, cs] * eb).astype(BF16), st.astype(BF16))
            st_ref[h] = st * eb[C - 1:C, :] + _dot_tn(vb, (kk_ref[:, cs] * er).astype(BF16))
            ms = jnp.mean(o * o, axis=-1, keepdims=True)
            on = o * lax.rsqrt(ms + EPS) * onorm
            g = g_ref[rows, cs]
            o_ref[rows, cs] = on * (g * _sigmoid(g))
        return carry

    lax.fori_loop(0, n_chunks, chunk, 0)


def _hgrn(z, lb, onorm, B, S, sblk=1024):
    T = B * S
    sblk = min(sblk, S)
    ns = S // sblk
    lc_np, mc_np = _hgrn_consts()
    lc = jnp.asarray(lc_np, BF16)
    mc = jnp.asarray(mc_np, F32)
    cb = HG_OFF // HG_WIDTH

    def zspec(k):
        return pl.BlockSpec((sblk, HG_WIDTH), lambda b, s, k=k: (b * ns + s, cb + k))

    return pl.pallas_call(
        _hgrn_kernel,
        out_shape=jax.ShapeDtypeStruct((T, HG_WIDTH), F32),
        grid=(B, ns),
        in_specs=[
            zspec(0), zspec(1), zspec(2), zspec(3),
            pl.BlockSpec((1, HG_WIDTH), lambda b, s: (0, 0)),
            pl.BlockSpec((1, HG_DK), lambda b, s: (0, 0)),
            pl.BlockSpec(lc.shape, lambda b, s: (0, 0)),
            pl.BlockSpec(mc.shape, lambda b, s: (0, 0, 0)),
        ],
        out_specs=pl.BlockSpec((sblk, HG_WIDTH), lambda b, s: (b * ns + s, 0)),
        scratch_shapes=[
            pltpu.VMEM((HG_HEADS, HG_DK, HG_DK), F32),
            pltpu.VMEM((CHUNK, HG_WIDTH), F32),
            pltpu.VMEM((CHUNK, HG_WIDTH), F32),
            pltpu.VMEM(((len(_HG_LEVELS) + 2) * CHUNK, HG_WIDTH), F32),
            pltpu.VMEM((HG_HEADS, CHUNK, CHUNK), BF16),
        ],
        compiler_params=_cparams(("parallel", "arbitrary")),
        name="hgrn2",
    )(z, z, z, z, lb, onorm, lc, mc)


def _dsa_kernel(topk, sq_ref, iq_ref, iwq_ref, sc_ref, ikw_ref, cn_ref, wk_ref, wvt_ref,
                o_ref, kk_ref, vt_ref, ik_ref, key_ref, plane_ref, qt_ref, m_ref, l_ref, acc_ref):
    QB = QBLOCK
    j = pl.program_id(1)
    nkb = j + 1
    S = sc_ref.shape[0]
    nb = S // QB
    idx_bits = (S - 1).bit_length()

    @pl.when(j == 0)
    def _():
        c = sc_ref[...]
        ms = jnp.mean(c * c, axis=-1, keepdims=True)
        cb = (c * lax.rsqrt(ms + EPS) * cn_ref[...]).astype(BF16)
        kfull = _dot(cb, wk_ref[...]).astype(BF16)
        vtfull = _dot_nt(wvt_ref[...], cb).astype(BF16)
        ikfull = ikw_ref[:, 0:IDX_DIM].astype(BF16)
        for kb in range(nb):
            kk_ref[kb] = kfull[kb * QB:(kb + 1) * QB]
            vt_ref[kb] = vtfull[:, kb * QB:(kb + 1) * QB]
            ik_ref[kb] = ikfull[kb * QB:(kb + 1) * QB]

    qt_ref[...] = (sq_ref[...] * (SA_DH ** -0.5 * LOG2E)).T.astype(BF16)
    iqt = iq_ref[...].T.astype(BF16)
    iwt = iwq_ref[...].T * (IDX_HEADS ** -0.5)

    row = lax.broadcasted_iota(I32, (QB, QB), 0)
    lane = lax.broadcasted_iota(I32, (QB, QB), 1)
    diag_bad = (row >= CHUNK) & (lane < CHUNK)

    n_pairs = (nkb + 1) // 2

    def score_block(kb):
        ikb = ik_ref[kb]
        acc = jnp.zeros((QB, QB), F32)
        for h in range(IDX_HEADS):
            s_h = _dot(ikb, iqt[h * IDX_DIM:(h + 1) * IDX_DIM])
            acc = acc + jnp.maximum(s_h, 0.0) * iwt[IDX_DIM + h:IDX_DIM + h + 1, :]
        acc = acc + 0.0
        acc = jnp.where((kb == j) & diag_bad, -jnp.inf, acc)
        bits = pltpu.bitcast(acc, I32)
        key_ref[kb] = jnp.where(bits < 0, bits ^ jnp.int32(0x7FFFFFFF), bits)

    def score_pair(g, carry):
        score_block(2 * g)
        score_block(2 * g + 1)
        return carry

    lax.fori_loop(0, n_pairs, score_pair, 0)

    @pl.when(j + 1 < 2 * n_pairs)
    def _():
        key_ref[j + 1] = jnp.full((QB, QB), INT_MIN, I32)

    def build_planes(g, carry):
        a = []
        for half in range(2):
            blk = key_ref[2 * g + half] ^ jnp.int32(INT_MIN)
            a.extend(blk[r * 8:(r + 1) * 8, :] for r in range(QB // 8))
        for sh, m in ((16, 0x0000FFFF), (8, 0x00FF00FF), (4, 0x0F0F0F0F), (2, 0x33333333),
                      (1, 0x55555555)):
            for k in range(32):
                if k & sh == 0:
                    t = (lax.shift_right_logical(a[k], jnp.int32(sh)) ^ a[k + sh]) & jnp.int32(m)
                    a[k + sh] = a[k + sh] ^ t
                    a[k] = a[k] ^ lax.shift_left(t, jnp.int32(sh))
        rows = pl.ds(pl.multiple_of(g * 8, 8), 8)
        for b in range(32):
            plane_ref[b, rows, :] = a[b]
        return carry

    def search(_):
        lax.fori_loop(0, n_pairs, build_planes, 0)
        wrow = lax.broadcasted_iota(I32, (nb // 2 * 8, QB), 0)
        cand = jnp.where((wrow >> 3) < n_pairs, jnp.int32(-1), jnp.int32(0))
        above = jnp.zeros((1, QB), I32)
        u = jnp.zeros((1, QB), I32)
        for b in range(31, -1, -1):
            ones = cand & plane_ref[b]
            cnt1 = jnp.sum(lax.population_count(ones), axis=0, keepdims=True)
            take = (above + cnt1) >= topk
            cand = jnp.where(take, ones, cand ^ ones)
            above = jnp.where(take, above, above + cnt1)
            u = u | jnp.where(take, jnp.int32(INT_MIN if b == 31 else 1 << b), jnp.int32(0))
        thr = u ^ jnp.int32(INT_MIN)
        need = topk - above
        n_tied = jnp.sum(lax.population_count(cand), axis=0, keepdims=True)

        def break_ties(_):
            base = (wrow >> 3) * (2 * QB) + (wrow & 7)

            def ibit(i, jp):
                c = jp + lax.shift_left(jnp.int32(1), idx_bits - 1 - i)
                nbits = jnp.clip((c - base + 7) >> 3, 0, 32)
                below = jnp.where(nbits >= 32, jnp.int32(-1),
                                  lax.shift_left(jnp.int32(1), nbits) - 1)
                cnt = jnp.sum(lax.population_count(cand & below), axis=0, keepdims=True)
                return jnp.where(cnt < need, c, jp)

            return lax.fori_loop(0, idx_bits, ibit, jnp.zeros((1, QB), I32))

        def keep_ties(_):
            return jnp.full((1, QB), S, I32)

        jp = lax.cond(jnp.max(n_tied - need) > 0, break_ties, keep_ties, 0)
        return thr, jp

    def take_all(_):
        return (jnp.full((1, QB), INT_MIN, I32), jnp.full((1, QB), -1, I32))

    thr, jp = lax.cond(nkb * QB > topk, search, take_all, 0)

    m_ref[...] = jnp.full(m_ref.shape, NEG_BIG, F32)
    l_ref[...] = jnp.zeros(l_ref.shape, F32)
    acc_ref[...] = jnp.zeros(acc_ref.shape, F32)

    def attend_block(kb):
        key = key_ref[kb]
        kidx = row + kb * QB
        sel = (key > thr) | ((key == thr) & (kidx <= jp))
        hidden = ((kidx >= j * QB + CHUNK) & (lane < CHUNK)) | (kidx >= nkb * QB)
        bias = jnp.where(sel & jnp.logical_not(hidden), 0.0, -jnp.inf)
        kblk = kk_ref[kb]
        vtb = vt_ref[kb]
        for h in range(SA_HEADS):
            lg = _dot(kblk, qt_ref[h * SA_DH:(h + 1) * SA_DH, :]) + bias
            m_old = m_ref[h:h + 1, :]
            m_new = jnp.maximum(m_old, jnp.max(lg, axis=0, keepdims=True))
            alpha = jnp.exp2(m_old - m_new)
            p = jnp.exp2(lg - m_new)
            l_ref[h:h + 1, :] = alpha * l_ref[h:h + 1, :] + jnp.sum(p, axis=0, keepdims=True)
            acc_ref[h] = acc_ref[h] * alpha + _dot(vtb, p.astype(BF16))
            m_ref[h:h + 1, :] = m_new

    def attend(g, carry):
        attend_block(2 * g)
        attend_block(2 * g + 1)
        return carry

    lax.fori_loop(0, n_pairs, attend, 0)

    outs = []
    for h in range(SA_HEADS):
        outs.append(acc_ref[h] * (1.0 / l_ref[h:h + 1, :]))
    o_ref[...] = jnp.concatenate(outs, axis=0).T


def _dsa(z, cnorm, wk, wvt, B, S):
    T = B * S
    nb = S // QBLOCK
    topk = min(TOPK_MAX, S // 4)
    assert nb % 2 == 0
    kern = functools.partial(_dsa_kernel, topk)
    return pl.pallas_call(
        kern,
        out_shape=jax.ShapeDtypeStruct((T, SA_WIDTH), F32),
        grid=(B, nb),
        in_specs=[
            pl.BlockSpec((QBLOCK, SA_WIDTH), lambda b, j: (b * nb + j, SQ_OFF // SA_WIDTH)),
            pl.BlockSpec((QBLOCK, IDX_HEADS * IDX_DIM),
                         lambda b, j: (b * nb + j, IQ_OFF // (IDX_HEADS * IDX_DIM))),
            pl.BlockSpec((QBLOCK, LANES), lambda b, j: (b * nb + j, IKW_OFF // LANES)),
            pl.BlockSpec((S, SA_LATENT), lambda b, j: (b, SC_OFF // SA_LATENT)),
            pl.BlockSpec((S, LANES), lambda b, j: (b, IKW_OFF // LANES)),
            pl.BlockSpec((1, SA_LATENT), lambda b, j: (0, 0)),
            pl.BlockSpec((SA_LATENT, SA_DH), lambda b, j: (0, 0)),
            pl.BlockSpec((SA_DH, SA_LATENT), lambda b, j: (0, 0)),
        ],
        out_specs=pl.BlockSpec((QBLOCK, SA_WIDTH), lambda b, j: (b * nb + j, 0)),
        scratch_shapes=[
            pltpu.VMEM((nb, QBLOCK, SA_DH), BF16),
            pltpu.VMEM((nb, SA_DH, QBLOCK), BF16),
            pltpu.VMEM((nb, QBLOCK, IDX_DIM), BF16),
            pltpu.VMEM((nb, QBLOCK, QBLOCK), I32),
            pltpu.VMEM((32, nb // 2 * 8, QBLOCK), I32),
            pltpu.VMEM((SA_WIDTH, QBLOCK), BF16),
            pltpu.VMEM((SA_HEADS, QBLOCK), F32),
            pltpu.VMEM((SA_HEADS, QBLOCK), F32),
            pltpu.VMEM((SA_HEADS, SA_DH, QBLOCK), F32),
        ],
        compiler_params=_cparams(("parallel", "arbitrary")),
        name="dsa",
    )(z, z, z, z, z, cnorm, wk, wvt)


def _sb_kernel(q_ref, k_ref, v_ref, uo_ref, o_ref, kt_ref, vb_ref, run_ref, acc_ref,
               ls_ref, lkh_ref, lkl_ref, tot_ref, a_ref):
    QB = QBLOCK
    j = pl.program_id(1)
    S = k_ref.shape[0]
    nb = S // QB
    npair = SB_HEADS // 2
    lane = lax.broadcasted_iota(I32, (QB, QB), 1)
    row = lax.broadcasted_iota(I32, (QB, QB), 0)

    @pl.when(j == 0)
    def _():
        for p in range(npair):
            cols = slice(p * LANES, (p + 1) * LANES)
            kt = k_ref[:, cols].T
            vv = v_ref[:, cols]
            for hh in range(2):
                lo = hh * SB_DH
                ktz = jnp.where((row >= lo) & (row < lo + SB_DH), 1.0, 0.0)
                vz = jnp.where((lane >= lo) & (lane < lo + SB_DH), 1.0, 0.0)
                for kb in range(nb):
                    ks = slice(kb * QB, (kb + 1) * QB)
                    kt_ref[kb, 2 * p + hh] = (kt[:, ks] * ktz).astype(BF16)
                    vb_ref[kb, 2 * p + hh] = (vv[ks, :] * vz).astype(BF16)

    uo = uo_ref[...]
    strict = lane < row
    qs = (q_ref[...] * (SB_DH ** -0.5)).astype(BF16)

    def all_heads(kbs, diag):
        nk = len(kbs)
        for h in range(SB_HEADS):
            p = h // 2
            for i, kb in enumerate(kbs):
                s = h * nk + i
                z = _dot(qs[:, p * LANES:(p + 1) * LANES], kt_ref[kb, h])
                l1p = jnp.log(1.0 + jnp.exp(-jnp.abs(z)))
                ls_pos = jnp.minimum(z, 0.0) - l1p
                lk = ls_pos - z
                if diag:
                    lk = jnp.where(strict, lk, 0.0)
                lk_hi, lk_lo = _split_bf16(lk)
                ls_ref[s] = ls_pos
                lkh_ref[s] = lk_hi
                lkl_ref[s] = lk_lo
                tot_ref[s] = jnp.sum(lk, axis=1, keepdims=True)
        top = None
        for h in range(SB_HEADS):
            run = None if diag else run_ref[h]
            for i in range(nk):
                s = h * nk + i
                after = _dot(lkh_ref[s], uo) + _dot(lkl_ref[s], uo)
                if diag:
                    a = jnp.where(strict, jnp.exp(ls_ref[s] + after), 0.0)
                    run = tot_ref[s]
                else:
                    a = jnp.exp(ls_ref[s] + after + run)
                    run = run + tot_ref[s]
                a_ref[s] = a.astype(BF16)
            run_ref[h] = run
            top = run if top is None else jnp.maximum(top, run)
        for p in range(npair):
            pv = None
            for h in (2 * p, 2 * p + 1):
                for i, kb in enumerate(kbs):
                    out = _dot(a_ref[h * nk + i], vb_ref[kb, h])
                    pv = out if pv is None else pv + out
            if diag:
                acc_ref[p] = pv
            else:
                acc_ref[p] += pv
        return jnp.max(top) > SB_UNDERFLOW

    alive = all_heads([j], True)

    def cond(c):
        return (c[0] >= 1) & c[1]

    def body(c):
        return c[0] - 2, all_heads([c[0], c[0] - 1], False)

    kb_left, alive = lax.while_loop(cond, body, (j - 1, alive))

    @pl.when((kb_left == 0) & alive)
    def _():
        all_heads([0], False)

    o_ref[...] = jnp.concatenate([acc_ref[p] for p in range(npair)], axis=1)


def _sb(z, B, S):
    T = B * S
    nb = S // QBLOCK
    u = np.tril(np.ones((QBLOCK, QBLOCK), np.float32), -1)
    uo = jnp.asarray(u, BF16)
    qoff = SB_OFF // SB_WIDTH
    return pl.pallas_call(
        _sb_kernel,
        out_shape=jax.ShapeDtypeStruct((T, SB_WIDTH), F32),
        grid=(B, nb),
        in_specs=[
            pl.BlockSpec((QBLOCK, SB_WIDTH), lambda b, j: (b * nb + j, qoff)),
            pl.BlockSpec((S, SB_WIDTH), lambda b, j: (b, qoff + 1)),
            pl.BlockSpec((S, SB_WIDTH), lambda b, j: (b, qoff + 2)),
            pl.BlockSpec(uo.shape, lambda b, j: (0, 0)),
        ],
        out_specs=pl.BlockSpec((QBLOCK, SB_WIDTH), lambda b, j: (b * nb + j, 0)),
        scratch_shapes=[
            pltpu.VMEM((nb, SB_HEADS, LANES, QBLOCK), BF16),
            pltpu.VMEM((nb, SB_HEADS, QBLOCK, LANES), BF16),
            pltpu.VMEM((SB_HEADS, QBLOCK, 1), F32),
            pltpu.VMEM((SB_HEADS // 2, QBLOCK, LANES), F32),
            pltpu.VMEM((2 * SB_HEADS, QBLOCK, QBLOCK), F32),
            pltpu.VMEM((2 * SB_HEADS, QBLOCK, QBLOCK), BF16),
            pltpu.VMEM((2 * SB_HEADS, QBLOCK, QBLOCK), BF16),
            pltpu.VMEM((2 * SB_HEADS, QBLOCK, 1), F32),
            pltpu.VMEM((2 * SB_HEADS, QBLOCK, QBLOCK), BF16),
        ],
        compiler_params=_cparams(("parallel", "arbitrary")),
        name="stickbreak",
    )(z, z, z, uo)


def _merge_kernel(a_ref, b_ref, c_ref, u_ref, h_ref, wgz_ref, wb_ref, wo_ref, gn_ref, ho_ref, v_ref):
    u = u_ref[...]
    merged = None
    for n, br in enumerate((a_ref, b_ref, c_ref)):
        proj = _dot(br[...].astype(BF16), wb_ref[n])
        gate = _sigmoid(_dot(u, wgz_ref[:, n * D_MODEL:(n + 1) * D_MODEL]))
        merged = gate * proj if merged is None else merged + gate * proj
    hn = h_ref[...] + _dot(merged.astype(BF16), wo_ref[...])
    ho_ref[...] = hn
    ms = jnp.mean(hn * hn, axis=-1, keepdims=True)
    v_ref[...] = (hn * lax.rsqrt(ms + EPS) * gn_ref[...]).astype(BF16)


def _merge(a, b, c, u, h, wgz, wb, wo, gn, tm=256):
    T, D = h.shape
    bw = BRANCH_WIDTH
    return pl.pallas_call(
        _merge_kernel,
        out_shape=(jax.ShapeDtypeStruct((T, D), F32), jax.ShapeDtypeStruct((T, D), BF16)),
        grid=(T // tm,),
        in_specs=[
            pl.BlockSpec((tm, bw), lambda i: (i, 0)),
            pl.BlockSpec((tm, bw), lambda i: (i, 0)),
            pl.BlockSpec((tm, bw), lambda i: (i, 0)),
            pl.BlockSpec((tm, D), lambda i: (i, 0)),
            pl.BlockSpec((tm, D), lambda i: (i, 0)),
            pl.BlockSpec((D, GZW), lambda i: (0, 0)),
            pl.BlockSpec((N_BRANCH, bw, D), lambda i: (0, 0, 0)),
            pl.BlockSpec((D, D), lambda i: (0, 0)),
            pl.BlockSpec((1, D), lambda i: (0, 0)),
        ],
        out_specs=(pl.BlockSpec((tm, D), lambda i: (i, 0)), pl.BlockSpec((tm, D), lambda i: (i, 0))),
        compiler_params=_cparams(("parallel",)),
        name="merge",
    )(a, b, c, u, h, wgz, wb, wo, gn)


def _swiglu_kernel(v_ref, h_ref, wg_ref, wu_ref, wd_ref, o_ref, acc_ref):
    f = pl.program_id(1)

    @pl.when(f == 0)
    def _():
        acc_ref[...] = h_ref[...]

    v = v_ref[...]
    gt = _dot(v, wg_ref[...])
    up = _dot(v, wu_ref[...])
    act = (gt * _sigmoid(gt) * up).astype(BF16)
    acc_ref[...] += _dot(act, wd_ref[...])

    @pl.when(f == pl.num_programs(1) - 1)
    def _():
        o_ref[...] = acc_ref[...]


def _swiglu(v, h, wg, wu, wd, tm=512, tf=1408):
    T, D = h.shape
    F = wg.shape[1]
    return pl.pallas_call(
        _swiglu_kernel,
        out_shape=jax.ShapeDtypeStruct((T, D), F32),
        grid=(T // tm, F // tf),
        in_specs=[
            pl.BlockSpec((tm, D), lambda i, f: (i, 0)),
            pl.BlockSpec((tm, D), lambda i, f: (i, 0)),
            pl.BlockSpec((D, tf), lambda i, f: (0, f)),
            pl.BlockSpec((D, tf), lambda i, f: (0, f)),
            pl.BlockSpec((tf, D), lambda i, f: (f, 0)),
        ],
        out_specs=pl.BlockSpec((tm, D), lambda i, f: (i, 0)),
        scratch_shapes=[pltpu.VMEM((tm, D), F32)],
        compiler_params=_cparams(("parallel", "arbitrary")),
        name="swiglu",
    )(v, h, wg, wu, wd)


MOE_TILE = 1024
MOE_CHUNK = 288


def _router_kernel(h_ref, gn_ref, r_ref, us_ref, cw_ref, cwt_ref, rkt_ref):
    hh = h_ref[...]
    ms = jnp.mean(hh * hh, axis=-1, keepdims=True)
    vf = hh * lax.rsqrt(ms + EPS) * gn_ref[...]
    v_hi, v_lo = _split_bf16(vf)
    r_hi, r_lo = _split_bf16(r_ref[...])
    logits = _dot(v_hi, r_hi) + _dot(v_hi, r_lo) + _dot(v_lo, r_hi)
    lane = lax.broadcasted_iota(I32, logits.shape, 1)
    logits = jnp.where(lane < N_EXPERTS, logits, -jnp.inf)
    m1 = jnp.max(logits, axis=-1, keepdims=True)
    i1 = jnp.min(jnp.where(logits == m1, lane, LANES), axis=-1, keepdims=True)
    rest = jnp.where(lane == i1, -jnp.inf, logits)
    m2 = jnp.max(rest, axis=-1, keepdims=True)
    i2 = jnp.min(jnp.where(rest == m2, lane, LANES), axis=-1, keepdims=True)
    e2 = jnp.exp(m2 - m1)
    w1 = 1.0 / (1.0 + e2)
    cw = jnp.where(lane == i1, w1, 0.0) + jnp.where(lane == i2, e2 * w1, 0.0)
    cw_ref[...] = cw
    cwt = cw.T
    cwt_ref[...] = cwt[0:N_EXPERTS]
    sel = jnp.where(cwt != 0.0, 1.0, 0.0).astype(BF16)
    rkt_ref[...] = _dot(sel, us_ref[...])[0:N_EXPERTS]


def _router(h, gn, router, tm=MOE_TILE):
    T, D = h.shape
    us = jnp.asarray(np.triu(np.ones((tm, tm), np.float32), 1), BF16)
    return pl.pallas_call(
        _router_kernel,
        out_shape=(jax.ShapeDtypeStruct((T, LANES), F32),
                   jax.ShapeDtypeStruct((N_EXPERTS, T), F32),
                   jax.ShapeDtypeStruct((N_EXPERTS, T), F32)),
        grid=(T // tm,),
        in_specs=[
            pl.BlockSpec((tm, D), lambda i: (i, 0)),
            pl.BlockSpec((1, D), lambda i: (0, 0)),
            pl.BlockSpec((D, LANES), lambda i: (0, 0)),
            pl.BlockSpec((tm, tm), lambda i: (0, 0)),
        ],
        out_specs=(pl.BlockSpec((tm, LANES), lambda i: (i, 0)),
                   pl.BlockSpec((N_EXPERTS, tm), lambda i: (0, i)),
                   pl.BlockSpec((N_EXPERTS, tm), lambda i: (0, i))),
        compiler_params=_cparams(("parallel",)),
        name="router",
    )(h, gn, router, us)


MOE_SUB = 256
MOE_WIN = 112


def _moe_kernel(v_ref, h_ref, cw_ref, cwt_ref, rkt_ref, wg_ref, wu_ref, wd_ref, o_ref,
                xg_ref, ce_ref, ya_ref):
    e = pl.program_id(1)
    f = pl.program_id(2)
    nf = pl.num_programs(2)
    tm = v_ref.shape[0]
    CH, SUB, WIN = MOE_CHUNK, MOE_SUB, MOE_WIN
    nsub = tm // SUB

    @pl.when((e == 0) & (f == 0))
    def _():
        o_ref[...] = h_ref[...]

    ce_row = cwt_ref[pl.ds(e, 1), :]
    routed = ce_row != 0.0
    rank_row = rkt_ref[pl.ds(e, 1), :]
    slot_row = jnp.where(routed, rank_row, -1.0)
    count = jnp.sum(jnp.where(routed, 1, 0).astype(I32))
    n_chunks = (count + (CH - 1)) // CH
    bounds = [jnp.int32(0)]
    for s in range(1, nsub):
        bounds.append(rank_row[0, s * SUB].astype(I32))
    bounds.append(count)
    wiota = lax.broadcasted_iota(I32, (WIN, SUB), 0).astype(F32)

    def for_windows(fn):
        def window(s, r0):
            slots = slot_row[:, s * SUB:(s + 1) * SUB]
            pick = jnp.where(slots == wiota + r0.astype(F32), 1.0, 0.0).astype(BF16)
            fn(s, r0, pick)

        starts = [pl.multiple_of((bounds[s] // 16) * 16, 16) for s in range(nsub)]
        for s in range(nsub):
            window(s, starts[s])
        for s in range(nsub):
            nw = (bounds[s + 1] - starts[s] + (WIN - 1)) // WIN

            def body(m, carry, s=s):
                window(s, pl.multiple_of(starts[s] + m * WIN, 16))
                return carry

            lax.fori_loop(1, nw, body, 0)

    @pl.when(f == 0)
    def _():
        def clear(i, carry):
            rows = pl.ds(pl.multiple_of(i * CH, 16), CH)
            xg_ref[rows, :] = jnp.zeros((CH, xg_ref.shape[1]), BF16)
            ce_ref[rows, :] = jnp.zeros((CH, LANES), F32)
            ya_ref[rows, :] = jnp.zeros((CH, ya_ref.shape[1]), F32)
            return carry

        lax.fori_loop(0, jnp.minimum(n_chunks + 1, xg_ref.shape[0] // CH), clear, 0)
        lane = lax.broadcasted_iota(I32, (WIN, LANES), 1)

        def gather(s, r0, pick):
            toks = slice(s * SUB, (s + 1) * SUB)
            rows = pl.ds(r0, WIN)
            got = _dot(pick, v_ref[toks, :])
            xg_ref[rows, :] = (xg_ref[rows, :].astype(F32) + got).astype(BF16)
            c_hi, c_lo = _split_bf16(cw_ref[toks, :])
            cg = _dot(pick, c_hi) + _dot(pick, c_lo)
            ce = jnp.sum(jnp.where(lane == e, cg, 0.0), axis=-1, keepdims=True)
            ce_ref[rows, :] += jnp.broadcast_to(ce, (WIN, LANES))

        for_windows(gather)

    def expert(i, carry):
        rows = pl.ds(pl.multiple_of(i * CH, 16), CH)
        xg = xg_ref[rows, :]
        gt = _dot(xg, wg_ref[0])
        up = _dot(xg, wu_ref[0])
        act = (gt * _sigmoid(gt) * up * ce_ref[rows, 0:1]).astype(BF16)
        ya_ref[rows, :] += _dot(act, wd_ref[0])
        return carry

    lax.fori_loop(0, n_chunks, expert, 0)

    @pl.when(f == nf - 1)
    def _():
        def scatter(s, r0, pick):
            ye = ya_ref[pl.ds(r0, WIN), :].astype(BF16)
            o_ref[s * SUB:(s + 1) * SUB, :] += _dot_tn(pick, ye)

        for_windows(scatter)


def _moe(v, h, cw, cwt, rkt, wg, wu, wd, tm=MOE_TILE, fsplit=2):
    T, D = h.shape
    E, _, F = wg.shape
    tf = F // fsplit
    rows = (-(-tm // MOE_CHUNK)) * MOE_CHUNK
    assert rows >= tm + MOE_WIN and tm % MOE_SUB == 0
    return pl.pallas_call(
        _moe_kernel,
        out_shape=jax.ShapeDtypeStruct((T, D), F32),
        grid=(T // tm, E, fsplit),
        in_specs=[
            pl.BlockSpec((tm, D), lambda i, e, f: (i, 0), pipeline_mode=pl.Buffered(1)),
            pl.BlockSpec((tm, D), lambda i, e, f: (i, 0), pipeline_mode=pl.Buffered(1)),
            pl.BlockSpec((tm, LANES), lambda i, e, f: (i, 0), pipeline_mode=pl.Buffered(1)),
            pl.BlockSpec((N_EXPERTS, tm), lambda i, e, f: (0, i)),
            pl.BlockSpec((N_EXPERTS, tm), lambda i, e, f: (0, i)),
            pl.BlockSpec((1, D, tf), lambda i, e, f: (e, 0, f)),
            pl.BlockSpec((1, D, tf), lambda i, e, f: (e, 0, f)),
            pl.BlockSpec((1, tf, D), lambda i, e, f: (e, f, 0)),
        ],
        out_specs=pl.BlockSpec((tm, D), lambda i, e, f: (i, 0)),
        scratch_shapes=[
            pltpu.VMEM((rows, D), BF16),
            pltpu.VMEM((rows, LANES), F32),
            pltpu.VMEM((rows, D), F32),
        ],
        compiler_params=_cparams(("parallel", "arbitrary", "arbitrary")),
        name="moe",
    )(v, h, cw, cwt, rkt, wg, wu, wd)


def _ple_kernel(final, h_ref, p_ref, gn_ref, wg_ref, wp_ref, fn_ref, o_ref):
    h = h_ref[...]
    ms = jnp.mean(h * h, axis=-1, keepdims=True)
    u = (h * lax.rsqrt(ms + EPS) * gn_ref[...]).astype(BF16)
    gate = _sigmoid(_dot(u, wg_ref[...]))
    hn = h + gate * _dot(p_ref[0].astype(BF16), wp_ref[...])
    if final:
        ms = jnp.mean(hn * hn, axis=-1, keepdims=True)
        hn = hn * lax.rsqrt(ms + EPS) * fn_ref[...]
    o_ref[...] = hn


def _ple(h, p, layer, gn, wg, wp, fn, final, tm=512):
    T, D = h.shape
    return pl.pallas_call(
        functools.partial(_ple_kernel, final),
        out_shape=jax.ShapeDtypeStruct((T, D), F32),
        grid=(T // tm,),
        in_specs=[
            pl.BlockSpec((tm, D), lambda i: (i, 0)),
            pl.BlockSpec((1, tm, PLE_DIM), lambda i: (layer, i, 0)),
            pl.BlockSpec((1, D), lambda i: (0, 0)),
            pl.BlockSpec((D, D), lambda i: (0, 0)),
            pl.BlockSpec((PLE_DIM, D), lambda i: (0, 0)),
            pl.BlockSpec((1, D), lambda i: (0, 0)),
        ],
        out_specs=pl.BlockSpec((tm, D), lambda i: (i, 0)),
        compiler_params=_cparams(("parallel",)),
        name="ple",
    )(h, p, gn, wg, wp, fn)


def _reorder_kernel(w_ref, o_ref, ogz_ref):
    o_sq = 4 * HG_WIDTH
    o_sc = o_sq + SA_WIDTH
    o_iq = o_sc + SA_LATENT
    o_ik = o_iq + IDX_HEADS * IDX_DIM
    o_bq = o_ik + IDX_DIM + IDX_HEADS
    o_gz = o_bq + 3 * SB_WIDTH

    def put(dst, lo, hi):
        o_ref[:, dst:dst + (hi - lo)] = w_ref[0, :, lo:hi].astype(BF16)

    ogz_ref[...] = w_ref[0, :, o_gz:o_gz + GZW].astype(BF16)
    put(HG_OFF, 0, o_sq)
    put(SQ_OFF, o_sq, o_sc)
    put(IQ_OFF, o_iq, o_ik)
    put(SC_OFF, o_sc, o_iq)
    put(IKW_OFF, o_ik, o_bq)
    used = IDX_DIM + IDX_HEADS
    o_ref[:, IKW_OFF + used:SB_OFF] = jnp.zeros((o_ref.shape[0], LANES - used), BF16)
    put(SB_OFF, o_bq, o_gz)


def _reorder_w_in(w_in, layer, tr=256):
    _, R, C = w_in.shape
    return pl.pallas_call(
        _reorder_kernel,
        out_shape=(jax.ShapeDtypeStruct((R, ZW), BF16), jax.ShapeDtypeStruct((R, GZW), BF16)),
        grid=(R // tr,),
        in_specs=[pl.BlockSpec((1, tr, C), lambda i: (layer, i, 0))],
        out_specs=(pl.BlockSpec((tr, ZW), lambda i: (i, 0)), pl.BlockSpec((tr, GZW), lambda i: (i, 0))),
        compiler_params=_cparams(("parallel",)),
        name="reorder_w_in",
    )(w_in)


def kernel(x, p, lb_param, norm_mix, w_in, hg_onorm, sa_cnorm, sa_wk, sa_wv, w_branch, w_out,
           norm_ffn, dense_wg, dense_wu, dense_wd, moe_router, moe_wg, moe_wu, moe_wd, norm_ple,
           ple_gate, ple_proj, norm_final):
    B, S, D = x.shape
    T = B * S
    depth = w_in.shape[0]
    lb_sm = jax.nn.softmax(lb_param.astype(F32), axis=0)
    lb_all = jnp.cumsum(lb_sm, axis=0) - lb_sm[0:1]
    h = x.reshape(T, D)
    for i in range(depth):
        w_mix, w_gz = _reorder_w_in(w_in, i)
        z, u = _inproj(h, norm_mix[i][None, :], w_mix)
        a = _hgrn(z, lb_all[i][None, :], hg_onorm[i][None, :], B, S)
        b = _dsa(z, sa_cnorm[i][None, :], sa_wk[i].astype(BF16), sa_wv[i].T.astype(BF16), B, S)
        c = _sb(z, B, S)
        h, v = _merge(a, b, c, u, h, w_gz, w_branch[i].astype(BF16), w_out[i].astype(BF16),
                      norm_ffn[i][None, :])
        jj = i // 2
        if i % 2 == 0:
            h = _swiglu(v, h, dense_wg[jj].astype(BF16), dense_wu[jj].astype(BF16),
                        dense_wd[jj].astype(BF16))
        else:
            router = jnp.pad(moe_router[jj], ((0, 0), (0, LANES - N_EXPERTS)))
            cw, cwt, rkt = _router(h, norm_ffn[i][None, :], router)
            h = _moe(v, h, cw, cwt, rkt, moe_wg[jj].astype(BF16), moe_wu[jj].astype(BF16),
                     moe_wd[jj].astype(BF16))
        h = _ple(h, p.reshape(depth, T, PLE_DIM), i, norm_ple[i][None, :],
                 ple_gate[i].astype(BF16), ple_proj[i].astype(BF16), norm_final[None, :],
                 i == depth - 1)
    return h.reshape(B, S, D)
```

```python
import functools

import numpy as np
import jax
import jax.numpy as jnp
from jax import lax
from jax.experimental import pallas as pl
from jax.experimental.pallas import tpu as pltpu

F32 = jnp.float32
BF16 = jnp.bfloat16
I32 = jnp.int32

D_MODEL = 1024
CHUNK = 64
QBLOCK = 128
HG_HEADS = 4
HG_DK = 128
HG_WIDTH = HG_HEADS * HG_DK
SA_HEADS = 8
SA_DH = 64
SA_WIDTH = SA_HEADS * SA_DH
SA_LATENT = 128
IDX_HEADS = 4
IDX_DIM = 64
TOPK_MAX = 256
SB_HEADS = 8
SB_DH = 64
SB_WIDTH = SB_HEADS * SB_DH
N_BRANCH = 3
BRANCH_WIDTH = 512
N_EXPERTS = 8
PLE_DIM = 256
EPS = 1e-6

LANES = 128

HG_OFF = 0
SQ_OFF = HG_OFF + 4 * HG_WIDTH
IQ_OFF = SQ_OFF + SA_WIDTH
SC_OFF = IQ_OFF + IDX_HEADS * IDX_DIM
IKW_OFF = SC_OFF + SA_LATENT
SB_OFF = IKW_OFF + LANES
ZW = SB_OFF + 3 * SB_WIDTH
GZW = N_BRANCH * D_MODEL

VMEM_LIMIT = 56 * 1024 * 1024

NEG_BIG = -1e30
INT_MIN = -2147483648
LOG2E = 1.4426950408889634
SB_UNDERFLOW = -104.0


def _cparams(sem):
    return pltpu.CompilerParams(dimension_semantics=sem, vmem_limit_bytes=VMEM_LIMIT)


def _sigmoid(x):
    e = jnp.exp(-jnp.abs(x))
    inv = 1.0 / (1.0 + e)
    return jnp.where(x >= 0, inv, e * inv)


def _dot(a, b):
    return jnp.dot(a, b, preferred_element_type=F32)


def _dot_nt(a, b):
    return lax.dot_general(a, b, (((1,), (1,)), ((), ())), preferred_element_type=F32)


def _dot_tn(a, b):
    return lax.dot_general(a, b, (((0,), (0,)), ((), ())), preferred_element_type=F32)


def _split_bf16(x):
    hi = x.astype(BF16)
    lo = (x - hi.astype(F32)).astype(BF16)
    return hi, lo


def _inproj_kernel(x_ref, g_ref, w_ref, o_ref, u_ref):
    @pl.when(pl.program_id(1) == 0)
    def _():
        x = x_ref[...]
        ms = jnp.mean(x * x, axis=-1, keepdims=True)
        u_ref[...] = (x * lax.rsqrt(ms + EPS) * g_ref[...]).astype(BF16)

    o_ref[...] = _dot(u_ref[...], w_ref[...])


def _inproj(h, gain, w, tm=1024, tn=1536):
    T, D = h.shape
    N = w.shape[1]
    return pl.pallas_call(
        _inproj_kernel,
        out_shape=(jax.ShapeDtypeStruct((T, N), F32), jax.ShapeDtypeStruct((T, D), BF16)),
        grid=(T // tm, N // tn),
        in_specs=[
            pl.BlockSpec((tm, D), lambda i, j: (i, 0)),
            pl.BlockSpec((1, D), lambda i, j: (0, 0)),
            pl.BlockSpec((D, tn), lambda i, j: (0, j)),
        ],
        out_specs=(pl.BlockSpec((tm, tn), lambda i, j: (i, j)),
                   pl.BlockSpec((tm, D), lambda i, j: (i, 0))),
        compiler_params=_cparams(("parallel", "arbitrary")),
        name="inproj",
    )(h, gain, w)


_HG_LEVELS = (64, 32, 16, 8, 4, 2)


def _hgrn_consts():
    C = CHUNK
    blocks, masks = [], []
    for n in _HG_LEVELS:
        half = n // 2
        L = np.zeros((C, C), np.float32)
        M = np.zeros((C, C), np.float32)
        for t in range(C):
            base = (t // n) * n
            mid = base + half
            if t >= mid:
                L[t, mid:t + 1] = 1.0
                M[t, base:mid] = 1.0
            else:
                L[t, t + 1:mid] = 1.0
        blocks.append(L)
        masks.append(M)
    masks.append(np.eye(C, dtype=np.float32))
    blocks.append(np.tril(np.ones((C, C), np.float32)))
    blocks.append(np.triu(np.ones((C, C), np.float32), 1))
    return np.concatenate(blocks, 0), np.stack(masks)


def _hgrn_kernel(q_ref, f_ref, i_ref, g_ref, lb_ref, on_ref, lc_ref, mc_ref, o_ref, st_ref,
                 qq_ref, kk_ref, ex_ref, sc_ref):
    C = CHUNK
    nlev = len(_HG_LEVELS)

    @pl.when(pl.program_id(1) == 0)
    def _():
        st_ref[...] = jnp.zeros(st_ref.shape, F32)

    lc = lc_ref[...]
    onorm = on_ref[...]
    n_chunks = q_ref.shape[0] // C

    def chunk(c, carry):
        r0 = pl.multiple_of(c * C, C)
        rows = pl.ds(r0, C)
        fpre = f_ref[rows, :]
        lb = lb_ref[...]
        e = jnp.exp(-jnp.abs(fpre))
        inv = 1.0 / (1.0 + e)
        sg_pos = jnp.where(fpre >= 0, inv, e * inv)
        sg_neg = jnp.where(fpre >= 0, e * inv, inv)
        lf = jnp.log(lb + (1.0 - lb) * sg_pos)
        kk_ref[...] = (1.0 - lb) * sg_neg
        qin = q_ref[rows, :]
        qq_ref[...] = qin * _sigmoid(qin)
        lf_hi, lf_lo = _split_bf16(lf)
        ex_ref[...] = jnp.exp(_dot(lc, lf_hi) + _dot(lc, lf_lo))
        for h in range(HG_HEADS):
            cs = slice(h * HG_DK, (h + 1) * HG_DK)
            q = qq_ref[:, cs]
            k = kk_ref[:, cs]
            scores = _dot_nt(q.astype(BF16), k.astype(BF16)) * mc_ref[nlev]
            for l in range(nlev):
                el = ex_ref[l * C:(l + 1) * C, cs]
                s_l = _dot_nt((q * el).astype(BF16), (k * el).astype(BF16))
                scores = scores + s_l * mc_ref[l]
            sc_ref[h] = scores.astype(BF16)
        for h in range(HG_HEADS):
            cs = slice(h * HG_DK, (h + 1) * HG_DK)
            eb = ex_ref[nlev * C:(nlev + 1) * C, cs]
            er = ex_ref[(nlev + 1) * C:(nlev + 2) * C, cs]
            st = st_ref[h]
            vb = i_ref[rows, cs].astype(BF16)
            o = _dot(sc_ref[h], vb) + _dot_nt((qq_ref[:, cs] * eb).astype(BF16), st.astype(BF16))
            st_ref[h] = st * eb[C - 1:C, :] + _dot_tn(vb, (kk_ref[:, cs] * er).astype(BF16))
            ms = jnp.mean(o * o, axis=-1, keepdims=True)
            on = o * lax.rsqrt(ms + EPS) * onorm
            g = g_ref[rows, cs]
            o_ref[rows, cs] = on * (g * _sigmoid(g))
        return carry

    lax.fori_loop(0, n_chunks, chunk, 0)


def _hgrn(z, lb, onorm, B, S, sblk=1024):
    T = B * S
    sblk = min(sblk, S)
    ns = S // sblk
    lc_np, mc_np = _hgrn_consts()
    lc = jnp.asarray(lc_np, BF16)
    mc = jnp.asarray(mc_np, F32)
    cb = HG_OFF // HG_WIDTH

    def zspec(k):
        return pl.BlockSpec((sblk, HG_WIDTH), lambda b, s, k=k: (b * ns + s, cb + k))

    return pl.pallas_call(
        _hgrn_kernel,
        out_shape=jax.ShapeDtypeStruct((T, HG_WIDTH), F32),
        grid=(B, ns),
        in_specs=[
            zspec(0), zspec(1), zspec(2), zspec(3),
            pl.BlockSpec((1, HG_WIDTH), lambda b, s: (0, 0)),
            pl.BlockSpec((1, HG_DK), lambda b, s: (0, 0)),
            pl.BlockSpec(lc.shape, lambda b, s: (0, 0)),
            pl.BlockSpec(mc.shape, lambda b, s: (0, 0, 0)),
        ],
        out_specs=pl.BlockSpec((sblk, HG_WIDTH), lambda b, s: (b * ns + s, 0)),
        scratch_shapes=[
            pltpu.VMEM((HG_HEADS, HG_DK, HG_DK), F32),
            pltpu.VMEM((CHUNK, HG_WIDTH), F32),
            pltpu.VMEM((CHUNK, HG_WIDTH), F32),
            pltpu.VMEM(((len(_HG_LEVELS) + 2) * CHUNK, HG_WIDTH), F32),
            pltpu.VMEM((HG_HEADS, CHUNK, CHUNK), BF16),
        ],
        compiler_params=_cparams(("parallel", "arbitrary")),
        name="hgrn2",
    )(z, z, z, z, lb, onorm, lc, mc)


def _dsa_kernel(topk, sq_ref, iq_ref, iwq_ref, sc_ref, ikw_ref, cn_ref, wk_ref, wvt_ref,
                o_ref, kk_ref, vt_ref, ik_ref, key_ref, plane_ref, qt_ref, m_ref, l_ref, acc_ref):
    QB = QBLOCK
    j = pl.program_id(1)
    nkb = j + 1
    S = sc_ref.shape[0]
    nb = S // QB
    idx_bits = (S - 1).bit_length()

    @pl.when(j == 0)
    def _():
        c = sc_ref[...]
        ms = jnp.mean(c * c, axis=-1, keepdims=True)
        cb = (c * lax.rsqrt(ms + EPS) * cn_ref[...]).astype(BF16)
        kfull = _dot(cb, wk_ref[...]).astype(BF16)
        vtfull = _dot_nt(wvt_ref[...], cb).astype(BF16)
        ikfull = ikw_ref[:, 0:IDX_DIM].astype(BF16)
        for kb in range(nb):
            kk_ref[kb] = kfull[kb * QB:(kb + 1) * QB]
            vt_ref[kb] = vtfull[:, kb * QB:(kb + 1) * QB]
            ik_ref[kb] = ikfull[kb * QB:(kb + 1) * QB]

    qt_ref[...] = (sq_ref[...] * (SA_DH ** -0.5 * LOG2E)).T.astype(BF16)
    iqt = iq_ref[...].T.astype(BF16)
    iwt = iwq_ref[...].T * (IDX_HEADS ** -0.5)

    row = lax.broadcasted_iota(I32, (QB, QB), 0)
    lane = lax.broadcasted_iota(I32, (QB, QB), 1)
    diag_bad = (row >= CHUNK) & (lane < CHUNK)

    n_pairs = (nkb + 1) // 2

    def score_block(kb):
        ikb = ik_ref[kb]
        acc = jnp.zeros((QB, QB), F32)
        for h in range(IDX_HEADS):
            s_h = _dot(ikb, iqt[h * IDX_DIM:(h + 1) * IDX_DIM])
            acc = acc + jnp.maximum(s_h, 0.0) * iwt[IDX_DIM + h:IDX_DIM + h + 1, :]
        acc = acc + 0.0
        acc = jnp.where((kb == j) & diag_bad, -jnp.inf, acc)
        bits = pltpu.bitcast(acc, I32)
        key_ref[kb] = jnp.where(bits < 0, bits ^ jnp.int32(0x7FFFFFFF), bits)

    def score_pair(g, carry):
        score_block(2 * g)
        score_block(2 * g + 1)
        return carry

    lax.fori_loop(0, n_pairs, score_pair, 0)

    @pl.when(j + 1 < 2 * n_pairs)
    def _():
        key_ref[j + 1] = jnp.full((QB, QB), INT_MIN, I32)

    def build_planes(g, carry):
        a = []
        for half in range(2):
            blk = key_ref[2 * g + half] ^ jnp.int32(INT_MIN)
            a.extend(blk[r * 8:(r + 1) * 8, :] for r in range(QB // 8))
        for sh, m in ((16, 0x0000FFFF), (8, 0x00FF00FF), (4, 0x0F0F0F0F), (2, 0x33333333),
                      (1, 0x55555555)):
            for k in range(32):
                if k & sh == 0:
                    t = (lax.shift_right_logical(a[k], jnp.int32(sh)) ^ a[k + sh]) & jnp.int32(m)
                    a[k + sh] = a[k + sh] ^ t
                    a[k] = a[k] ^ lax.shift_left(t, jnp.int32(sh))
        rows = pl.ds(pl.multiple_of(g * 8, 8), 8)
        for b in range(32):
            plane_ref[b, rows, :] = a[b]
        return carry

    def search(_):
        lax.fori_loop(0, n_pairs, build_planes, 0)
        wrow = lax.broadcasted_iota(I32, (nb // 2 * 8, QB), 0)
        cand = jnp.where((wrow >> 3) < n_pairs, jnp.int32(-1), jnp.int32(0))
        above = jnp.zeros((1, QB), I32)
        u = jnp.zeros((1, QB), I32)
        for b in range(31, -1, -1):
            ones = cand & plane_ref[b]
            cnt1 = jnp.sum(lax.population_count(ones), axis=0, keepdims=True)
            take = (above + cnt1) >= topk
            cand = jnp.where(take, ones, cand ^ ones)
            above = jnp.where(take, above, above + cnt1)
            u = u | jnp.where(take, jnp.int32(INT_MIN if b == 31 else 1 << b), jnp.int32(0))
        thr = u ^ jnp.int32(INT_MIN)
        need = topk - above
        n_tied = jnp.sum(lax.population_count(cand), axis=0, keepdims=True)

        def break_ties(_):
            base = (wrow >> 3) * (2 * QB) + (wrow & 7)

            def ibit(i, jp):
                c = jp + lax.shift_left(jnp.int32(1), idx_bits - 1 - i)
                nbits = jnp.clip((c - base + 7) >> 3, 0, 32)
                below = jnp.where(nbits >= 32, jnp.int32(-1),
                                  lax.shift_left(jnp.int32(1), nbits) - 1)
                cnt = jnp.sum(lax.population_count(cand & below), axis=0, keepdims=True)
                return jnp.where(cnt < need, c, jp)

            return lax.fori_loop(0, idx_bits, ibit, jnp.zeros((1, QB), I32))

        def keep_ties(_):
            return jnp.full((1, QB), S, I32)

        jp = lax.cond(jnp.max(n_tied - need) > 0, break_ties, keep_ties, 0)
        return thr, jp

    def take_all(_):
        return (jnp.full((1, QB), INT_MIN, I32), jnp.full((1, QB), -1, I32))

    thr, jp = lax.cond(nkb * QB > topk, search, take_all, 0)

    m_ref[...] = jnp.full(m_ref.shape, NEG_BIG, F32)
    l_ref[...] = jnp.zeros(l_ref.shape, F32)
    acc_ref[...] = jnp.zeros(acc_ref.shape, F32)

    def attend_block(kb):
        key = key_ref[kb]
        kidx = row + kb * QB
        sel = (key > thr) | ((key == thr) & (kidx <= jp))
        hidden = (kidx >= j * QB + CHUNK) & (lane < CHUNK)
        bias = jnp.where(sel & jnp.logical_not(hidden), 0.0, -jnp.inf)
        kblk = kk_ref[kb]
        vtb = vt_ref[kb]
        for h in range(SA_HEADS):
            lg = _dot(kblk, qt_ref[h * SA_DH:(h + 1) * SA_DH, :]) + bias
            m_old = m_ref[h:h + 1, :]
            m_new = jnp.maximum(m_old, jnp.max(lg, axis=0, keepdims=True))
            alpha = jnp.exp2(m_old - m_new)
            p = jnp.exp2(lg - m_new)
            l_ref[h:h + 1, :] = alpha * l_ref[h:h + 1, :] + jnp.sum(p, axis=0, keepdims=True)
            acc_ref[h] = acc_ref[h] * alpha + _dot(vtb, p.astype(BF16))
            m_ref[h:h + 1, :] = m_new

    def attend(g, carry):
        for i in range(4):
            attend_block(4 * g + i)
        return carry

    n_quads = nkb // 4
    lax.fori_loop(0, n_quads, attend, 0)

    @pl.when((nkb & 2) != 0)
    def _():
        attend_block(4 * n_quads)
        attend_block(4 * n_quads + 1)

    @pl.when((nkb & 1) != 0)
    def _():
        attend_block(nkb - 1)

    outs = []
    for h in range(SA_HEADS):
        outs.append(acc_ref[h] * (1.0 / l_ref[h:h + 1, :]))
    o_ref[...] = jnp.concatenate(outs, axis=0).T


def _dsa(z, cnorm, wk, wvt, B, S):
    T = B * S
    nb = S // QBLOCK
    topk = min(TOPK_MAX, S // 4)
    assert nb % 2 == 0
    kern = functools.partial(_dsa_kernel, topk)
    return pl.pallas_call(
        kern,
        out_shape=jax.ShapeDtypeStruct((T, SA_WIDTH), F32),
        grid=(B, nb),
        in_specs=[
            pl.BlockSpec((QBLOCK, SA_WIDTH), lambda b, j: (b * nb + j, SQ_OFF // SA_WIDTH)),
            pl.BlockSpec((QBLOCK, IDX_HEADS * IDX_DIM),
                         lambda b, j: (b * nb + j, IQ_OFF // (IDX_HEADS * IDX_DIM))),
            pl.BlockSpec((QBLOCK, LANES), lambda b, j: (b * nb + j, IKW_OFF // LANES)),
            pl.BlockSpec((S, SA_LATENT), lambda b, j: (b, SC_OFF // SA_LATENT)),
            pl.BlockSpec((S, LANES), lambda b, j: (b, IKW_OFF // LANES)),
            pl.BlockSpec((1, SA_LATENT), lambda b, j: (0, 0)),
            pl.BlockSpec((SA_LATENT, SA_DH), lambda b, j: (0, 0)),
            pl.BlockSpec((SA_DH, SA_LATENT), lambda b, j: (0, 0)),
        ],
        out_specs=pl.BlockSpec((QBLOCK, SA_WIDTH), lambda b, j: (b * nb + j, 0)),
        scratch_shapes=[
            pltpu.VMEM((nb, QBLOCK, SA_DH), BF16),
            pltpu.VMEM((nb, SA_DH, QBLOCK), BF16),
            pltpu.VMEM((nb, QBLOCK, IDX_DIM), BF16),
            pltpu.VMEM((nb, QBLOCK, QBLOCK), I32),
            pltpu.VMEM((32, nb // 2 * 8, QBLOCK), I32),
            pltpu.VMEM((SA_WIDTH, QBLOCK), BF16),
            pltpu.VMEM((SA_HEADS, QBLOCK), F32),
            pltpu.VMEM((SA_HEADS, QBLOCK), F32),
            pltpu.VMEM((SA_HEADS, SA_DH, QBLOCK), F32),
        ],
        compiler_params=_cparams(("parallel", "arbitrary")),
        name="dsa",
    )(z, z, z, z, z, cnorm, wk, wvt)


def _sb_kernel(q_ref, k_ref, v_ref, uo_ref, o_ref, kt_ref, vb_ref, run_ref, acc_ref,
               ls_ref, lkh_ref, lkl_ref, tot_ref, a_ref):
    QB = QBLOCK
    j = pl.program_id(1)
    S = k_ref.shape[0]
    nb = S // QB
    npair = SB_HEADS // 2
    lane = lax.broadcasted_iota(I32, (QB, QB), 1)
    row = lax.broadcasted_iota(I32, (QB, QB), 0)

    @pl.when(j == 0)
    def _():
        for p in range(npair):
            cols = slice(p * LANES, (p + 1) * LANES)
            kt = k_ref[:, cols].T
            vv = v_ref[:, cols]
            for hh in range(2):
                lo = hh * SB_DH
                ktz = jnp.where((row >= lo) & (row < lo + SB_DH), 1.0, 0.0)
                vz = jnp.where((lane >= lo) & (lane < lo + SB_DH), 1.0, 0.0)
                for kb in range(nb):
                    ks = slice(kb * QB, (kb + 1) * QB)
                    kt_ref[kb, 2 * p + hh] = (kt[:, ks] * ktz).astype(BF16)
                    vb_ref[kb, 2 * p + hh] = (vv[ks, :] * vz).astype(BF16)

    uo = uo_ref[...]
    strict = lane < row
    qs = (q_ref[...] * (SB_DH ** -0.5)).astype(BF16)

    def all_heads(kbs, diag):
        nk = len(kbs)
        for h in range(SB_HEADS):
            p = h // 2
            for i, kb in enumerate(kbs):
                s = h * nk + i
                z = _dot(qs[:, p * LANES:(p + 1) * LANES], kt_ref[kb, h])
                l1p = jnp.log(1.0 + jnp.exp(-jnp.abs(z)))
                ls_pos = jnp.minimum(z, 0.0) - l1p
                lk = ls_pos - z
                if diag:
                    lk = jnp.where(strict, lk, 0.0)
                lk_hi, lk_lo = _split_bf16(lk)
                ls_ref[s] = ls_pos
                lkh_ref[s] = lk_hi
                lkl_ref[s] = lk_lo
                tot_ref[s] = jnp.sum(lk, axis=1, keepdims=True)
        top = None
        for h in range(SB_HEADS):
            run = None if diag else run_ref[h]
            for i in range(nk):
                s = h * nk + i
                after = _dot(lkh_ref[s], uo) + _dot(lkl_ref[s], uo)
                if diag:
                    a = jnp.where(strict, jnp.exp(ls_ref[s] + after), 0.0)
                    run = tot_ref[s]
                else:
                    a = jnp.exp(ls_ref[s] + after + run)
                    run = run + tot_ref[s]
                a_ref[s] = a.astype(BF16)
            run_ref[h] = run
            top = run if top is None else jnp.maximum(top, run)
        for p in range(npair):
            pv = None
            for h in (2 * p, 2 * p + 1):
                for i, kb in enumerate(kbs):
                    out = _dot(a_ref[h * nk + i], vb_ref[kb, h])
                    pv = out if pv is None else pv + out
            if diag:
                acc_ref[p] = pv
            else:
                acc_ref[p] += pv
        return jnp.max(top) > SB_UNDERFLOW

    alive = all_heads([j], True)

    def cond(c):
        return (c[0] >= 1) & c[1]

    def body(c):
        return c[0] - 2, all_heads([c[0], c[0] - 1], False)

    kb_left, alive = lax.while_loop(cond, body, (j - 1, alive))

    @pl.when((kb_left == 0) & alive)
    def _():
        all_heads([0], False)

    o_ref[...] = jnp.concatenate([acc_ref[p] for p in range(npair)], axis=1)


def _sb(z, B, S):
    T = B * S
    nb = S // QBLOCK
    u = np.tril(np.ones((QBLOCK, QBLOCK), np.float32), -1)
    uo = jnp.asarray(u, BF16)
    qoff = SB_OFF // SB_WIDTH
    return pl.pallas_call(
        _sb_kernel,
        out_shape=jax.ShapeDtypeStruct((T, SB_WIDTH), F32),
        grid=(B, nb),
        in_specs=[
            pl.BlockSpec((QBLOCK, SB_WIDTH), lambda b, j: (b * nb + j, qoff)),
            pl.BlockSpec((S, SB_WIDTH), lambda b, j: (b, qoff + 1)),
            pl.BlockSpec((S, SB_WIDTH), lambda b, j: (b, qoff + 2)),
            pl.BlockSpec(uo.shape, lambda b, j: (0, 0)),
        ],
        out_specs=pl.BlockSpec((QBLOCK, SB_WIDTH), lambda b, j: (b * nb + j, 0)),
        scratch_shapes=[
            pltpu.VMEM((nb, SB_HEADS, LANES, QBLOCK), BF16),
            pltpu.VMEM((nb, SB_HEADS, QBLOCK, LANES), BF16),
            pltpu.VMEM((SB_HEADS, QBLOCK, 1), F32),
            pltpu.VMEM((SB_HEADS // 2, QBLOCK, LANES), F32),
            pltpu.VMEM((2 * SB_HEADS, QBLOCK, QBLOCK), F32),
            pltpu.VMEM((2 * SB_HEADS, QBLOCK, QBLOCK), BF16),
            pltpu.VMEM((2 * SB_HEADS, QBLOCK, QBLOCK), BF16),
            pltpu.VMEM((2 * SB_HEADS, QBLOCK, 1), F32),
            pltpu.VMEM((2 * SB_HEADS, QBLOCK, QBLOCK), BF16),
        ],
        compiler_params=_cparams(("parallel", "arbitrary")),
        name="stickbreak",
    )(z, z, z, uo)


def _merge_kernel(a_ref, b_ref, c_ref, u_ref, h_ref, wgz_ref, wb_ref, wo_ref, gn_ref, ho_ref, v_ref):
    u = u_ref[...]
    merged = None
    for n, br in enumerate((a_ref, b_ref, c_ref)):
        proj = _dot(br[...].astype(BF16), wb_ref[n])
        gate = _sigmoid(_dot(u, wgz_ref[:, n * D_MODEL:(n + 1) * D_MODEL]))
        merged = gate * proj if merged is None else merged + gate * proj
    hn = h_ref[...] + _dot(merged.astype(BF16), wo_ref[...])
    ho_ref[...] = hn
    ms = jnp.mean(hn * hn, axis=-1, keepdims=True)
    v_ref[...] = (hn * lax.rsqrt(ms + EPS) * gn_ref[...]).astype(BF16)


def _merge(a, b, c, u, h, wgz, wb, wo, gn, tm=256):
    T, D = h.shape
    bw = BRANCH_WIDTH
    return pl.pallas_call(
        _merge_kernel,
        out_shape=(jax.ShapeDtypeStruct((T, D), F32), jax.ShapeDtypeStruct((T, D), BF16)),
        grid=(T // tm,),
        in_specs=[
            pl.BlockSpec((tm, bw), lambda i: (i, 0)),
            pl.BlockSpec((tm, bw), lambda i: (i, 0)),
            pl.BlockSpec((tm, bw), lambda i: (i, 0)),
            pl.BlockSpec((tm, D), lambda i: (i, 0)),
            pl.BlockSpec((tm, D), lambda i: (i, 0)),
            pl.BlockSpec((D, GZW), lambda i: (0, 0)),
            pl.BlockSpec((N_BRANCH, bw, D), lambda i: (0, 0, 0)),
            pl.BlockSpec((D, D), lambda i: (0, 0)),
            pl.BlockSpec((1, D), lambda i: (0, 0)),
        ],
        out_specs=(pl.BlockSpec((tm, D), lambda i: (i, 0)), pl.BlockSpec((tm, D), lambda i: (i, 0))),
        compiler_params=_cparams(("parallel",)),
        name="merge",
    )(a, b, c, u, h, wgz, wb, wo, gn)


def _swiglu_kernel(v_ref, h_ref, wg_ref, wu_ref, wd_ref, o_ref, acc_ref):
    f = pl.program_id(1)

    @pl.when(f == 0)
    def _():
        acc_ref[...] = h_ref[...]

    v = v_ref[...]
    gt = _dot(v, wg_ref[...])
    up = _dot(v, wu_ref[...])
    act = (gt * _sigmoid(gt) * up).astype(BF16)
    acc_ref[...] += _dot(act, wd_ref[...])

    @pl.when(f == pl.num_programs(1) - 1)
    def _():
        o_ref[...] = acc_ref[...]


def _swiglu(v, h, wg, wu, wd, tm=512, tf=1408):
    T, D = h.shape
    F = wg.shape[1]
    return pl.pallas_call(
        _swiglu_kernel,
        out_shape=jax.ShapeDtypeStruct((T, D), F32),
        grid=(T // tm, F // tf),
        in_specs=[
            pl.BlockSpec((tm, D), lambda i, f: (i, 0)),
            pl.BlockSpec((tm, D), lambda i, f: (i, 0)),
            pl.BlockSpec((D, tf), lambda i, f: (0, f)),
            pl.BlockSpec((D, tf), lambda i, f: (0, f)),
            pl.BlockSpec((tf, D), lambda i, f: (f, 0)),
        ],
        out_specs=pl.BlockSpec((tm, D), lambda i, f: (i, 0)),
        scratch_shapes=[pltpu.VMEM((tm, D), F32)],
        compiler_params=_cparams(("parallel", "arbitrary")),
        name="swiglu",
    )(v, h, wg, wu, wd)


MOE_TILE = 1024
MOE_CHUNK = 288


def _router_kernel(h_ref, gn_ref, r_ref, us_ref, cw_ref, cwt_ref, rkt_ref):
    hh = h_ref[...]
    ms = jnp.mean(hh * hh, axis=-1, keepdims=True)
    vf = hh * lax.rsqrt(ms + EPS) * gn_ref[...]
    v_hi, v_lo = _split_bf16(vf)
    r_hi, r_lo = _split_bf16(r_ref[...])
    logits = _dot(v_hi, r_hi) + _dot(v_hi, r_lo) + _dot(v_lo, r_hi)
    lane = lax.broadcasted_iota(I32, logits.shape, 1)
    logits = jnp.where(lane < N_EXPERTS, logits, -jnp.inf)
    m1 = jnp.max(logits, axis=-1, keepdims=True)
    i1 = jnp.min(jnp.where(logits == m1, lane, LANES), axis=-1, keepdims=True)
    rest = jnp.where(lane == i1, -jnp.inf, logits)
    m2 = jnp.max(rest, axis=-1, keepdims=True)
    i2 = jnp.min(jnp.where(rest == m2, lane, LANES), axis=-1, keepdims=True)
    e2 = jnp.exp(m2 - m1)
    w1 = 1.0 / (1.0 + e2)
    cw = jnp.where(lane == i1, w1, 0.0) + jnp.where(lane == i2, e2 * w1, 0.0)
    cw_ref[...] = cw
    cwt = cw.T
    cwt_ref[...] = cwt[0:N_EXPERTS]
    sel = jnp.where(cwt != 0.0, 1.0, 0.0).astype(BF16)
    rkt_ref[...] = _dot(sel, us_ref[...])[0:N_EXPERTS]


def _router(h, gn, router, tm=MOE_TILE):
    T, D = h.shape
    us = jnp.asarray(np.triu(np.ones((tm, tm), np.float32), 1), BF16)
    return pl.pallas_call(
        _router_kernel,
        out_shape=(jax.ShapeDtypeStruct((T, LANES), F32),
                   jax.ShapeDtypeStruct((N_EXPERTS, T), F32),
                   jax.ShapeDtypeStruct((N_EXPERTS, T), F32)),
        grid=(T // tm,),
        in_specs=[
            pl.BlockSpec((tm, D), lambda i: (i, 0)),
            pl.BlockSpec((1, D), lambda i: (0, 0)),
            pl.BlockSpec((D, LANES), lambda i: (0, 0)),
            pl.BlockSpec((tm, tm), lambda i: (0, 0)),
        ],
        out_specs=(pl.BlockSpec((tm, LANES), lambda i: (i, 0)),
                   pl.BlockSpec((N_EXPERTS, tm), lambda i: (0, i)),
                   pl.BlockSpec((N_EXPERTS, tm), lambda i: (0, i))),
        compiler_params=_cparams(("parallel",)),
        name="router",
    )(h, gn, router, us)


MOE_SUB = 256
MOE_WIN = 112


def _moe_kernel(v_ref, h_ref, cw_ref, cwt_ref, rkt_ref, wg_ref, wu_ref, wd_ref, o_ref,
                xg_ref, ce_ref, ya_ref):
    e = pl.program_id(1)
    f = pl.program_id(2)
    nf = pl.num_programs(2)
    tm = v_ref.shape[0]
    CH, SUB, WIN = MOE_CHUNK, MOE_SUB, MOE_WIN
    nsub = tm // SUB

    @pl.when((e == 0) & (f == 0))
    def _():
        o_ref[...] = h_ref[...]

    ce_row = cwt_ref[pl.ds(e, 1), :]
    routed = ce_row != 0.0
    rank_row = rkt_ref[pl.ds(e, 1), :]
    slot_row = jnp.where(routed, rank_row, -1.0)
    count = jnp.sum(jnp.where(routed, 1, 0).astype(I32))
    n_chunks = (count + (CH - 1)) // CH
    bounds = [jnp.int32(0)]
    for s in range(1, nsub):
        bounds.append(rank_row[0, s * SUB].astype(I32))
    bounds.append(count)
    wiota = lax.broadcasted_iota(I32, (WIN, SUB), 0).astype(F32)

    def for_windows(fn):
        def window(s, r0):
            slots = slot_row[:, s * SUB:(s + 1) * SUB]
            pick = jnp.where(slots == wiota + r0.astype(F32), 1.0, 0.0).astype(BF16)
            fn(s, r0, pick)

        starts = [pl.multiple_of((bounds[s] // 16) * 16, 16) for s in range(nsub)]
        for s in range(nsub):
            window(s, starts[s])
        for s in range(nsub):
            nw = (bounds[s + 1] - starts[s] + (WIN - 1)) // WIN

            def body(m, carry, s=s):
                window(s, pl.multiple_of(starts[s] + m * WIN, 16))
                return carry

            lax.fori_loop(1, nw, body, 0)

    @pl.when(f == 0)
    def _():
        def clear(i, carry):
            rows = pl.ds(pl.multiple_of(i * CH, 16), CH)
            xg_ref[rows, :] = jnp.zeros((CH, xg_ref.shape[1]), BF16)
            ce_ref[rows, :] = jnp.zeros((CH, LANES), F32)
            ya_ref[rows, :] = jnp.zeros((CH, ya_ref.shape[1]), F32)
            return carry

        lax.fori_loop(0, jnp.minimum(n_chunks + 1, xg_ref.shape[0] // CH), clear, 0)
        lane = lax.broadcasted_iota(I32, (WIN, LANES), 1)

        def gather(s, r0, pick):
            toks = slice(s * SUB, (s + 1) * SUB)
            rows = pl.ds(r0, WIN)
            got = _dot(pick, v_ref[toks, :])
            xg_ref[rows, :] = (xg_ref[rows, :].astype(F32) + got).astype(BF16)
            c_hi, c_lo = _split_bf16(cw_ref[toks, :])
            cg = _dot(pick, c_hi) + _dot(pick, c_lo)
            ce = jnp.sum(jnp.where(lane == e, cg, 0.0), axis=-1, keepdims=True)
            ce_ref[rows, :] += jnp.broadcast_to(ce, (WIN, LANES))

        for_windows(gather)

    def expert(i, carry):
        rows = pl.ds(pl.multiple_of(i * CH, 16), CH)
        xg = xg_ref[rows, :]
        gt = _dot(xg, wg_ref[0])
        up = _dot(xg, wu_ref[0])
        act = (gt * _sigmoid(gt) * up * ce_ref[rows, 0:1]).astype(BF16)
        ya_ref[rows, :] += _dot(act, wd_ref[0])
        return carry

    lax.fori_loop(0, n_chunks, expert, 0)

    @pl.when(f == nf - 1)
    def _():
        def scatter(s, r0, pick):
            ye = ya_ref[pl.ds(r0, WIN), :].astype(BF16)
            o_ref[s * SUB:(s + 1) * SUB, :] += _dot_tn(pick, ye)

        for_windows(scatter)


def _moe(v, h, cw, cwt, rkt, wg, wu, wd, tm=MOE_TILE, fsplit=2):
    T, D = h.shape
    E, _, F = wg.shape
    tf = F // fsplit
    rows = (-(-tm // MOE_CHUNK)) * MOE_CHUNK
    assert rows >= tm + MOE_WIN and tm % MOE_SUB == 0
    return pl.pallas_call(
        _moe_kernel,
        out_shape=jax.ShapeDtypeStruct((T, D), F32),
        grid=(T // tm, E, fsplit),
        in_specs=[
            pl.BlockSpec((tm, D), lambda i, e, f: (i, 0), pipeline_mode=pl.Buffered(1)),
            pl.BlockSpec((tm, D), lambda i, e, f: (i, 0), pipeline_mode=pl.Buffered(1)),
            pl.BlockSpec((tm, LANES), lambda i, e, f: (i, 0), pipeline_mode=pl.Buffered(1)),
            pl.BlockSpec((N_EXPERTS, tm), lambda i, e, f: (0, i)),
            pl.BlockSpec((N_EXPERTS, tm), lambda i, e, f: (0, i)),
            pl.BlockSpec((1, D, tf), lambda i, e, f: (e, 0, f)),
            pl.BlockSpec((1, D, tf), lambda i, e, f: (e, 0, f)),
            pl.BlockSpec((1, tf, D), lambda i, e, f: (e, f, 0)),
        ],
        out_specs=pl.BlockSpec((tm, D), lambda i, e, f: (i, 0)),
        scratch_shapes=[
            pltpu.VMEM((rows, D), BF16),
            pltpu.VMEM((rows, LANES), F32),
            pltpu.VMEM((rows, D), F32),
        ],
        compiler_params=_cparams(("parallel", "arbitrary", "arbitrary")),
        name="moe",
    )(v, h, cw, cwt, rkt, wg, wu, wd)


def _ple_kernel(final, h_ref, p_ref, gn_ref, wg_ref, wp_ref, fn_ref, o_ref):
    h = h_ref[...]
    ms = jnp.mean(h * h, axis=-1, keepdims=True)
    u = (h * lax.rsqrt(ms + EPS) * gn_ref[...]).astype(BF16)
    gate = _sigmoid(_dot(u, wg_ref[...]))
    hn = h + gate * _dot(p_ref[0].astype(BF16), wp_ref[...])
    if final:
        ms = jnp.mean(hn * hn, axis=-1, keepdims=True)
        hn = hn * lax.rsqrt(ms + EPS) * fn_ref[...]
    o_ref[...] = hn


def _ple(h, p, layer, gn, wg, wp, fn, final, tm=512):
    T, D = h.shape
    return pl.pallas_call(
        functools.partial(_ple_kernel, final),
        out_shape=jax.ShapeDtypeStruct((T, D), F32),
        grid=(T // tm,),
        in_specs=[
            pl.BlockSpec((tm, D), lambda i: (i, 0)),
            pl.BlockSpec((1, tm, PLE_DIM), lambda i: (layer, i, 0)),
            pl.BlockSpec((1, D), lambda i: (0, 0)),
            pl.BlockSpec((D, D), lambda i: (0, 0)),
            pl.BlockSpec((PLE_DIM, D), lambda i: (0, 0)),
            pl.BlockSpec((1, D), lambda i: (0, 0)),
        ],
        out_specs=pl.BlockSpec((tm, D), lambda i: (i, 0)),
        compiler_params=_cparams(("parallel",)),
        name="ple",
    )(h, p, gn, wg, wp, fn)


def _reorder_kernel(w_ref, o_ref, ogz_ref):
    o_sq = 4 * HG_WIDTH
    o_sc = o_sq + SA_WIDTH
    o_iq = o_sc + SA_LATENT
    o_ik = o_iq + IDX_HEADS * IDX_DIM
    o_bq = o_ik + IDX_DIM + IDX_HEADS
    o_gz = o_bq + 3 * SB_WIDTH

    def put(dst, lo, hi):
        o_ref[:, dst:dst + (hi - lo)] = w_ref[0, :, lo:hi].astype(BF16)

    ogz_ref[...] = w_ref[0, :, o_gz:o_gz + GZW].astype(BF16)
    put(HG_OFF, 0, o_sq)
    put(SQ_OFF, o_sq, o_sc)
    put(IQ_OFF, o_iq, o_ik)
    put(SC_OFF, o_sc, o_iq)
    put(IKW_OFF, o_ik, o_bq)
    used = IDX_DIM + IDX_HEADS
    o_ref[:, IKW_OFF + used:SB_OFF] = jnp.zeros((o_ref.shape[0], LANES - used), BF16)
    put(SB_OFF, o_bq, o_gz)


def _reorder_w_in(w_in, layer, tr=256):
    _, R, C = w_in.shape
    return pl.pallas_call(
        _reorder_kernel,
        out_shape=(jax.ShapeDtypeStruct((R, ZW), BF16), jax.ShapeDtypeStruct((R, GZW), BF16)),
        grid=(R // tr,),
        in_specs=[pl.BlockSpec((1, tr, C), lambda i: (layer, i, 0))],
        out_specs=(pl.BlockSpec((tr, ZW), lambda i: (i, 0)), pl.BlockSpec((tr, GZW), lambda i: (i, 0))),
        compiler_params=_cparams(("parallel",)),
        name="reorder_w_in",
    )(w_in)


def kernel(x, p, lb_param, norm_mix, w_in, hg_onorm, sa_cnorm, sa_wk, sa_wv, w_branch, w_out,
           norm_ffn, dense_wg, dense_wu, dense_wd, moe_router, moe_wg, moe_wu, moe_wd, norm_ple,
           ple_gate, ple_proj, norm_final):
    B, S, D = x.shape
    T = B * S
    depth = w_in.shape[0]
    lb_sm = jax.nn.softmax(lb_param.astype(F32), axis=0)
    lb_all = jnp.cumsum(lb_sm, axis=0) - lb_sm[0:1]
    h = x.reshape(T, D)
    for i in range(depth):
        w_mix, w_gz = _reorder_w_in(w_in, i)
        z, u = _inproj(h, norm_mix[i][None, :], w_mix)
        a = _hgrn(z, lb_all[i][None, :], hg_onorm[i][None, :], B, S)
        b = _dsa(z, sa_cnorm[i][None, :], sa_wk[i].astype(BF16), sa_wv[i].T.astype(BF16), B, S)
        c = _sb(z, B, S)
        h, v = _merge(a, b, c, u, h, w_gz, w_branch[i].astype(BF16), w_out[i].astype(BF16),
                      norm_ffn[i][None, :])
        jj = i // 2
        if i % 2 == 0:
            h = _swiglu(v, h, dense_wg[jj].astype(BF16), dense_wu[jj].astype(BF16),
                        dense_wd[jj].astype(BF16))
        else:
            router = jnp.pad(moe_router[jj], ((0, 0), (0, LANES - N_EXPERTS)))
            cw, cwt, rkt = _router(h, norm_ffn[i][None, :], router)
            h = _moe(v, h, cw, cwt, rkt, moe_wg[jj].astype(BF16), moe_wu[jj].astype(BF16),
                     moe_wd[jj].astype(BF16))
        h = _ple(h, p.reshape(depth, T, PLE_DIM), i, norm_ple[i][None, :],
                 ple_gate[i].astype(BF16), ple_proj[i].astype(BF16), norm_final[None, :],
                 i == depth - 1)
    return h.reshape(B, S, D)
```

```python
import functools

import numpy as np
import jax
import jax.numpy as jnp
from jax import lax
from jax.experimental import pallas as pl
from jax.experimental.pallas import tpu as pltpu

F32 = jnp.float32
BF16 = jnp.bfloat16
I32 = jnp.int32

D_MODEL = 1024
CHUNK = 64
QBLOCK = 128
HG_HEADS = 4
HG_DK = 128
HG_WIDTH = HG_HEADS * HG_DK
SA_HEADS = 8
SA_DH = 64
SA_WIDTH = SA_HEADS * SA_DH
SA_LATENT = 128
IDX_HEADS = 4
IDX_DIM = 64
TOPK_MAX = 256
SB_HEADS = 8
SB_DH = 64
SB_WIDTH = SB_HEADS * SB_DH
N_BRANCH = 3
BRANCH_WIDTH = 512
N_EXPERTS = 8
PLE_DIM = 256
EPS = 1e-6

LANES = 128

HG_OFF = 0
SQ_OFF = HG_OFF + 4 * HG_WIDTH
IQ_OFF = SQ_OFF + SA_WIDTH
SC_OFF = IQ_OFF + IDX_HEADS * IDX_DIM
IKW_OFF = SC_OFF + SA_LATENT
SB_OFF = IKW_OFF + LANES
ZW = SB_OFF + 3 * SB_WIDTH
GZW = N_BRANCH * D_MODEL

VMEM_LIMIT = 56 * 1024 * 1024

NEG_BIG = -1e30
INT_MIN = -2147483648
LOG2E = 1.4426950408889634
SB_UNDERFLOW = -104.0


def _cparams(sem):
    return pltpu.CompilerParams(dimension_semantics=sem, vmem_limit_bytes=VMEM_LIMIT)


def _sigmoid(x):
    e = jnp.exp(-jnp.abs(x))
    inv = 1.0 / (1.0 + e)
    return jnp.where(x >= 0, inv, e * inv)


def _dot(a, b):
    return jnp.dot(a, b, preferred_element_type=F32)


def _dot_nt(a, b):
    return lax.dot_general(a, b, (((1,), (1,)), ((), ())), preferred_element_type=F32)


def _dot_tn(a, b):
    return lax.dot_general(a, b, (((0,), (0,)), ((), ())), preferred_element_type=F32)


def _split_bf16(x):
    hi = x.astype(BF16)
    lo = (x - hi.astype(F32)).astype(BF16)
    return hi, lo


def _inproj_kernel(x_ref, g_ref, w_ref, o_ref, u_ref):
    @pl.when(pl.program_id(1) == 0)
    def _():
        x = x_ref[...]
        ms = jnp.mean(x * x, axis=-1, keepdims=True)
        u_ref[...] = (x * lax.rsqrt(ms + EPS) * g_ref[...]).astype(BF16)

    o_ref[...] = _dot(u_ref[...], w_ref[...])


def _inproj(h, gain, w, tm=1024, tn=1536):
    T, D = h.shape
    N = w.shape[1]
    return pl.pallas_call(
        _inproj_kernel,
        out_shape=(jax.ShapeDtypeStruct((T, N), F32), jax.ShapeDtypeStruct((T, D), BF16)),
        grid=(T // tm, N // tn),
        in_specs=[
            pl.BlockSpec((tm, D), lambda i, j: (i, 0)),
            pl.BlockSpec((1, D), lambda i, j: (0, 0)),
            pl.BlockSpec((D, tn), lambda i, j: (0, j)),
        ],
        out_specs=(pl.BlockSpec((tm, tn), lambda i, j: (i, j)),
                   pl.BlockSpec((tm, D), lambda i, j: (i, 0))),
        compiler_params=_cparams(("parallel", "arbitrary")),
        name="inproj",
    )(h, gain, w)


_HG_LEVELS = (64, 32, 16, 8, 4, 2)


def _hgrn_consts():
    C = CHUNK
    blocks, masks = [], []
    for n in _HG_LEVELS:
        half = n // 2
        L = np.zeros((C, C), np.float32)
        M = np.zeros((C, C), np.float32)
        for t in range(C):
            base = (t // n) * n
            mid = base + half
            if t >= mid:
                L[t, mid:t + 1] = 1.0
                M[t, base:mid] = 1.0
            else:
                L[t, t + 1:mid] = 1.0
        blocks.append(L)
        masks.append(M)
    masks.append(np.eye(C, dtype=np.float32))
    blocks.append(np.tril(np.ones((C, C), np.float32)))
    blocks.append(np.triu(np.ones((C, C), np.float32), 1))
    return np.concatenate(blocks, 0), np.stack(masks)


def _hgrn_kernel(q_ref, f_ref, i_ref, g_ref, lb_ref, on_ref, lc_ref, mc_ref, o_ref, st_ref,
                 qq_ref, kk_ref, ex_ref, sc_ref):
    C = CHUNK
    nlev = len(_HG_LEVELS)

    @pl.when(pl.program_id(1) == 0)
    def _():
        st_ref[...] = jnp.zeros(st_ref.shape, F32)

    lc = lc_ref[...]
    onorm = on_ref[...]
    n_chunks = q_ref.shape[0] // C

    def chunk(c, carry):
        r0 = pl.multiple_of(c * C, C)
        rows = pl.ds(r0, C)
        fpre = f_ref[rows, :]
        lb = lb_ref[...]
        e = jnp.exp(-jnp.abs(fpre))
        inv = 1.0 / (1.0 + e)
        sg_pos = jnp.where(fpre >= 0, inv, e * inv)
        sg_neg = jnp.where(fpre >= 0, e * inv, inv)
        lf = jnp.log(lb + (1.0 - lb) * sg_pos)
        kk_ref[...] = (1.0 - lb) * sg_neg
        qin = q_ref[rows, :]
        qq_ref[...] = qin * _sigmoid(qin)
        lf_hi, lf_lo = _split_bf16(lf)
        ex_ref[...] = jnp.exp(_dot(lc, lf_hi) + _dot(lc, lf_lo))
        for h in range(HG_HEADS):
            cs = slice(h * HG_DK, (h + 1) * HG_DK)
            q = qq_ref[:, cs]
            k = kk_ref[:, cs]
            scores = _dot_nt(q.astype(BF16), k.astype(BF16)) * mc_ref[nlev]
            for l in range(nlev):
                el = ex_ref[l * C:(l + 1) * C, cs]
                s_l = _dot_nt((q * el).astype(BF16), (k * el).astype(BF16))
                scores = scores + s_l * mc_ref[l]
            sc_ref[h] = scores.astype(BF16)
        for h in range(HG_HEADS):
            cs = slice(h * HG_DK, (h + 1) * HG_DK)
            eb = ex_ref[nlev * C:(nlev + 1) * C, cs]
            er = ex_ref[(nlev + 1) * C:(nlev + 2) * C, cs]
            st = st_ref[h]
            vb = i_ref[rows, cs].astype(BF16)
            o = _dot(sc_ref[h], vb) + _dot_nt((qq_ref[:, cs] * eb).astype(BF16), st.astype(BF16))
            st_ref[h] = st * eb[C - 1:C, :] + _dot_tn(vb, (kk_ref[:, cs] * er).astype(BF16))
            ms = jnp.mean(o * o, axis=-1, keepdims=True)
            on = o * lax.rsqrt(ms + EPS) * onorm
            g = g_ref[rows, cs]
            o_ref[rows, cs] = on * (g * _sigmoid(g))
        return carry

    lax.fori_loop(0, n_chunks, chunk, 0)


def _hgrn(z, lb, onorm, B, S, sblk=1024):
    T = B * S
    sblk = min(sblk, S)
    ns = S // sblk
    lc_np, mc_np = _hgrn_consts()
    lc = jnp.asarray(lc_np, BF16)
    mc = jnp.asarray(mc_np, F32)
    cb = HG_OFF // HG_WIDTH

    def zspec(k):
        return pl.BlockSpec((sblk, HG_WIDTH), lambda b, s, k=k: (b * ns + s, cb + k))

    return pl.pallas_call(
        _hgrn_kernel,
        out_shape=jax.ShapeDtypeStruct((T, HG_WIDTH), F32),
        grid=(B, ns),
        in_specs=[
            zspec(0), zspec(1), zspec(2), zspec(3),
            pl.BlockSpec((1, HG_WIDTH), lambda b, s: (0, 0)),
            pl.BlockSpec((1, HG_DK), lambda b, s: (0, 0)),
            pl.BlockSpec(lc.shape, lambda b, s: (0, 0)),
            pl.BlockSpec(mc.shape, lambda b, s: (0, 0, 0)),
        ],
        out_specs=pl.BlockSpec((sblk, HG_WIDTH), lambda b, s: (b * ns + s, 0)),
        scratch_shapes=[
            pltpu.VMEM((HG_HEADS, HG_DK, HG_DK), F32),
            pltpu.VMEM((CHUNK, HG_WIDTH), F32),
            pltpu.VMEM((CHUNK, HG_WIDTH), F32),
            pltpu.VMEM(((len(_HG_LEVELS) + 2) * CHUNK, HG_WIDTH), F32),
            pltpu.VMEM((HG_HEADS, CHUNK, CHUNK), BF16),
        ],
        compiler_params=_cparams(("parallel", "arbitrary")),
        name="hgrn2",
    )(z, z, z, z, lb, onorm, lc, mc)


def _dsa_kernel(topk, sq_ref, iq_ref, iwq_ref, sc_ref, ikw_ref, cn_ref, wk_ref, wvt_ref,
                o_ref, kk_ref, vt_ref, ik_ref, key_ref, plane_ref, qt_ref, m_ref, l_ref, acc_ref):
    QB = QBLOCK
    j = pl.program_id(1)
    nkb = j + 1
    S = sc_ref.shape[0]
    nb = S // QB
    idx_bits = (S - 1).bit_length()

    @pl.when(j == 0)
    def _():
        c = sc_ref[...]
        ms = jnp.mean(c * c, axis=-1, keepdims=True)
        cb = (c * lax.rsqrt(ms + EPS) * cn_ref[...]).astype(BF16)
        kfull = _dot(cb, wk_ref[...]).astype(BF16)
        vtfull = _dot_nt(wvt_ref[...], cb).astype(BF16)
        ikfull = ikw_ref[:, 0:IDX_DIM].astype(BF16)
        for kb in range(nb):
            kk_ref[kb] = kfull[kb * QB:(kb + 1) * QB]
            vt_ref[kb] = vtfull[:, kb * QB:(kb + 1) * QB]
            ik_ref[kb] = ikfull[kb * QB:(kb + 1) * QB]

    qt_ref[...] = (sq_ref[...] * (SA_DH ** -0.5 * LOG2E)).T.astype(BF16)
    iqt = iq_ref[...].T.astype(BF16)
    iwt = iwq_ref[...].T * (IDX_HEADS ** -0.5)

    row = lax.broadcasted_iota(I32, (QB, QB), 0)
    lane = lax.broadcasted_iota(I32, (QB, QB), 1)
    diag_bad = (row >= CHUNK) & (lane < CHUNK)

    n_pairs = (nkb + 1) // 2

    def score_block(kb):
        ikb = ik_ref[kb]
        acc = jnp.zeros((QB, QB), F32)
        for h in range(IDX_HEADS):
            s_h = _dot(ikb, iqt[h * IDX_DIM:(h + 1) * IDX_DIM])
            acc = acc + jnp.maximum(s_h, 0.0) * iwt[IDX_DIM + h:IDX_DIM + h + 1, :]
        acc = acc + 0.0
        acc = jnp.where((kb == j) & diag_bad, -jnp.inf, acc)
        bits = pltpu.bitcast(acc, I32)
        key_ref[kb] = jnp.where(bits < 0, bits ^ jnp.int32(0x7FFFFFFF), bits)

    def score_pair(g, carry):
        score_block(2 * g)
        score_block(2 * g + 1)
        return carry

    lax.fori_loop(0, n_pairs, score_pair, 0)

    @pl.when(j + 1 < 2 * n_pairs)
    def _():
        key_ref[j + 1] = jnp.full((QB, QB), INT_MIN, I32)

    def build_planes(g, carry):
        a = []
        for half in range(2):
            blk = key_ref[2 * g + half] ^ jnp.int32(INT_MIN)
            a.extend(blk[r * 8:(r + 1) * 8, :] for r in range(QB // 8))
        for sh, m in ((16, 0x0000FFFF), (8, 0x00FF00FF), (4, 0x0F0F0F0F), (2, 0x33333333),
                      (1, 0x55555555)):
            for k in range(32):
                if k & sh == 0:
                    t = (lax.shift_right_logical(a[k], jnp.int32(sh)) ^ a[k + sh]) & jnp.int32(m)
                    a[k + sh] = a[k + sh] ^ t
                    a[k] = a[k] ^ lax.shift_left(t, jnp.int32(sh))
        rows = pl.ds(pl.multiple_of(g * 8, 8), 8)
        for b in range(32):
            plane_ref[b, rows, :] = a[b]
        return carry

    def search(_):
        lax.fori_loop(0, n_pairs, build_planes, 0)
        wrow = lax.broadcasted_iota(I32, (nb // 2 * 8, QB), 0)
        cand = jnp.where((wrow >> 3) < n_pairs, jnp.int32(-1), jnp.int32(0))
        above = jnp.zeros((1, QB), I32)
        u = jnp.zeros((1, QB), I32)
        for b in range(31, -1, -1):
            ones = cand & plane_ref[b]
            cnt1 = jnp.sum(lax.population_count(ones), axis=0, keepdims=True)
            take = (above + cnt1) >= topk
            cand = jnp.where(take, ones, cand ^ ones)
            above = jnp.where(take, above, above + cnt1)
            u = u | jnp.where(take, jnp.int32(INT_MIN if b == 31 else 1 << b), jnp.int32(0))
        thr = u ^ jnp.int32(INT_MIN)
        need = topk - above
        n_tied = jnp.sum(lax.population_count(cand), axis=0, keepdims=True)

        def break_ties(_):
            base = (wrow >> 3) * (2 * QB) + (wrow & 7)

            def ibit(i, jp):
                c = jp + lax.shift_left(jnp.int32(1), idx_bits - 1 - i)
                nbits = jnp.clip((c - base + 7) >> 3, 0, 32)
                below = jnp.where(nbits >= 32, jnp.int32(-1),
                                  lax.shift_left(jnp.int32(1), nbits) - 1)
                cnt = jnp.sum(lax.population_count(cand & below), axis=0, keepdims=True)
                return jnp.where(cnt < need, c, jp)

            return lax.fori_loop(0, idx_bits, ibit, jnp.zeros((1, QB), I32))

        def keep_ties(_):
            return jnp.full((1, QB), S, I32)

        jp = lax.cond(jnp.max(n_tied - need) > 0, break_ties, keep_ties, 0)
        return thr, jp

    def take_all(_):
        return (jnp.full((1, QB), INT_MIN, I32), jnp.full((1, QB), -1, I32))

    thr, jp = lax.cond(nkb * QB > topk, search, take_all, 0)

    m_ref[...] = jnp.full(m_ref.shape, NEG_BIG, F32)
    l_ref[...] = jnp.zeros(l_ref.shape, F32)
    acc_ref[...] = jnp.zeros(acc_ref.shape, F32)

    def attend_block(kb):
        key = key_ref[kb]
        kidx = row + kb * QB
        sel = (key > thr) | ((key == thr) & (kidx <= jp))
        hidden = (kidx >= j * QB + CHUNK) & (lane < CHUNK)
        bias = jnp.where(sel & jnp.logical_not(hidden), 0.0, -jnp.inf)
        kblk = kk_ref[kb]
        vtb = vt_ref[kb]
        for h in range(SA_HEADS):
            lg = _dot(kblk, qt_ref[h * SA_DH:(h + 1) * SA_DH, :]) + bias
            m_old = m_ref[h:h + 1, :]
            m_new = jnp.maximum(m_old, jnp.max(lg, axis=0, keepdims=True))
            alpha = jnp.exp2(m_old - m_new)
            p = jnp.exp2(lg - m_new)
            l_ref[h:h + 1, :] = alpha * l_ref[h:h + 1, :] + jnp.sum(p, axis=0, keepdims=True)
            acc_ref[h] = acc_ref[h] * alpha + _dot(vtb, p.astype(BF16))
            m_ref[h:h + 1, :] = m_new

    def attend(g, carry):
        for i in range(8):
            attend_block(8 * g + i)
        return carry

    n_oct = nkb // 8
    lax.fori_loop(0, n_oct, attend, 0)

    @pl.when((nkb & 4) != 0)
    def _():
        for i in range(4):
            attend_block(8 * n_oct + i)

    n_quads = nkb // 4

    @pl.when((nkb & 2) != 0)
    def _():
        attend_block(4 * n_quads)
        attend_block(4 * n_quads + 1)

    @pl.when((nkb & 1) != 0)
    def _():
        attend_block(nkb - 1)

    outs = []
    for h in range(SA_HEADS):
        outs.append(acc_ref[h] * (1.0 / l_ref[h:h + 1, :]))
    o_ref[...] = jnp.concatenate(outs, axis=0).T


def _dsa(z, cnorm, wk, wvt, B, S):
    T = B * S
    nb = S // QBLOCK
    topk = min(TOPK_MAX, S // 4)
    assert nb % 2 == 0
    kern = functools.partial(_dsa_kernel, topk)
    return pl.pallas_call(
        kern,
        out_shape=jax.ShapeDtypeStruct((T, SA_WIDTH), F32),
        grid=(B, nb),
        in_specs=[
            pl.BlockSpec((QBLOCK, SA_WIDTH), lambda b, j: (b * nb + j, SQ_OFF // SA_WIDTH)),
            pl.BlockSpec((QBLOCK, IDX_HEADS * IDX_DIM),
                         lambda b, j: (b * nb + j, IQ_OFF // (IDX_HEADS * IDX_DIM))),
            pl.BlockSpec((QBLOCK, LANES), lambda b, j: (b * nb + j, IKW_OFF // LANES)),
            pl.BlockSpec((S, SA_LATENT), lambda b, j: (b, SC_OFF // SA_LATENT)),
            pl.BlockSpec((S, LANES), lambda b, j: (b, IKW_OFF // LANES)),
            pl.BlockSpec((1, SA_LATENT), lambda b, j: (0, 0)),
            pl.BlockSpec((SA_LATENT, SA_DH), lambda b, j: (0, 0)),
            pl.BlockSpec((SA_DH, SA_LATENT), lambda b, j: (0, 0)),
        ],
        out_specs=pl.BlockSpec((QBLOCK, SA_WIDTH), lambda b, j: (b * nb + j, 0)),
        scratch_shapes=[
            pltpu.VMEM((nb, QBLOCK, SA_DH), BF16),
            pltpu.VMEM((nb, SA_DH, QBLOCK), BF16),
            pltpu.VMEM((nb, QBLOCK, IDX_DIM), BF16),
            pltpu.VMEM((nb, QBLOCK, QBLOCK), I32),
            pltpu.VMEM((32, nb // 2 * 8, QBLOCK), I32),
            pltpu.VMEM((SA_WIDTH, QBLOCK), BF16),
            pltpu.VMEM((SA_HEADS, QBLOCK), F32),
            pltpu.VMEM((SA_HEADS, QBLOCK), F32),
            pltpu.VMEM((SA_HEADS, SA_DH, QBLOCK), F32),
        ],
        compiler_params=_cparams(("parallel", "arbitrary")),
        name="dsa",
    )(z, z, z, z, z, cnorm, wk, wvt)


def _sb_kernel(q_ref, k_ref, v_ref, uo_ref, o_ref, kt_ref, vb_ref, run_ref, acc_ref,
               ls_ref, lkh_ref, lkl_ref, tot_ref, a_ref):
    QB = QBLOCK
    j = pl.program_id(1)
    S = k_ref.shape[0]
    nb = S // QB
    npair = SB_HEADS // 2
    lane = lax.broadcasted_iota(I32, (QB, QB), 1)
    row = lax.broadcasted_iota(I32, (QB, QB), 0)

    @pl.when(j == 0)
    def _():
        for p in range(npair):
            cols = slice(p * LANES, (p + 1) * LANES)
            kt = k_ref[:, cols].T
            vv = v_ref[:, cols]
            for hh in range(2):
                lo = hh * SB_DH
                ktz = jnp.where((row >= lo) & (row < lo + SB_DH), 1.0, 0.0)
                vz = jnp.where((lane >= lo) & (lane < lo + SB_DH), 1.0, 0.0)
                for kb in range(nb):
                    ks = slice(kb * QB, (kb + 1) * QB)
                    kt_ref[kb, 2 * p + hh] = (kt[:, ks] * ktz).astype(BF16)
                    vb_ref[kb, 2 * p + hh] = (vv[ks, :] * vz).astype(BF16)

    uo = uo_ref[...]
    strict = lane < row
    qs = (q_ref[...] * (SB_DH ** -0.5)).astype(BF16)

    def all_heads(kbs, diag):
        nk = len(kbs)
        for h in range(SB_HEADS):
            p = h // 2
            for i, kb in enumerate(kbs):
                s = h * nk + i
                z = _dot(qs[:, p * LANES:(p + 1) * LANES], kt_ref[kb, h])
                l1p = jnp.log(1.0 + jnp.exp(-jnp.abs(z)))
                ls_pos = jnp.minimum(z, 0.0) - l1p
                lk = ls_pos - z
                if diag and i == 0:
                    lk = jnp.where(strict, lk, 0.0)
                lk_hi, lk_lo = _split_bf16(lk)
                ls_ref[s] = ls_pos
                lkh_ref[s] = lk_hi
                lkl_ref[s] = lk_lo
                tot_ref[s] = jnp.sum(lk, axis=1, keepdims=True)
        top = None
        for h in range(SB_HEADS):
            run = None if diag else run_ref[h]
            for i in range(nk):
                s = h * nk + i
                after = _dot(lkh_ref[s], uo) + _dot(lkl_ref[s], uo)
                if diag and i == 0:
                    a = jnp.where(strict, jnp.exp(ls_ref[s] + after), 0.0)
                    run = tot_ref[s]
                else:
                    a = jnp.exp(ls_ref[s] + after + run)
                    run = run + tot_ref[s]
                a_ref[s] = a.astype(BF16)
            run_ref[h] = run
            top = run if top is None else jnp.maximum(top, run)
        for p in range(npair):
            pv = None
            for h in (2 * p, 2 * p + 1):
                for i, kb in enumerate(kbs):
                    out = _dot(a_ref[h * nk + i], vb_ref[kb, h])
                    pv = out if pv is None else pv + out
            if diag:
                acc_ref[p] = pv
            else:
                acc_ref[p] += pv
        return jnp.max(top) > SB_UNDERFLOW

    n_first = 3
    alive = lax.cond(
        j >= 2, lambda: all_heads([j, j - 1, j - 2], True),
        lambda: lax.cond(j == 1, lambda: all_heads([j, j - 1], True), lambda: all_heads([j], True)))

    def cond(c):
        return (c[0] >= 1) & c[1]

    def body(c):
        return c[0] - 2, all_heads([c[0], c[0] - 1], False)

    kb_left, alive = lax.while_loop(cond, body, (j - n_first, alive))

    @pl.when((kb_left == 0) & alive)
    def _():
        all_heads([0], False)

    o_ref[...] = jnp.concatenate([acc_ref[p] for p in range(npair)], axis=1)


def _sb(z, B, S):
    T = B * S
    nb = S // QBLOCK
    u = np.tril(np.ones((QBLOCK, QBLOCK), np.float32), -1)
    uo = jnp.asarray(u, BF16)
    qoff = SB_OFF // SB_WIDTH
    return pl.pallas_call(
        _sb_kernel,
        out_shape=jax.ShapeDtypeStruct((T, SB_WIDTH), F32),
        grid=(B, nb),
        in_specs=[
            pl.BlockSpec((QBLOCK, SB_WIDTH), lambda b, j: (b * nb + j, qoff)),
            pl.BlockSpec((S, SB_WIDTH), lambda b, j: (b, qoff + 1)),
            pl.BlockSpec((S, SB_WIDTH), lambda b, j: (b, qoff + 2)),
            pl.BlockSpec(uo.shape, lambda b, j: (0, 0)),
        ],
        out_specs=pl.BlockSpec((QBLOCK, SB_WIDTH), lambda b, j: (b * nb + j, 0)),
        scratch_shapes=[
            pltpu.VMEM((nb, SB_HEADS, LANES, QBLOCK), BF16),
            pltpu.VMEM((nb, SB_HEADS, QBLOCK, LANES), BF16),
            pltpu.VMEM((SB_HEADS, QBLOCK, 1), F32),
            pltpu.VMEM((SB_HEADS // 2, QBLOCK, LANES), F32),
            pltpu.VMEM((3 * SB_HEADS, QBLOCK, QBLOCK), F32),
            pltpu.VMEM((3 * SB_HEADS, QBLOCK, QBLOCK), BF16),
            pltpu.VMEM((3 * SB_HEADS, QBLOCK, QBLOCK), BF16),
            pltpu.VMEM((3 * SB_HEADS, QBLOCK, 1), F32),
            pltpu.VMEM((3 * SB_HEADS, QBLOCK, QBLOCK), BF16),
        ],
        compiler_params=_cparams(("parallel", "arbitrary")),
        name="stickbreak",
    )(z, z, z, uo)


def _merge_kernel(a_ref, b_ref, c_ref, u_ref, h_ref, wgz_ref, wb_ref, wo_ref, gn_ref, ho_ref, v_ref):
    u = u_ref[...]
    merged = None
    for n, br in enumerate((a_ref, b_ref, c_ref)):
        proj = _dot(br[...].astype(BF16), wb_ref[n])
        gate = _sigmoid(_dot(u, wgz_ref[:, n * D_MODEL:(n + 1) * D_MODEL]))
        merged = gate * proj if merged is None else merged + gate * proj
    hn = h_ref[...] + _dot(merged.astype(BF16), wo_ref[...])
    ho_ref[...] = hn
    ms = jnp.mean(hn * hn, axis=-1, keepdims=True)
    v_ref[...] = (hn * lax.rsqrt(ms + EPS) * gn_ref[...]).astype(BF16)


def _merge(a, b, c, u, h, wgz, wb, wo, gn, tm=256):
    T, D = h.shape
    bw = BRANCH_WIDTH
    return pl.pallas_call(
        _merge_kernel,
        out_shape=(jax.ShapeDtypeStruct((T, D), F32), jax.ShapeDtypeStruct((T, D), BF16)),
        grid=(T // tm,),
        in_specs=[
            pl.BlockSpec((tm, bw), lambda i: (i, 0)),
            pl.BlockSpec((tm, bw), lambda i: (i, 0)),
            pl.BlockSpec((tm, bw), lambda i: (i, 0)),
            pl.BlockSpec((tm, D), lambda i: (i, 0)),
            pl.BlockSpec((tm, D), lambda i: (i, 0)),
            pl.BlockSpec((D, GZW), lambda i: (0, 0)),
            pl.BlockSpec((N_BRANCH, bw, D), lambda i: (0, 0, 0)),
            pl.BlockSpec((D, D), lambda i: (0, 0)),
            pl.BlockSpec((1, D), lambda i: (0, 0)),
        ],
        out_specs=(pl.BlockSpec((tm, D), lambda i: (i, 0)), pl.BlockSpec((tm, D), lambda i: (i, 0))),
        compiler_params=_cparams(("parallel",)),
        name="merge",
    )(a, b, c, u, h, wgz, wb, wo, gn)


def _swiglu_kernel(v_ref, h_ref, wg_ref, wu_ref, wd_ref, o_ref, acc_ref):
    f = pl.program_id(1)

    @pl.when(f == 0)
    def _():
        acc_ref[...] = h_ref[...]

    v = v_ref[...]
    gt = _dot(v, wg_ref[...])
    up = _dot(v, wu_ref[...])
    act = (gt * _sigmoid(gt) * up).astype(BF16)
    acc_ref[...] += _dot(act, wd_ref[...])

    @pl.when(f == pl.num_programs(1) - 1)
    def _():
        o_ref[...] = acc_ref[...]


def _swiglu(v, h, wg, wu, wd, tm=512, tf=1408):
    T, D = h.shape
    F = wg.shape[1]
    return pl.pallas_call(
        _swiglu_kernel,
        out_shape=jax.ShapeDtypeStruct((T, D), F32),
        grid=(T // tm, F // tf),
        in_specs=[
            pl.BlockSpec((tm, D), lambda i, f: (i, 0)),
            pl.BlockSpec((tm, D), lambda i, f: (i, 0)),
            pl.BlockSpec((D, tf), lambda i, f: (0, f)),
            pl.BlockSpec((D, tf), lambda i, f: (0, f)),
            pl.BlockSpec((tf, D), lambda i, f: (f, 0)),
        ],
        out_specs=pl.BlockSpec((tm, D), lambda i, f: (i, 0)),
        scratch_shapes=[pltpu.VMEM((tm, D), F32)],
        compiler_params=_cparams(("parallel", "arbitrary")),
        name="swiglu",
    )(v, h, wg, wu, wd)


MOE_TILE = 1024
MOE_CHUNK = 288


def _router_kernel(h_ref, gn_ref, r_ref, us_ref, cw_ref, cwt_ref, rkt_ref):
    hh = h_ref[...]
    ms = jnp.mean(hh * hh, axis=-1, keepdims=True)
    vf = hh * lax.rsqrt(ms + EPS) * gn_ref[...]
    v_hi, v_lo = _split_bf16(vf)
    r_hi, r_lo = _split_bf16(r_ref[...])
    logits = _dot(v_hi, r_hi) + _dot(v_hi, r_lo) + _dot(v_lo, r_hi)
    lane = lax.broadcasted_iota(I32, logits.shape, 1)
    logits = jnp.where(lane < N_EXPERTS, logits, -jnp.inf)
    m1 = jnp.max(logits, axis=-1, keepdims=True)
    i1 = jnp.min(jnp.where(logits == m1, lane, LANES), axis=-1, keepdims=True)
    rest = jnp.where(lane == i1, -jnp.inf, logits)
    m2 = jnp.max(rest, axis=-1, keepdims=True)
    i2 = jnp.min(jnp.where(rest == m2, lane, LANES), axis=-1, keepdims=True)
    e2 = jnp.exp(m2 - m1)
    w1 = 1.0 / (1.0 + e2)
    cw = jnp.where(lane == i1, w1, 0.0) + jnp.where(lane == i2, e2 * w1, 0.0)
    cw_ref[...] = cw
    cwt = cw.T
    cwt_ref[...] = cwt[0:N_EXPERTS]
    sel = jnp.where(cwt != 0.0, 1.0, 0.0).astype(BF16)
    rkt_ref[...] = _dot(sel, us_ref[...])[0:N_EXPERTS]


def _router(h, gn, router, tm=MOE_TILE):
    T, D = h.shape
    us = jnp.asarray(np.triu(np.ones((tm, tm), np.float32), 1), BF16)
    return pl.pallas_call(
        _router_kernel,
        out_shape=(jax.ShapeDtypeStruct((T, LANES), F32),
                   jax.ShapeDtypeStruct((N_EXPERTS, T), F32),
                   jax.ShapeDtypeStruct((N_EXPERTS, T), F32)),
        grid=(T // tm,),
        in_specs=[
            pl.BlockSpec((tm, D), lambda i: (i, 0)),
            pl.BlockSpec((1, D), lambda i: (0, 0)),
            pl.BlockSpec((D, LANES), lambda i: (0, 0)),
            pl.BlockSpec((tm, tm), lambda i: (0, 0)),
        ],
        out_specs=(pl.BlockSpec((tm, LANES), lambda i: (i, 0)),
                   pl.BlockSpec((N_EXPERTS, tm), lambda i: (0, i)),
                   pl.BlockSpec((N_EXPERTS, tm), lambda i: (0, i))),
        compiler_params=_cparams(("parallel",)),
        name="router",
    )(h, gn, router, us)


MOE_SUB = 256
MOE_WIN = 112


def _moe_kernel(v_ref, h_ref, cw_ref, cwt_ref, rkt_ref, wg_ref, wu_ref, wd_ref, o_ref,
                xg_ref, ce_ref, ya_ref):
    e = pl.program_id(1)
    f = pl.program_id(2)
    nf = pl.num_programs(2)
    tm = v_ref.shape[0]
    CH, SUB, WIN = MOE_CHUNK, MOE_SUB, MOE_WIN
    nsub = tm // SUB

    @pl.when((e == 0) & (f == 0))
    def _():
        o_ref[...] = h_ref[...]

    ce_row = cwt_ref[pl.ds(e, 1), :]
    routed = ce_row != 0.0
    rank_row = rkt_ref[pl.ds(e, 1), :]
    slot_row = jnp.where(routed, rank_row, -1.0)
    count = jnp.sum(jnp.where(routed, 1, 0).astype(I32))
    n_chunks = (count + (CH - 1)) // CH
    bounds = [jnp.int32(0)]
    for s in range(1, nsub):
        bounds.append(rank_row[0, s * SUB].astype(I32))
    bounds.append(count)
    wiota = lax.broadcasted_iota(I32, (WIN, SUB), 0).astype(F32)

    def for_windows(fn):
        def window(s, r0):
            slots = slot_row[:, s * SUB:(s + 1) * SUB]
            pick = jnp.where(slots == wiota + r0.astype(F32), 1.0, 0.0).astype(BF16)
            fn(s, r0, pick)

        starts = [pl.multiple_of((bounds[s] // 16) * 16, 16) for s in range(nsub)]
        for s in range(nsub):
            window(s, starts[s])
        for s in range(nsub):
            nw = (bounds[s + 1] - starts[s] + (WIN - 1)) // WIN

            def body(m, carry, s=s):
                window(s, pl.multiple_of(starts[s] + m * WIN, 16))
                return carry

            lax.fori_loop(1, nw, body, 0)

    @pl.when(f == 0)
    def _():
        def clear(i, carry):
            rows = pl.ds(pl.multiple_of(i * CH, 16), CH)
            xg_ref[rows, :] = jnp.zeros((CH, xg_ref.shape[1]), BF16)
            ce_ref[rows, :] = jnp.zeros((CH, LANES), F32)
            ya_ref[rows, :] = jnp.zeros((CH, ya_ref.shape[1]), F32)
            return carry

        lax.fori_loop(0, jnp.minimum(n_chunks + 1, xg_ref.shape[0] // CH), clear, 0)
        lane = lax.broadcasted_iota(I32, (WIN, LANES), 1)

        def gather(s, r0, pick):
            toks = slice(s * SUB, (s + 1) * SUB)
            rows = pl.ds(r0, WIN)
            got = _dot(pick, v_ref[toks, :])
            xg_ref[rows, :] = (xg_ref[rows, :].astype(F32) + got).astype(BF16)
            c_hi, c_lo = _split_bf16(cw_ref[toks, :])
            cg = _dot(pick, c_hi) + _dot(pick, c_lo)
            ce = jnp.sum(jnp.where(lane == e, cg, 0.0), axis=-1, keepdims=True)
            ce_ref[rows, :] += jnp.broadcast_to(ce, (WIN, LANES))

        for_windows(gather)

    def expert(i, carry):
        rows = pl.ds(pl.multiple_of(i * CH, 16), CH)
        xg = xg_ref[rows, :]
        gt = _dot(xg, wg_ref[0])
        up = _dot(xg, wu_ref[0])
        act = (gt * _sigmoid(gt) * up * ce_ref[rows, 0:1]).astype(BF16)
        ya_ref[rows, :] += _dot(act, wd_ref[0])
        return carry

    lax.fori_loop(0, n_chunks, expert, 0)

    @pl.when(f == nf - 1)
    def _():
        def scatter(s, r0, pick):
            ye = ya_ref[pl.ds(r0, WIN), :].astype(BF16)
            o_ref[s * SUB:(s + 1) * SUB, :] += _dot_tn(pick, ye)

        for_windows(scatter)


def _moe(v, h, cw, cwt, rkt, wg, wu, wd, tm=MOE_TILE, fsplit=2):
    T, D = h.shape
    E, _, F = wg.shape
    tf = F // fsplit
    rows = (-(-tm // MOE_CHUNK)) * MOE_CHUNK
    assert rows >= tm + MOE_WIN and tm % MOE_SUB == 0
    return pl.pallas_call(
        _moe_kernel,
        out_shape=jax.ShapeDtypeStruct((T, D), F32),
        grid=(T // tm, E, fsplit),
        in_specs=[
            pl.BlockSpec((tm, D), lambda i, e, f: (i, 0), pipeline_mode=pl.Buffered(1)),
            pl.BlockSpec((tm, D), lambda i, e, f: (i, 0), pipeline_mode=pl.Buffered(1)),
            pl.BlockSpec((tm, LANES), lambda i, e, f: (i, 0), pipeline_mode=pl.Buffered(1)),
            pl.BlockSpec((N_EXPERTS, tm), lambda i, e, f: (0, i)),
            pl.BlockSpec((N_EXPERTS, tm), lambda i, e, f: (0, i)),
            pl.BlockSpec((1, D, tf), lambda i, e, f: (e, 0, f)),
            pl.BlockSpec((1, D, tf), lambda i, e, f: (e, 0, f)),
            pl.BlockSpec((1, tf, D), lambda i, e, f: (e, f, 0)),
        ],
        out_specs=pl.BlockSpec((tm, D), lambda i, e, f: (i, 0)),
        scratch_shapes=[
            pltpu.VMEM((rows, D), BF16),
            pltpu.VMEM((rows, LANES), F32),
            pltpu.VMEM((rows, D), F32),
        ],
        compiler_params=_cparams(("parallel", "arbitrary", "arbitrary")),
        name="moe",
    )(v, h, cw, cwt, rkt, wg, wu, wd)


def _ple_kernel(final, h_ref, p_ref, gn_ref, wg_ref, wp_ref, fn_ref, o_ref):
    h = h_ref[...]
    ms = jnp.mean(h * h, axis=-1, keepdims=True)
    u = (h * lax.rsqrt(ms + EPS) * gn_ref[...]).astype(BF16)
    gate = _sigmoid(_dot(u, wg_ref[...]))
    hn = h + gate * _dot(p_ref[0].astype(BF16), wp_ref[...])
    if final:
        ms = jnp.mean(hn * hn, axis=-1, keepdims=True)
        hn = hn * lax.rsqrt(ms + EPS) * fn_ref[...]
    o_ref[...] = hn


def _ple(h, p, layer, gn, wg, wp, fn, final, tm=512):
    T, D = h.shape
    return pl.pallas_call(
        functools.partial(_ple_kernel, final),
        out_shape=jax.ShapeDtypeStruct((T, D), F32),
        grid=(T // tm,),
        in_specs=[
            pl.BlockSpec((tm, D), lambda i: (i, 0)),
            pl.BlockSpec((1, tm, PLE_DIM), lambda i: (layer, i, 0)),
            pl.BlockSpec((1, D), lambda i: (0, 0)),
            pl.BlockSpec((D, D), lambda i: (0, 0)),
            pl.BlockSpec((PLE_DIM, D), lambda i: (0, 0)),
            pl.BlockSpec((1, D), lambda i: (0, 0)),
        ],
        out_specs=pl.BlockSpec((tm, D), lambda i: (i, 0)),
        compiler_params=_cparams(("parallel",)),
        name="ple",
    )(h, p, gn, wg, wp, fn)


def _reorder_kernel(w_ref, o_ref, ogz_ref):
    o_sq = 4 * HG_WIDTH
    o_sc = o_sq + SA_WIDTH
    o_iq = o_sc + SA_LATENT
    o_ik = o_iq + IDX_HEADS * IDX_DIM
    o_bq = o_ik + IDX_DIM + IDX_HEADS
    o_gz = o_bq + 3 * SB_WIDTH

    def put(dst, lo, hi):
        o_ref[:, dst:dst + (hi - lo)] = w_ref[0, :, lo:hi].astype(BF16)

    ogz_ref[...] = w_ref[0, :, o_gz:o_gz + GZW].astype(BF16)
    put(HG_OFF, 0, o_sq)
    put(SQ_OFF, o_sq, o_sc)
    put(IQ_OFF, o_iq, o_ik)
    put(SC_OFF, o_sc, o_iq)
    put(IKW_OFF, o_ik, o_bq)
    used = IDX_DIM + IDX_HEADS
    o_ref[:, IKW_OFF + used:SB_OFF] = jnp.zeros((o_ref.shape[0], LANES - used), BF16)
    put(SB_OFF, o_bq, o_gz)


def _reorder_w_in(w_in, layer, tr=256):
    _, R, C = w_in.shape
    return pl.pallas_call(
        _reorder_kernel,
        out_shape=(jax.ShapeDtypeStruct((R, ZW), BF16), jax.ShapeDtypeStruct((R, GZW), BF16)),
        grid=(R // tr,),
        in_specs=[pl.BlockSpec((1, tr, C), lambda i: (layer, i, 0))],
        out_specs=(pl.BlockSpec((tr, ZW), lambda i: (i, 0)), pl.BlockSpec((tr, GZW), lambda i: (i, 0))),
        compiler_params=_cparams(("parallel",)),
        name="reorder_w_in",
    )(w_in)


def kernel(x, p, lb_param, norm_mix, w_in, hg_onorm, sa_cnorm, sa_wk, sa_wv, w_branch, w_out,
           norm_ffn, dense_wg, dense_wu, dense_wd, moe_router, moe_wg, moe_wu, moe_wd, norm_ple,
           ple_gate, ple_proj, norm_final):
    B, S, D = x.shape
    T = B * S
    depth = w_in.shape[0]
    lb_sm = jax.nn.softmax(lb_param.astype(F32), axis=0)
    lb_all = jnp.cumsum(lb_sm, axis=0) - lb_sm[0:1]
    h = x.reshape(T, D)
    for i in range(depth):
        w_mix, w_gz = _reorder_w_in(w_in, i)
        z, u = _inproj(h, norm_mix[i][None, :], w_mix)
        a = _hgrn(z, lb_all[i][None, :], hg_onorm[i][None, :], B, S)
        b = _dsa(z, sa_cnorm[i][None, :], sa_wk[i].astype(BF16), sa_wv[i].T.astype(BF16), B, S)
        c = _sb(z, B, S)
        h, v = _merge(a, b, c, u, h, w_gz, w_branch[i].astype(BF16), w_out[i].astype(BF16),
                      norm_ffn[i][None, :])
        jj = i // 2
        if i % 2 == 0:
            h = _swiglu(v, h, dense_wg[jj].astype(BF16), dense_wu[jj].astype(BF16),
                        dense_wd[jj].astype(BF16))
        else:
            router = jnp.pad(moe_router[jj], ((0, 0), (0, LANES - N_EXPERTS)))
            cw, cwt, rkt = _router(h, norm_ffn[i][None, :], router)
            h = _moe(v, h, cw, cwt, rkt, moe_wg[jj].astype(BF16), moe_wu[jj].astype(BF16),
                     moe_wd[jj].astype(BF16))
        h = _ple(h, p.reshape(depth, T, PLE_DIM), i, norm_ple[i][None, :],
                 ple_gate[i].astype(BF16), ple_proj[i].astype(BF16), norm_final[None, :],
                 i == depth - 1)
    return h.reshape(B, S, D)
```

```python
import functools

import numpy as np
import jax
import jax.numpy as jnp
from jax import lax
from jax.experimental import pallas as pl
from jax.experimental.pallas import tpu as pltpu

F32 = jnp.float32
BF16 = jnp.bfloat16
I32 = jnp.int32

D_MODEL = 1024
CHUNK = 64
QBLOCK = 128
HG_HEADS = 4
HG_DK = 128
HG_WIDTH = HG_HEADS * HG_DK
SA_HEADS = 8
SA_DH = 64
SA_WIDTH = SA_HEADS * SA_DH
SA_LATENT = 128
IDX_HEADS = 4
IDX_DIM = 64
TOPK_MAX = 256
SB_HEADS = 8
SB_DH = 64
SB_WIDTH = SB_HEADS * SB_DH
N_BRANCH = 3
BRANCH_WIDTH = 512
N_EXPERTS = 8
PLE_DIM = 256
EPS = 1e-6

LANES = 128

HG_OFF = 0
SQ_OFF = HG_OFF + 4 * HG_WIDTH
IQ_OFF = SQ_OFF + SA_WIDTH
SC_OFF = IQ_OFF + IDX_HEADS * IDX_DIM
IKW_OFF = SC_OFF + SA_LATENT
SB_OFF = IKW_OFF + LANES
ZW = SB_OFF + 3 * SB_WIDTH
GZW = N_BRANCH * D_MODEL

VMEM_LIMIT = 56 * 1024 * 1024

NEG_BIG = -1e30
INT_MIN = -2147483648
LOG2E = 1.4426950408889634
SB_UNDERFLOW = -104.0


def _cparams(sem):
    return pltpu.CompilerParams(dimension_semantics=sem, vmem_limit_bytes=VMEM_LIMIT)


def _sigmoid(x):
    e = jnp.exp(-jnp.abs(x))
    inv = 1.0 / (1.0 + e)
    return jnp.where(x >= 0, inv, e * inv)


def _dot(a, b):
    return jnp.dot(a, b, preferred_element_type=F32)


def _dot_nt(a, b):
    return lax.dot_general(a, b, (((1,), (1,)), ((), ())), preferred_element_type=F32)


def _dot_tn(a, b):
    return lax.dot_general(a, b, (((0,), (0,)), ((), ())), preferred_element_type=F32)


def _split_bf16(x):
    hi = x.astype(BF16)
    lo = (x - hi.astype(F32)).astype(BF16)
    return hi, lo


def _inproj_kernel(x_ref, g_ref, w_ref, o_ref, u_ref):
    @pl.when(pl.program_id(1) == 0)
    def _():
        x = x_ref[...]
        ms = jnp.mean(x * x, axis=-1, keepdims=True)
        u_ref[...] = (x * lax.rsqrt(ms + EPS) * g_ref[...]).astype(BF16)

    o_ref[...] = _dot(u_ref[...], w_ref[...])


def _inproj(h, gain, w, tm=1024, tn=1536):
    T, D = h.shape
    N = w.shape[1]
    return pl.pallas_call(
        _inproj_kernel,
        out_shape=(jax.ShapeDtypeStruct((T, N), F32), jax.ShapeDtypeStruct((T, D), BF16)),
        grid=(T // tm, N // tn),
        in_specs=[
            pl.BlockSpec((tm, D), lambda i, j: (i, 0)),
            pl.BlockSpec((1, D), lambda i, j: (0, 0)),
            pl.BlockSpec((D, tn), lambda i, j: (0, j)),
        ],
        out_specs=(pl.BlockSpec((tm, tn), lambda i, j: (i, j)),
                   pl.BlockSpec((tm, D), lambda i, j: (i, 0))),
        compiler_params=_cparams(("parallel", "arbitrary")),
        name="inproj",
    )(h, gain, w)


_HG_LEVELS = (64, 32, 16, 8, 4, 2)


def _hgrn_consts():
    C = CHUNK
    blocks, masks = [], []
    for n in _HG_LEVELS:
        half = n // 2
        L = np.zeros((C, C), np.float32)
        M = np.zeros((C, C), np.float32)
        for t in range(C):
            base = (t // n) * n
            mid = base + half
            if t >= mid:
                L[t, mid:t + 1] = 1.0
                M[t, base:mid] = 1.0
            else:
                L[t, t + 1:mid] = 1.0
        blocks.append(L)
        masks.append(M)
    masks.append(np.eye(C, dtype=np.float32))
    blocks.append(np.tril(np.ones((C, C), np.float32)))
    blocks.append(np.triu(np.ones((C, C), np.float32), 1))
    return np.concatenate(blocks, 0), np.stack(masks)


def _hgrn_kernel(q_ref, f_ref, i_ref, g_ref, lb_ref, on_ref, lc_ref, mc_ref, o_ref, st_ref,
                 qq_ref, kk_ref, ex_ref, sc_ref):
    C = CHUNK
    nlev = len(_HG_LEVELS)

    @pl.when(pl.program_id(1) == 0)
    def _():
        st_ref[...] = jnp.zeros(st_ref.shape, F32)

    lc = lc_ref[...]
    onorm = on_ref[...]
    n_chunks = q_ref.shape[0] // C

    def chunk(c, carry):
        r0 = pl.multiple_of(c * C, C)
        rows = pl.ds(r0, C)
        fpre = f_ref[rows, :]
        lb = lb_ref[...]
        e = jnp.exp(-jnp.abs(fpre))
        inv = 1.0 / (1.0 + e)
        sg_pos = jnp.where(fpre >= 0, inv, e * inv)
        sg_neg = jnp.where(fpre >= 0, e * inv, inv)
        lf = jnp.log(lb + (1.0 - lb) * sg_pos)
        kk_ref[...] = (1.0 - lb) * sg_neg
        qin = q_ref[rows, :]
        qq_ref[...] = qin * _sigmoid(qin)
        lf_hi, lf_lo = _split_bf16(lf)
        ex_ref[...] = jnp.exp(_dot(lc, lf_hi) + _dot(lc, lf_lo))
        for h in range(HG_HEADS):
            cs = slice(h * HG_DK, (h + 1) * HG_DK)
            q = qq_ref[:, cs]
            k = kk_ref[:, cs]
            scores = _dot_nt(q.astype(BF16), k.astype(BF16)) * mc_ref[nlev]
            for l in range(nlev):
                el = ex_ref[l * C:(l + 1) * C, cs]
                s_l = _dot_nt((q * el).astype(BF16), (k * el).astype(BF16))
                scores = scores + s_l * mc_ref[l]
            sc_ref[h] = scores.astype(BF16)
        for h in range(HG_HEADS):
            cs = slice(h * HG_DK, (h + 1) * HG_DK)
            eb = ex_ref[nlev * C:(nlev + 1) * C, cs]
            er = ex_ref[(nlev + 1) * C:(nlev + 2) * C, cs]
            st = st_ref[h]
            vb = i_ref[rows, cs].astype(BF16)
            o = _dot(sc_ref[h], vb) + _dot_nt((qq_ref[:, cs] * eb).astype(BF16), st.astype(BF16))
            st_ref[h] = st * eb[C - 1:C, :] + _dot_tn(vb, (kk_ref[:, cs] * er).astype(BF16))
            ms = jnp.mean(o * o, axis=-1, keepdims=True)
            on = o * lax.rsqrt(ms + EPS) * onorm
            g = g_ref[rows, cs]
            o_ref[rows, cs] = on * (g * _sigmoid(g))
        return carry

    lax.fori_loop(0, n_chunks, chunk, 0)


def _hgrn(z, lb, onorm, B, S, sblk=1024):
    T = B * S
    sblk = min(sblk, S)
    ns = S // sblk
    lc_np, mc_np = _hgrn_consts()
    lc = jnp.asarray(lc_np, BF16)
    mc = jnp.asarray(mc_np, F32)
    cb = HG_OFF // HG_WIDTH

    def zspec(k):
        return pl.BlockSpec((sblk, HG_WIDTH), lambda b, s, k=k: (b * ns + s, cb + k))

    return pl.pallas_call(
        _hgrn_kernel,
        out_shape=jax.ShapeDtypeStruct((T, HG_WIDTH), F32),
        grid=(B, ns),
        in_specs=[
            zspec(0), zspec(1), zspec(2), zspec(3),
            pl.BlockSpec((1, HG_WIDTH), lambda b, s: (0, 0)),
            pl.BlockSpec((1, HG_DK), lambda b, s: (0, 0)),
            pl.BlockSpec(lc.shape, lambda b, s: (0, 0)),
            pl.BlockSpec(mc.shape, lambda b, s: (0, 0, 0)),
        ],
        out_specs=pl.BlockSpec((sblk, HG_WIDTH), lambda b, s: (b * ns + s, 0)),
        scratch_shapes=[
            pltpu.VMEM((HG_HEADS, HG_DK, HG_DK), F32),
            pltpu.VMEM((CHUNK, HG_WIDTH), F32),
            pltpu.VMEM((CHUNK, HG_WIDTH), F32),
            pltpu.VMEM(((len(_HG_LEVELS) + 2) * CHUNK, HG_WIDTH), F32),
            pltpu.VMEM((HG_HEADS, CHUNK, CHUNK), BF16),
        ],
        compiler_params=_cparams(("parallel", "arbitrary")),
        name="hgrn2",
    )(z, z, z, z, lb, onorm, lc, mc)


def _dsa_kernel(topk, sq_ref, iq_ref, iwq_ref, sc_ref, ikw_ref, cn_ref, wk_ref, wvt_ref,
                o_ref, kk_ref, vt_ref, ik_ref, key_ref, plane_ref, qt_ref, m_ref, l_ref, acc_ref):
    QB = QBLOCK
    j = pl.program_id(1)
    nkb = j + 1
    S = sc_ref.shape[0]
    nb = S // QB
    idx_bits = (S - 1).bit_length()

    @pl.when(j == 0)
    def _():
        c = sc_ref[...]
        ms = jnp.mean(c * c, axis=-1, keepdims=True)
        cb = (c * lax.rsqrt(ms + EPS) * cn_ref[...]).astype(BF16)
        kfull = _dot(cb, wk_ref[...]).astype(BF16)
        vtfull = _dot_nt(wvt_ref[...], cb).astype(BF16)
        ikfull = ikw_ref[:, 0:IDX_DIM].astype(BF16)
        for kb in range(nb):
            kk_ref[kb] = kfull[kb * QB:(kb + 1) * QB]
            vt_ref[kb] = vtfull[:, kb * QB:(kb + 1) * QB]
            ik_ref[kb] = ikfull[kb * QB:(kb + 1) * QB]

    qt_ref[...] = (sq_ref[...] * (SA_DH ** -0.5 * LOG2E)).T.astype(BF16)
    iqt = iq_ref[...].T.astype(BF16)
    iwt = iwq_ref[...].T * (IDX_HEADS ** -0.5)

    row = lax.broadcasted_iota(I32, (QB, QB), 0)
    lane = lax.broadcasted_iota(I32, (QB, QB), 1)
    diag_bad = (row >= CHUNK) & (lane < CHUNK)

    n_pairs = (nkb + 1) // 2

    def score_block(kb):
        ikb = ik_ref[kb]
        acc = jnp.zeros((QB, QB), F32)
        for h in range(IDX_HEADS):
            s_h = _dot(ikb, iqt[h * IDX_DIM:(h + 1) * IDX_DIM])
            acc = acc + jnp.maximum(s_h, 0.0) * iwt[IDX_DIM + h:IDX_DIM + h + 1, :]
        acc = acc + 0.0
        acc = jnp.where((kb == j) & diag_bad, -jnp.inf, acc)
        bits = pltpu.bitcast(acc, I32)
        key_ref[kb] = jnp.where(bits < 0, bits ^ jnp.int32(0x7FFFFFFF), bits)

    def score_quad(g, carry):
        for i in range(4):
            score_block(4 * g + i)
        return carry

    lax.fori_loop(0, nkb // 4, score_quad, 0)

    @pl.when((nkb & 2) != 0)
    def _():
        score_block(4 * (nkb // 4))
        score_block(4 * (nkb // 4) + 1)

    @pl.when((nkb & 1) != 0)
    def _():
        score_block(nkb - 1)

    @pl.when(j + 1 < 2 * n_pairs)
    def _():
        key_ref[j + 1] = jnp.full((QB, QB), INT_MIN, I32)

    def build_planes(g, carry):
        a = []
        for half in range(2):
            blk = key_ref[2 * g + half] ^ jnp.int32(INT_MIN)
            a.extend(blk[r * 8:(r + 1) * 8, :] for r in range(QB // 8))
        for sh, m in ((16, 0x0000FFFF), (8, 0x00FF00FF), (4, 0x0F0F0F0F), (2, 0x33333333),
                      (1, 0x55555555)):
            for k in range(32):
                if k & sh == 0:
                    t = (lax.shift_right_logical(a[k], jnp.int32(sh)) ^ a[k + sh]) & jnp.int32(m)
                    a[k + sh] = a[k + sh] ^ t
                    a[k] = a[k] ^ lax.shift_left(t, jnp.int32(sh))
        rows = pl.ds(pl.multiple_of(g * 8, 8), 8)
        for b in range(32):
            plane_ref[b, rows, :] = a[b]
        return carry

    def search(_):
        lax.fori_loop(0, n_pairs, build_planes, 0)
        wrow = lax.broadcasted_iota(I32, (nb // 2 * 8, QB), 0)
        cand = jnp.where((wrow >> 3) < n_pairs, jnp.int32(-1), jnp.int32(0))
        above = jnp.zeros((1, QB), I32)
        u = jnp.zeros((1, QB), I32)
        for b in range(31, -1, -1):
            ones = cand & plane_ref[b]
            cnt1 = jnp.sum(lax.population_count(ones), axis=0, keepdims=True)
            take = (above + cnt1) >= topk
            cand = jnp.where(take, ones, cand ^ ones)
            above = jnp.where(take, above, above + cnt1)
            u = u | jnp.where(take, jnp.int32(INT_MIN if b == 31 else 1 << b), jnp.int32(0))
        thr = u ^ jnp.int32(INT_MIN)
        need = topk - above
        n_tied = jnp.sum(lax.population_count(cand), axis=0, keepdims=True)

        def break_ties(_):
            base = (wrow >> 3) * (2 * QB) + (wrow & 7)

            def ibit(i, jp):
                c = jp + lax.shift_left(jnp.int32(1), idx_bits - 1 - i)
                nbits = jnp.clip((c - base + 7) >> 3, 0, 32)
                below = jnp.where(nbits >= 32, jnp.int32(-1),
                                  lax.shift_left(jnp.int32(1), nbits) - 1)
                cnt = jnp.sum(lax.population_count(cand & below), axis=0, keepdims=True)
                return jnp.where(cnt < need, c, jp)

            return lax.fori_loop(0, idx_bits, ibit, jnp.zeros((1, QB), I32))

        def keep_ties(_):
            return jnp.full((1, QB), S, I32)

        jp = lax.cond(jnp.max(n_tied - need) > 0, break_ties, keep_ties, 0)
        return thr, jp

    def take_all(_):
        return (jnp.full((1, QB), INT_MIN, I32), jnp.full((1, QB), -1, I32))

    thr, jp = lax.cond(nkb * QB > topk, search, take_all, 0)

    m_ref[...] = jnp.full(m_ref.shape, NEG_BIG, F32)
    l_ref[...] = jnp.zeros(l_ref.shape, F32)
    acc_ref[...] = jnp.zeros(acc_ref.shape, F32)

    def attend_block(kb):
        key = key_ref[kb]
        kidx = row + kb * QB
        sel = (key > thr) | ((key == thr) & (kidx <= jp))
        hidden = (kidx >= j * QB + CHUNK) & (lane < CHUNK)
        bias = jnp.where(sel & jnp.logical_not(hidden), 0.0, -jnp.inf)
        kblk = kk_ref[kb]
        vtb = vt_ref[kb]
        for h in range(SA_HEADS):
            lg = _dot(kblk, qt_ref[h * SA_DH:(h + 1) * SA_DH, :]) + bias
            m_old = m_ref[h:h + 1, :]
            m_new = jnp.maximum(m_old, jnp.max(lg, axis=0, keepdims=True))
            alpha = jnp.exp2(m_old - m_new)
            p = jnp.exp2(lg - m_new)
            l_ref[h:h + 1, :] = alpha * l_ref[h:h + 1, :] + jnp.sum(p, axis=0, keepdims=True)
            acc_ref[h] = acc_ref[h] * alpha + _dot(vtb, p.astype(BF16))
            m_ref[h:h + 1, :] = m_new

    def attend(g, carry):
        for i in range(8):
            attend_block(8 * g + i)
        return carry

    n_oct = nkb // 8
    lax.fori_loop(0, n_oct, attend, 0)

    @pl.when((nkb & 4) != 0)
    def _():
        for i in range(4):
            attend_block(8 * n_oct + i)

    n_quads = nkb // 4

    @pl.when((nkb & 2) != 0)
    def _():
        attend_block(4 * n_quads)
        attend_block(4 * n_quads + 1)

    @pl.when((nkb & 1) != 0)
    def _():
        attend_block(nkb - 1)

    outs = []
    for h in range(SA_HEADS):
        outs.append(acc_ref[h] * (1.0 / l_ref[h:h + 1, :]))
    o_ref[...] = jnp.concatenate(outs, axis=0).T


def _dsa(z, cnorm, wk, wvt, B, S):
    T = B * S
    nb = S // QBLOCK
    topk = min(TOPK_MAX, S // 4)
    assert nb % 2 == 0
    kern = functools.partial(_dsa_kernel, topk)
    return pl.pallas_call(
        kern,
        out_shape=jax.ShapeDtypeStruct((T, SA_WIDTH), F32),
        grid=(B, nb),
        in_specs=[
            pl.BlockSpec((QBLOCK, SA_WIDTH), lambda b, j: (b * nb + j, SQ_OFF // SA_WIDTH)),
            pl.BlockSpec((QBLOCK, IDX_HEADS * IDX_DIM),
                         lambda b, j: (b * nb + j, IQ_OFF // (IDX_HEADS * IDX_DIM))),
            pl.BlockSpec((QBLOCK, LANES), lambda b, j: (b * nb + j, IKW_OFF // LANES)),
            pl.BlockSpec((S, SA_LATENT), lambda b, j: (b, SC_OFF // SA_LATENT)),
            pl.BlockSpec((S, LANES), lambda b, j: (b, IKW_OFF // LANES)),
            pl.BlockSpec((1, SA_LATENT), lambda b, j: (0, 0)),
            pl.BlockSpec((SA_LATENT, SA_DH), lambda b, j: (0, 0)),
            pl.BlockSpec((SA_DH, SA_LATENT), lambda b, j: (0, 0)),
        ],
        out_specs=pl.BlockSpec((QBLOCK, SA_WIDTH), lambda b, j: (b * nb + j, 0)),
        scratch_shapes=[
            pltpu.VMEM((nb, QBLOCK, SA_DH), BF16),
            pltpu.VMEM((nb, SA_DH, QBLOCK), BF16),
            pltpu.VMEM((nb, QBLOCK, IDX_DIM), BF16),
            pltpu.VMEM((nb, QBLOCK, QBLOCK), I32),
            pltpu.VMEM((32, nb // 2 * 8, QBLOCK), I32),
            pltpu.VMEM((SA_WIDTH, QBLOCK), BF16),
            pltpu.VMEM((SA_HEADS, QBLOCK), F32),
            pltpu.VMEM((SA_HEADS, QBLOCK), F32),
            pltpu.VMEM((SA_HEADS, SA_DH, QBLOCK), F32),
        ],
        compiler_params=_cparams(("parallel", "arbitrary")),
        name="dsa",
    )(z, z, z, z, z, cnorm, wk, wvt)


def _sb_kernel(q_ref, k_ref, v_ref, uo_ref, o_ref, kt_ref, vb_ref, run_ref, acc_ref,
               ls_ref, lkh_ref, lkl_ref, tot_ref, a_ref):
    QB = QBLOCK
    j = pl.program_id(1)
    S = k_ref.shape[0]
    nb = S // QB
    npair = SB_HEADS // 2
    lane = lax.broadcasted_iota(I32, (QB, QB), 1)
    row = lax.broadcasted_iota(I32, (QB, QB), 0)

    @pl.when(j == 0)
    def _():
        for p in range(npair):
            cols = slice(p * LANES, (p + 1) * LANES)
            kt = k_ref[:, cols].T
            vv = v_ref[:, cols]
            for hh in range(2):
                lo = hh * SB_DH
                ktz = jnp.where((row >= lo) & (row < lo + SB_DH), 1.0, 0.0)
                vz = jnp.where((lane >= lo) & (lane < lo + SB_DH), 1.0, 0.0)
                for kb in range(nb):
                    ks = slice(kb * QB, (kb + 1) * QB)
                    kt_ref[kb, 2 * p + hh] = (kt[:, ks] * ktz).astype(BF16)
                    vb_ref[kb, 2 * p + hh] = (vv[ks, :] * vz).astype(BF16)

    uo = uo_ref[...]
    strict = lane < row
    qs = (q_ref[...] * (SB_DH ** -0.5)).astype(BF16)

    def all_heads(kbs, diag):
        nk = len(kbs)
        for h in range(SB_HEADS):
            p = h // 2
            for i, kb in enumerate(kbs):
                s = h * nk + i
                z = _dot(qs[:, p * LANES:(p + 1) * LANES], kt_ref[kb, h])
                l1p = jnp.log(1.0 + jnp.exp(-jnp.abs(z)))
                ls_pos = jnp.minimum(z, 0.0) - l1p
                lk = ls_pos - z
                if diag and i == 0:
                    lk = jnp.where(strict, lk, 0.0)
                lk_hi, lk_lo = _split_bf16(lk)
                ls_ref[s] = ls_pos
                lkh_ref[s] = lk_hi
                lkl_ref[s] = lk_lo
                tot_ref[s] = jnp.sum(lk, axis=1, keepdims=True)
        top = None
        for h in range(SB_HEADS):
            run = None if diag else run_ref[h]
            for i in range(nk):
                s = h * nk + i
                after = _dot(lkh_ref[s], uo) + _dot(lkl_ref[s], uo)
                if diag and i == 0:
                    a = jnp.where(strict, jnp.exp(ls_ref[s] + after), 0.0)
                    run = tot_ref[s]
                else:
                    a = jnp.exp(ls_ref[s] + after + run)
                    run = run + tot_ref[s]
                a_ref[s] = a.astype(BF16)
            run_ref[h] = run
            top = run if top is None else jnp.maximum(top, run)
        for p in range(npair):
            pv = None
            for h in (2 * p, 2 * p + 1):
                for i, kb in enumerate(kbs):
                    out = _dot(a_ref[h * nk + i], vb_ref[kb, h])
                    pv = out if pv is None else pv + out
            if diag:
                acc_ref[p] = pv
            else:
                acc_ref[p] += pv
        return jnp.max(top) > SB_UNDERFLOW

    n_first = 3
    alive = lax.cond(
        j >= 2, lambda: all_heads([j, j - 1, j - 2], True),
        lambda: lax.cond(j == 1, lambda: all_heads([j, j - 1], True), lambda: all_heads([j], True)))

    def cond(c):
        return (c[0] >= 1) & c[1]

    def body(c):
        return c[0] - 2, all_heads([c[0], c[0] - 1], False)

    kb_left, alive = lax.while_loop(cond, body, (j - n_first, alive))

    @pl.when((kb_left == 0) & alive)
    def _():
        all_heads([0], False)

    o_ref[...] = jnp.concatenate([acc_ref[p] for p in range(npair)], axis=1)


def _sb(z, B, S):
    T = B * S
    nb = S // QBLOCK
    u = np.tril(np.ones((QBLOCK, QBLOCK), np.float32), -1)
    uo = jnp.asarray(u, BF16)
    qoff = SB_OFF // SB_WIDTH
    return pl.pallas_call(
        _sb_kernel,
        out_shape=jax.ShapeDtypeStruct((T, SB_WIDTH), F32),
        grid=(B, nb),
        in_specs=[
            pl.BlockSpec((QBLOCK, SB_WIDTH), lambda b, j: (b * nb + j, qoff)),
            pl.BlockSpec((S, SB_WIDTH), lambda b, j: (b, qoff + 1)),
            pl.BlockSpec((S, SB_WIDTH), lambda b, j: (b, qoff + 2)),
            pl.BlockSpec(uo.shape, lambda b, j: (0, 0)),
        ],
        out_specs=pl.BlockSpec((QBLOCK, SB_WIDTH), lambda b, j: (b * nb + j, 0)),
        scratch_shapes=[
            pltpu.VMEM((nb, SB_HEADS, LANES, QBLOCK), BF16),
            pltpu.VMEM((nb, SB_HEADS, QBLOCK, LANES), BF16),
            pltpu.VMEM((SB_HEADS, QBLOCK, 1), F32),
            pltpu.VMEM((SB_HEADS // 2, QBLOCK, LANES), F32),
            pltpu.VMEM((3 * SB_HEADS, QBLOCK, QBLOCK), F32),
            pltpu.VMEM((3 * SB_HEADS, QBLOCK, QBLOCK), BF16),
            pltpu.VMEM((3 * SB_HEADS, QBLOCK, QBLOCK), BF16),
            pltpu.VMEM((3 * SB_HEADS, QBLOCK, 1), F32),
            pltpu.VMEM((3 * SB_HEADS, QBLOCK, QBLOCK), BF16),
        ],
        compiler_params=_cparams(("parallel", "arbitrary")),
        name="stickbreak",
    )(z, z, z, uo)


def _merge_kernel(a_ref, b_ref, c_ref, u_ref, h_ref, wgz_ref, wb_ref, wo_ref, gn_ref, ho_ref, v_ref):
    u = u_ref[...]
    merged = None
    for n, br in enumerate((a_ref, b_ref, c_ref)):
        proj = _dot(br[...].astype(BF16), wb_ref[n])
        gate = _sigmoid(_dot(u, wgz_ref[:, n * D_MODEL:(n + 1) * D_MODEL]))
        merged = gate * proj if merged is None else merged + gate * proj
    hn = h_ref[...] + _dot(merged.astype(BF16), wo_ref[...])
    ho_ref[...] = hn
    ms = jnp.mean(hn * hn, axis=-1, keepdims=True)
    v_ref[...] = (hn * lax.rsqrt(ms + EPS) * gn_ref[...]).astype(BF16)


def _merge(a, b, c, u, h, wgz, wb, wo, gn, tm=256):
    T, D = h.shape
    bw = BRANCH_WIDTH
    return pl.pallas_call(
        _merge_kernel,
        out_shape=(jax.ShapeDtypeStruct((T, D), F32), jax.ShapeDtypeStruct((T, D), BF16)),
        grid=(T // tm,),
        in_specs=[
            pl.BlockSpec((tm, bw), lambda i: (i, 0)),
            pl.BlockSpec((tm, bw), lambda i: (i, 0)),
            pl.BlockSpec((tm, bw), lambda i: (i, 0)),
            pl.BlockSpec((tm, D), lambda i: (i, 0)),
            pl.BlockSpec((tm, D), lambda i: (i, 0)),
            pl.BlockSpec((D, GZW), lambda i: (0, 0)),
            pl.BlockSpec((N_BRANCH, bw, D), lambda i: (0, 0, 0)),
            pl.BlockSpec((D, D), lambda i: (0, 0)),
            pl.BlockSpec((1, D), lambda i: (0, 0)),
        ],
        out_specs=(pl.BlockSpec((tm, D), lambda i: (i, 0)), pl.BlockSpec((tm, D), lambda i: (i, 0))),
        compiler_params=_cparams(("parallel",)),
        name="merge",
    )(a, b, c, u, h, wgz, wb, wo, gn)


def _swiglu_kernel(v_ref, h_ref, wg_ref, wu_ref, wd_ref, o_ref, acc_ref):
    f = pl.program_id(1)

    @pl.when(f == 0)
    def _():
        acc_ref[...] = h_ref[...]

    v = v_ref[...]
    gt = _dot(v, wg_ref[...])
    up = _dot(v, wu_ref[...])
    act = (gt * _sigmoid(gt) * up).astype(BF16)
    acc_ref[...] += _dot(act, wd_ref[...])

    @pl.when(f == pl.num_programs(1) - 1)
    def _():
        o_ref[...] = acc_ref[...]


def _swiglu(v, h, wg, wu, wd, tm=512, tf=1408):
    T, D = h.shape
    F = wg.shape[1]
    return pl.pallas_call(
        _swiglu_kernel,
        out_shape=jax.ShapeDtypeStruct((T, D), F32),
        grid=(T // tm, F // tf),
        in_specs=[
            pl.BlockSpec((tm, D), lambda i, f: (i, 0)),
            pl.BlockSpec((tm, D), lambda i, f: (i, 0)),
            pl.BlockSpec((D, tf), lambda i, f: (0, f)),
            pl.BlockSpec((D, tf), lambda i, f: (0, f)),
            pl.BlockSpec((tf, D), lambda i, f: (f, 0)),
        ],
        out_specs=pl.BlockSpec((tm, D), lambda i, f: (i, 0)),
        scratch_shapes=[pltpu.VMEM((tm, D), F32)],
        compiler_params=_cparams(("parallel", "arbitrary")),
        name="swiglu",
    )(v, h, wg, wu, wd)


MOE_TILE = 1024
MOE_CHUNK = 288


def _router_kernel(h_ref, gn_ref, r_ref, us_ref, cw_ref, cwt_ref, rkt_ref):
    hh = h_ref[...]
    ms = jnp.mean(hh * hh, axis=-1, keepdims=True)
    vf = hh * lax.rsqrt(ms + EPS) * gn_ref[...]
    v_hi, v_lo = _split_bf16(vf)
    r_hi, r_lo = _split_bf16(r_ref[...])
    logits = _dot(v_hi, r_hi) + _dot(v_hi, r_lo) + _dot(v_lo, r_hi)
    lane = lax.broadcasted_iota(I32, logits.shape, 1)
    logits = jnp.where(lane < N_EXPERTS, logits, -jnp.inf)
    m1 = jnp.max(logits, axis=-1, keepdims=True)
    i1 = jnp.min(jnp.where(logits == m1, lane, LANES), axis=-1, keepdims=True)
    rest = jnp.where(lane == i1, -jnp.inf, logits)
    m2 = jnp.max(rest, axis=-1, keepdims=True)
    i2 = jnp.min(jnp.where(rest == m2, lane, LANES), axis=-1, keepdims=True)
    e2 = jnp.exp(m2 - m1)
    w1 = 1.0 / (1.0 + e2)
    cw = jnp.where(lane == i1, w1, 0.0) + jnp.where(lane == i2, e2 * w1, 0.0)
    cw_ref[...] = cw
    cwt = cw.T
    cwt_ref[...] = cwt[0:N_EXPERTS]
    sel = jnp.where(cwt != 0.0, 1.0, 0.0).astype(BF16)
    rkt_ref[...] = _dot(sel, us_ref[...])[0:N_EXPERTS]


def _router(h, gn, router, tm=MOE_TILE):
    T, D = h.shape
    us = jnp.asarray(np.triu(np.ones((tm, tm), np.float32), 1), BF16)
    return pl.pallas_call(
        _router_kernel,
        out_shape=(jax.ShapeDtypeStruct((T, LANES), F32),
                   jax.ShapeDtypeStruct((N_EXPERTS, T), F32),
                   jax.ShapeDtypeStruct((N_EXPERTS, T), F32)),
        grid=(T // tm,),
        in_specs=[
            pl.BlockSpec((tm, D), lambda i: (i, 0)),
            pl.BlockSpec((1, D), lambda i: (0, 0)),
            pl.BlockSpec((D, LANES), lambda i: (0, 0)),
            pl.BlockSpec((tm, tm), lambda i: (0, 0)),
        ],
        out_specs=(pl.BlockSpec((tm, LANES), lambda i: (i, 0)),
                   pl.BlockSpec((N_EXPERTS, tm), lambda i: (0, i)),
                   pl.BlockSpec((N_EXPERTS, tm), lambda i: (0, i))),
        compiler_params=_cparams(("parallel",)),
        name="router",
    )(h, gn, router, us)


MOE_SUB = 256
MOE_WIN = 112


def _moe_kernel(v_ref, h_ref, cw_ref, cwt_ref, rkt_ref, wg_ref, wu_ref, wd_ref, o_ref,
                xg_ref, ce_ref, ya_ref):
    e = pl.program_id(1)
    f = pl.program_id(2)
    nf = pl.num_programs(2)
    tm = v_ref.shape[0]
    CH, SUB, WIN = MOE_CHUNK, MOE_SUB, MOE_WIN
    nsub = tm // SUB

    @pl.when((e == 0) & (f == 0))
    def _():
        o_ref[...] = h_ref[...]

    ce_row = cwt_ref[pl.ds(e, 1), :]
    routed = ce_row != 0.0
    rank_row = rkt_ref[pl.ds(e, 1), :]
    slot_row = jnp.where(routed, rank_row, -1.0)
    count = jnp.sum(jnp.where(routed, 1, 0).astype(I32))
    n_chunks = (count + (CH - 1)) // CH
    bounds = [jnp.int32(0)]
    for s in range(1, nsub):
        bounds.append(rank_row[0, s * SUB].astype(I32))
    bounds.append(count)
    wiota = lax.broadcasted_iota(I32, (WIN, SUB), 0).astype(F32)

    def for_windows(fn):
        def window(s, r0):
            slots = slot_row[:, s * SUB:(s + 1) * SUB]
            pick = jnp.where(slots == wiota + r0.astype(F32), 1.0, 0.0).astype(BF16)
            fn(s, r0, pick)

        starts = [pl.multiple_of((bounds[s] // 16) * 16, 16) for s in range(nsub)]
        for s in range(nsub):
            window(s, starts[s])
        for s in range(nsub):
            nw = (bounds[s + 1] - starts[s] + (WIN - 1)) // WIN

            def body(m, carry, s=s):
                window(s, pl.multiple_of(starts[s] + m * WIN, 16))
                return carry

            lax.fori_loop(1, nw, body, 0)

    @pl.when(f == 0)
    def _():
        def clear(i, carry):
            rows = pl.ds(pl.multiple_of(i * CH, 16), CH)
            xg_ref[rows, :] = jnp.zeros((CH, xg_ref.shape[1]), BF16)
            ce_ref[rows, :] = jnp.zeros((CH, LANES), F32)
            ya_ref[rows, :] = jnp.zeros((CH, ya_ref.shape[1]), F32)
            return carry

        lax.fori_loop(0, jnp.minimum(n_chunks + 1, xg_ref.shape[0] // CH), clear, 0)
        lane = lax.broadcasted_iota(I32, (WIN, LANES), 1)

        def gather(s, r0, pick):
            toks = slice(s * SUB, (s + 1) * SUB)
            rows = pl.ds(r0, WIN)
            got = _dot(pick, v_ref[toks, :])
            xg_ref[rows, :] = (xg_ref[rows, :].astype(F32) + got).astype(BF16)
            c_hi, c_lo = _split_bf16(cw_ref[toks, :])
            cg = _dot(pick, c_hi) + _dot(pick, c_lo)
            ce = jnp.sum(jnp.where(lane == e, cg, 0.0), axis=-1, keepdims=True)
            ce_ref[rows, :] += jnp.broadcast_to(ce, (WIN, LANES))

        for_windows(gather)

    def expert(i, carry):
        rows = pl.ds(pl.multiple_of(i * CH, 16), CH)
        xg = xg_ref[rows, :]
        gt = _dot(xg, wg_ref[0])
        up = _dot(xg, wu_ref[0])
        act = (gt * _sigmoid(gt) * up * ce_ref[rows, 0:1]).astype(BF16)
        ya_ref[rows, :] += _dot(act, wd_ref[0])
        return carry

    lax.fori_loop(0, n_chunks, expert, 0)

    @pl.when(f == nf - 1)
    def _():
        def scatter(s, r0, pick):
            ye = ya_ref[pl.ds(r0, WIN), :].astype(BF16)
            o_ref[s * SUB:(s + 1) * SUB, :] += _dot_tn(pick, ye)

        for_windows(scatter)


def _moe(v, h, cw, cwt, rkt, wg, wu, wd, tm=MOE_TILE, fsplit=2):
    T, D = h.shape
    E, _, F = wg.shape
    tf = F // fsplit
    rows = (-(-tm // MOE_CHUNK)) * MOE_CHUNK
    assert rows >= tm + MOE_WIN and tm % MOE_SUB == 0
    return pl.pallas_call(
        _moe_kernel,
        out_shape=jax.ShapeDtypeStruct((T, D), F32),
        grid=(T // tm, E, fsplit),
        in_specs=[
            pl.BlockSpec((tm, D), lambda i, e, f: (i, 0), pipeline_mode=pl.Buffered(1)),
            pl.BlockSpec((tm, D), lambda i, e, f: (i, 0), pipeline_mode=pl.Buffered(1)),
            pl.BlockSpec((tm, LANES), lambda i, e, f: (i, 0), pipeline_mode=pl.Buffered(1)),
            pl.BlockSpec((N_EXPERTS, tm), lambda i, e, f: (0, i)),
            pl.BlockSpec((N_EXPERTS, tm), lambda i, e, f: (0, i)),
            pl.BlockSpec((1, D, tf), lambda i, e, f: (e, 0, f)),
            pl.BlockSpec((1, D, tf), lambda i, e, f: (e, 0, f)),
            pl.BlockSpec((1, tf, D), lambda i, e, f: (e, f, 0)),
        ],
        out_specs=pl.BlockSpec((tm, D), lambda i, e, f: (i, 0)),
        scratch_shapes=[
            pltpu.VMEM((rows, D), BF16),
            pltpu.VMEM((rows, LANES), F32),
            pltpu.VMEM((rows, D), F32),
        ],
        compiler_params=_cparams(("parallel", "arbitrary", "arbitrary")),
        name="moe",
    )(v, h, cw, cwt, rkt, wg, wu, wd)


def _ple_kernel(final, h_ref, p_ref, gn_ref, wg_ref, wp_ref, fn_ref, o_ref):
    h = h_ref[...]
    ms = jnp.mean(h * h, axis=-1, keepdims=True)
    u = (h * lax.rsqrt(ms + EPS) * gn_ref[...]).astype(BF16)
    gate = _sigmoid(_dot(u, wg_ref[...]))
    hn = h + gate * _dot(p_ref[0].astype(BF16), wp_ref[...])
    if final:
        ms = jnp.mean(hn * hn, axis=-1, keepdims=True)
        hn = hn * lax.rsqrt(ms + EPS) * fn_ref[...]
    o_ref[...] = hn


def _ple(h, p, layer, gn, wg, wp, fn, final, tm=512):
    T, D = h.shape
    return pl.pallas_call(
        functools.partial(_ple_kernel, final),
        out_shape=jax.ShapeDtypeStruct((T, D), F32),
        grid=(T // tm,),
        in_specs=[
            pl.BlockSpec((tm, D), lambda i: (i, 0)),
            pl.BlockSpec((1, tm, PLE_DIM), lambda i: (layer, i, 0)),
            pl.BlockSpec((1, D), lambda i: (0, 0)),
            pl.BlockSpec((D, D), lambda i: (0, 0)),
            pl.BlockSpec((PLE_DIM, D), lambda i: (0, 0)),
            pl.BlockSpec((1, D), lambda i: (0, 0)),
        ],
        out_specs=pl.BlockSpec((tm, D), lambda i: (i, 0)),
        compiler_params=_cparams(("parallel",)),
        name="ple",
    )(h, p, gn, wg, wp, fn)


def _reorder_kernel(w_ref, o_ref, ogz_ref):
    o_sq = 4 * HG_WIDTH
    o_sc = o_sq + SA_WIDTH
    o_iq = o_sc + SA_LATENT
    o_ik = o_iq + IDX_HEADS * IDX_DIM
    o_bq = o_ik + IDX_DIM + IDX_HEADS
    o_gz = o_bq + 3 * SB_WIDTH

    def put(dst, lo, hi):
        o_ref[:, dst:dst + (hi - lo)] = w_ref[0, :, lo:hi].astype(BF16)

    ogz_ref[...] = w_ref[0, :, o_gz:o_gz + GZW].astype(BF16)
    put(HG_OFF, 0, o_sq)
    put(SQ_OFF, o_sq, o_sc)
    put(IQ_OFF, o_iq, o_ik)
    put(SC_OFF, o_sc, o_iq)
    put(IKW_OFF, o_ik, o_bq)
    used = IDX_DIM + IDX_HEADS
    o_ref[:, IKW_OFF + used:SB_OFF] = jnp.zeros((o_ref.shape[0], LANES - used), BF16)
    put(SB_OFF, o_bq, o_gz)


def _reorder_w_in(w_in, layer, tr=256):
    _, R, C = w_in.shape
    return pl.pallas_call(
        _reorder_kernel,
        out_shape=(jax.ShapeDtypeStruct((R, ZW), BF16), jax.ShapeDtypeStruct((R, GZW), BF16)),
        grid=(R // tr,),
        in_specs=[pl.BlockSpec((1, tr, C), lambda i: (layer, i, 0))],
        out_specs=(pl.BlockSpec((tr, ZW), lambda i: (i, 0)), pl.BlockSpec((tr, GZW), lambda i: (i, 0))),
        compiler_params=_cparams(("parallel",)),
        name="reorder_w_in",
    )(w_in)


def kernel(x, p, lb_param, norm_mix, w_in, hg_onorm, sa_cnorm, sa_wk, sa_wv, w_branch, w_out,
           norm_ffn, dense_wg, dense_wu, dense_wd, moe_router, moe_wg, moe_wu, moe_wd, norm_ple,
           ple_gate, ple_proj, norm_final):
    B, S, D = x.shape
    T = B * S
    depth = w_in.shape[0]
    lb_sm = jax.nn.softmax(lb_param.astype(F32), axis=0)
    lb_all = jnp.cumsum(lb_sm, axis=0) - lb_sm[0:1]
    h = x.reshape(T, D)
    for i in range(depth):
        w_mix, w_gz = _reorder_w_in(w_in, i)
        z, u = _inproj(h, norm_mix[i][None, :], w_mix)
        a = _hgrn(z, lb_all[i][None, :], hg_onorm[i][None, :], B, S)
        b = _dsa(z, sa_cnorm[i][None, :], sa_wk[i].astype(BF16), sa_wv[i].T.astype(BF16), B, S)
        c = _sb(z, B, S)
        h, v = _merge(a, b, c, u, h, w_gz, w_branch[i].astype(BF16), w_out[i].astype(BF16),
                      norm_ffn[i][None, :])
        jj = i // 2
        if i % 2 == 0:
            h = _swiglu(v, h, dense_wg[jj].astype(BF16), dense_wu[jj].astype(BF16),
                        dense_wd[jj].astype(BF16))
        else:
            router = jnp.pad(moe_router[jj], ((0, 0), (0, LANES - N_EXPERTS)))
            cw, cwt, rkt = _router(h, norm_ffn[i][None, :], router)
            h = _moe(v, h, cw, cwt, rkt, moe_wg[jj].astype(BF16), moe_wu[jj].astype(BF16),
                     moe_wd[jj].astype(BF16))
        h = _ple(h, p.reshape(depth, T, PLE_DIM), i, norm_ple[i][None, :],
                 ple_gate[i].astype(BF16), ple_proj[i].astype(BF16), norm_final[None, :],
                 i == depth - 1)
    return h.reshape(B, S, D)
```

```python
import functools

import numpy as np
import jax
import jax.numpy as jnp
from jax import lax
from jax.experimental import pallas as pl
from jax.experimental.pallas import tpu as pltpu

F32 = jnp.float32
BF16 = jnp.bfloat16
I32 = jnp.int32

D_MODEL = 1024
CHUNK = 64
QBLOCK = 128
HG_HEADS = 4
HG_DK = 128
HG_WIDTH = HG_HEADS * HG_DK
SA_HEADS = 8
SA_DH = 64
SA_WIDTH = SA_HEADS * SA_DH
SA_LATENT = 128
IDX_HEADS = 4
IDX_DIM = 64
TOPK_MAX = 256
SB_HEADS = 8
SB_DH = 64
SB_WIDTH = SB_HEADS * SB_DH
N_BRANCH = 3
BRANCH_WIDTH = 512
N_EXPERTS = 8
PLE_DIM = 256
EPS = 1e-6

LANES = 128

HG_OFF = 0
SQ_OFF = HG_OFF + 4 * HG_WIDTH
IQ_OFF = SQ_OFF + SA_WIDTH
SC_OFF = IQ_OFF + IDX_HEADS * IDX_DIM
IKW_OFF = SC_OFF + SA_LATENT
SB_OFF = IKW_OFF + LANES
ZW = SB_OFF + 3 * SB_WIDTH
GZW = N_BRANCH * D_MODEL

VMEM_LIMIT = 56 * 1024 * 1024

NEG_BIG = -1e30
INT_MIN = -2147483648
LOG2E = 1.4426950408889634
SB_UNDERFLOW = -104.0


def _cparams(sem):
    return pltpu.CompilerParams(dimension_semantics=sem, vmem_limit_bytes=VMEM_LIMIT)


def _sigmoid(x):
    e = jnp.exp(-jnp.abs(x))
    inv = 1.0 / (1.0 + e)
    return jnp.where(x >= 0, inv, e * inv)


def _dot(a, b):
    return jnp.dot(a, b, preferred_element_type=F32)


def _dot_nt(a, b):
    return lax.dot_general(a, b, (((1,), (1,)), ((), ())), preferred_element_type=F32)


def _dot_tn(a, b):
    return lax.dot_general(a, b, (((0,), (0,)), ((), ())), preferred_element_type=F32)


def _split_bf16(x):
    hi = x.astype(BF16)
    lo = (x - hi.astype(F32)).astype(BF16)
    return hi, lo


def _inproj_kernel(x_ref, g_ref, w_ref, o_ref, u_ref):
    @pl.when(pl.program_id(1) == 0)
    def _():
        x = x_ref[...]
        ms = jnp.mean(x * x, axis=-1, keepdims=True)
        u_ref[...] = (x * lax.rsqrt(ms + EPS) * g_ref[...]).astype(BF16)

    o_ref[...] = _dot(u_ref[...], w_ref[...])


def _inproj(h, gain, w, tm=1024, tn=1536):
    T, D = h.shape
    N = w.shape[1]
    return pl.pallas_call(
        _inproj_kernel,
        out_shape=(jax.ShapeDtypeStruct((T, N), F32), jax.ShapeDtypeStruct((T, D), BF16)),
        grid=(T // tm, N // tn),
        in_specs=[
            pl.BlockSpec((tm, D), lambda i, j: (i, 0)),
            pl.BlockSpec((1, D), lambda i, j: (0, 0)),
            pl.BlockSpec((D, tn), lambda i, j: (0, j)),
        ],
        out_specs=(pl.BlockSpec((tm, tn), lambda i, j: (i, j)),
                   pl.BlockSpec((tm, D), lambda i, j: (i, 0))),
        compiler_params=_cparams(("parallel", "arbitrary")),
        name="inproj",
    )(h, gain, w)


_HG_LEVELS = (64, 32, 16, 8, 4, 2)


def _hgrn_consts():
    C = CHUNK
    blocks, masks = [], []
    for n in _HG_LEVELS:
        half = n // 2
        L = np.zeros((C, C), np.float32)
        M = np.zeros((C, C), np.float32)
        for t in range(C):
            base = (t // n) * n
            mid = base + half
            if t >= mid:
                L[t, mid:t + 1] = 1.0
                M[t, base:mid] = 1.0
            else:
                L[t, t + 1:mid] = 1.0
        blocks.append(L)
        masks.append(M)
    masks.append(np.eye(C, dtype=np.float32))
    blocks.append(np.tril(np.ones((C, C), np.float32)))
    blocks.append(np.triu(np.ones((C, C), np.float32), 1))
    return np.concatenate(blocks, 0), np.stack(masks)


def _hgrn_kernel(q_ref, f_ref, i_ref, g_ref, lb_ref, on_ref, lc_ref, mc_ref, o_ref, st_ref,
                 qq_ref, kk_ref, lfh_ref, lfl_ref, ex_ref, sc_ref):
    C = CHUNK
    nlev = len(_HG_LEVELS)

    @pl.when(pl.program_id(1) == 0)
    def _():
        st_ref[...] = jnp.zeros(st_ref.shape, F32)

    lc = lc_ref[...]
    onorm = on_ref[...]
    n_chunks = q_ref.shape[0] // C

    def gates(c, slot):
        rows = pl.ds(pl.multiple_of(c * C, C), C)
        fpre = f_ref[rows, :]
        lb = lb_ref[...]
        e = jnp.exp(-jnp.abs(fpre))
        inv = 1.0 / (1.0 + e)
        sg_pos = jnp.where(fpre >= 0, inv, e * inv)
        sg_neg = jnp.where(fpre >= 0, e * inv, inv)
        lf = jnp.log(lb + (1.0 - lb) * sg_pos)
        kk_ref[slot] = (1.0 - lb) * sg_neg
        qin = q_ref[rows, :]
        qq_ref[slot] = qin * _sigmoid(qin)
        lf_hi, lf_lo = _split_bf16(lf)
        lfh_ref[slot] = lf_hi
        lfl_ref[slot] = lf_lo

    gates(0, 0)

    def chunk(c, carry):
        r0 = pl.multiple_of(c * C, C)
        rows = pl.ds(r0, C)
        slot = c % 2
        ex_ref[...] = jnp.exp(_dot(lc, lfh_ref[slot]) + _dot(lc, lfl_ref[slot]))
        gates(jnp.minimum(c + 1, n_chunks - 1), 1 - slot)
        for h in range(HG_HEADS):
            cs = slice(h * HG_DK, (h + 1) * HG_DK)
            q = qq_ref[slot, :, cs]
            k = kk_ref[slot, :, cs]
            scores = _dot_nt(q.astype(BF16), k.astype(BF16)) * mc_ref[nlev]
            for l in range(nlev):
                el = ex_ref[l * C:(l + 1) * C, cs]
                s_l = _dot_nt((q * el).astype(BF16), (k * el).astype(BF16))
                scores = scores + s_l * mc_ref[l]
            sc_ref[h] = scores.astype(BF16)
        for h in range(HG_HEADS):
            cs = slice(h * HG_DK, (h + 1) * HG_DK)
            eb = ex_ref[nlev * C:(nlev + 1) * C, cs]
            er = ex_ref[(nlev + 1) * C:(nlev + 2) * C, cs]
            st = st_ref[h]
            vb = i_ref[rows, cs].astype(BF16)
            o = _dot(sc_ref[h], vb) + \
                _dot_nt((qq_ref[slot, :, cs] * eb).astype(BF16), st.astype(BF16))
            st_ref[h] = st * eb[C - 1:C, :] + _dot_tn(vb, (kk_ref[slot, :, cs] * er).astype(BF16))
            ms = jnp.mean(o * o, axis=-1, keepdims=True)
            on = o * lax.rsqrt(ms + EPS) * onorm
            g = g_ref[rows, cs]
            o_ref[rows, cs] = on * (g * _sigmoid(g))
        return carry

    lax.fori_loop(0, n_chunks, chunk, 0)


def _hgrn(z, lb, onorm, B, S, sblk=1024):
    T = B * S
    sblk = min(sblk, S)
    ns = S // sblk
    lc_np, mc_np = _hgrn_consts()
    lc = jnp.asarray(lc_np, BF16)
    mc = jnp.asarray(mc_np, F32)
    cb = HG_OFF // HG_WIDTH

    def zspec(k):
        return pl.BlockSpec((sblk, HG_WIDTH), lambda b, s, k=k: (b * ns + s, cb + k))

    return pl.pallas_call(
        _hgrn_kernel,
        out_shape=jax.ShapeDtypeStruct((T, HG_WIDTH), F32),
        grid=(B, ns),
        in_specs=[
            zspec(0), zspec(1), zspec(2), zspec(3),
            pl.BlockSpec((1, HG_WIDTH), lambda b, s: (0, 0)),
            pl.BlockSpec((1, HG_DK), lambda b, s: (0, 0)),
            pl.BlockSpec(lc.shape, lambda b, s: (0, 0)),
            pl.BlockSpec(mc.shape, lambda b, s: (0, 0, 0)),
        ],
        out_specs=pl.BlockSpec((sblk, HG_WIDTH), lambda b, s: (b * ns + s, 0)),
        scratch_shapes=[
            pltpu.VMEM((HG_HEADS, HG_DK, HG_DK), F32),
            pltpu.VMEM((2, CHUNK, HG_WIDTH), F32),
            pltpu.VMEM((2, CHUNK, HG_WIDTH), F32),
            pltpu.VMEM((2, CHUNK, HG_WIDTH), BF16),
            pltpu.VMEM((2, CHUNK, HG_WIDTH), BF16),
            pltpu.VMEM(((len(_HG_LEVELS) + 2) * CHUNK, HG_WIDTH), F32),
            pltpu.VMEM((HG_HEADS, CHUNK, CHUNK), BF16),
        ],
        compiler_params=_cparams(("parallel", "arbitrary")),
        name="hgrn2",
    )(z, z, z, z, lb, onorm, lc, mc)


def _dsa_kernel(topk, sq_ref, iq_ref, iwq_ref, sc_ref, ikw_ref, cn_ref, wk_ref, wvt_ref,
                o_ref, kk_ref, vt_ref, ik_ref, key_ref, plane_ref, qt_ref, m_ref, l_ref, acc_ref):
    QB = QBLOCK
    j = pl.program_id(1)
    nkb = j + 1
    S = sc_ref.shape[0]
    nb = S // QB
    idx_bits = (S - 1).bit_length()

    @pl.when(j == 0)
    def _():
        c = sc_ref[...]
        ms = jnp.mean(c * c, axis=-1, keepdims=True)
        cb = (c * lax.rsqrt(ms + EPS) * cn_ref[...]).astype(BF16)
        kfull = _dot(cb, wk_ref[...]).astype(BF16)
        vtfull = _dot_nt(wvt_ref[...], cb).astype(BF16)
        ikfull = ikw_ref[:, 0:IDX_DIM].astype(BF16)
        for kb in range(nb):
            kk_ref[kb] = kfull[kb * QB:(kb + 1) * QB]
            vt_ref[kb] = vtfull[:, kb * QB:(kb + 1) * QB]
            ik_ref[kb] = ikfull[kb * QB:(kb + 1) * QB]

    qt_ref[...] = (sq_ref[...] * (SA_DH ** -0.5 * LOG2E)).T.astype(BF16)
    iqt = iq_ref[...].T.astype(BF16)
    iwt = iwq_ref[...].T * (IDX_HEADS ** -0.5)

    row = lax.broadcasted_iota(I32, (QB, QB), 0)
    lane = lax.broadcasted_iota(I32, (QB, QB), 1)
    diag_bad = (row >= CHUNK) & (lane < CHUNK)

    n_pairs = (nkb + 1) // 2

    def score_block(kb):
        ikb = ik_ref[kb]
        acc = jnp.zeros((QB, QB), F32)
        for h in range(IDX_HEADS):
            s_h = _dot(ikb, iqt[h * IDX_DIM:(h + 1) * IDX_DIM])
            acc = acc + jnp.maximum(s_h, 0.0) * iwt[IDX_DIM + h:IDX_DIM + h + 1, :]
        acc = acc + 0.0
        acc = jnp.where((kb == j) & diag_bad, -jnp.inf, acc)
        bits = pltpu.bitcast(acc, I32)
        key_ref[kb] = jnp.where(bits < 0, bits ^ jnp.int32(0x7FFFFFFF), bits)

    def score_quad(g, carry):
        for i in range(4):
            score_block(4 * g + i)
        return carry

    lax.fori_loop(0, nkb // 4, score_quad, 0)

    @pl.when((nkb & 2) != 0)
    def _():
        score_block(4 * (nkb // 4))
        score_block(4 * (nkb // 4) + 1)

    @pl.when((nkb & 1) != 0)
    def _():
        score_block(nkb - 1)

    @pl.when(j + 1 < 2 * n_pairs)
    def _():
        key_ref[j + 1] = jnp.full((QB, QB), INT_MIN, I32)

    def build_planes(g, carry):
        a = []
        for half in range(2):
            blk = key_ref[2 * g + half] ^ jnp.int32(INT_MIN)
            a.extend(blk[r * 8:(r + 1) * 8, :] for r in range(QB // 8))
        for sh, m in ((16, 0x0000FFFF), (8, 0x00FF00FF), (4, 0x0F0F0F0F), (2, 0x33333333),
                      (1, 0x55555555)):
            for k in range(32):
                if k & sh == 0:
                    t = (lax.shift_right_logical(a[k], jnp.int32(sh)) ^ a[k + sh]) & jnp.int32(m)
                    a[k + sh] = a[k + sh] ^ t
                    a[k] = a[k] ^ lax.shift_left(t, jnp.int32(sh))
        rows = pl.ds(pl.multiple_of(g * 8, 8), 8)
        for b in range(32):
            plane_ref[b, rows, :] = a[b]
        return carry

    def search(_):
        lax.fori_loop(0, n_pairs, build_planes, 0)
        wrow = lax.broadcasted_iota(I32, (nb // 2 * 8, QB), 0)
        cand = jnp.where((wrow >> 3) < n_pairs, jnp.int32(-1), jnp.int32(0))
        above = jnp.zeros((1, QB), I32)
        u = jnp.zeros((1, QB), I32)
        for b in range(31, -1, -1):
            ones = cand & plane_ref[b]
            cnt1 = jnp.sum(lax.population_count(ones), axis=0, keepdims=True)
            take = (above + cnt1) >= topk
            cand = jnp.where(take, ones, cand ^ ones)
            above = jnp.where(take, above, above + cnt1)
            u = u | jnp.where(take, jnp.int32(INT_MIN if b == 31 else 1 << b), jnp.int32(0))
        thr = u ^ jnp.int32(INT_MIN)
        need = topk - above
        n_tied = jnp.sum(lax.population_count(cand), axis=0, keepdims=True)

        def break_ties(_):
            base = (wrow >> 3) * (2 * QB) + (wrow & 7)

            def ibit(i, jp):
                c = jp + lax.shift_left(jnp.int32(1), idx_bits - 1 - i)
                nbits = jnp.clip((c - base + 7) >> 3, 0, 32)
                below = jnp.where(nbits >= 32, jnp.int32(-1),
                                  lax.shift_left(jnp.int32(1), nbits) - 1)
                cnt = jnp.sum(lax.population_count(cand & below), axis=0, keepdims=True)
                return jnp.where(cnt < need, c, jp)

            return lax.fori_loop(0, idx_bits, ibit, jnp.zeros((1, QB), I32))

        def keep_ties(_):
            return jnp.full((1, QB), S, I32)

        jp = lax.cond(jnp.max(n_tied - need) > 0, break_ties, keep_ties, 0)
        return thr, jp

    def take_all(_):
        return (jnp.full((1, QB), INT_MIN, I32), jnp.full((1, QB), -1, I32))

    thr, jp = lax.cond(nkb * QB > topk, search, take_all, 0)

    m_ref[...] = jnp.full(m_ref.shape, NEG_BIG, F32)
    l_ref[...] = jnp.zeros(l_ref.shape, F32)
    acc_ref[...] = jnp.zeros(acc_ref.shape, F32)

    def attend_block(kb):
        key = key_ref[kb]
        kidx = row + kb * QB
        sel = (key > thr) | ((key == thr) & (kidx <= jp))
        hidden = (kidx >= j * QB + CHUNK) & (lane < CHUNK)
        bias = jnp.where(sel & jnp.logical_not(hidden), 0.0, -jnp.inf)
        kblk = kk_ref[kb]
        vtb = vt_ref[kb]
        for h in range(SA_HEADS):
            lg = _dot(kblk, qt_ref[h * SA_DH:(h + 1) * SA_DH, :]) + bias
            m_old = m_ref[h:h + 1, :]
            m_new = jnp.maximum(m_old, jnp.max(lg, axis=0, keepdims=True))
            alpha = jnp.exp2(m_old - m_new)
            p = jnp.exp2(lg - m_new)
            l_ref[h:h + 1, :] = alpha * l_ref[h:h + 1, :] + jnp.sum(p, axis=0, keepdims=True)
            acc_ref[h] = acc_ref[h] * alpha + _dot(vtb, p.astype(BF16))
            m_ref[h:h + 1, :] = m_new

    def attend(g, carry):
        for i in range(8):
            attend_block(8 * g + i)
        return carry

    n_oct = nkb // 8
    lax.fori_loop(0, n_oct, attend, 0)

    @pl.when((nkb & 4) != 0)
    def _():
        for i in range(4):
            attend_block(8 * n_oct + i)

    n_quads = nkb // 4

    @pl.when((nkb & 2) != 0)
    def _():
        attend_block(4 * n_quads)
        attend_block(4 * n_quads + 1)

    @pl.when((nkb & 1) != 0)
    def _():
        attend_block(nkb - 1)

    outs = []
    for h in range(SA_HEADS):
        outs.append(acc_ref[h] * (1.0 / l_ref[h:h + 1, :]))
    o_ref[...] = jnp.concatenate(outs, axis=0).T


def _dsa(z, cnorm, wk, wvt, B, S):
    T = B * S
    nb = S // QBLOCK
    topk = min(TOPK_MAX, S // 4)
    assert nb % 2 == 0
    kern = functools.partial(_dsa_kernel, topk)
    return pl.pallas_call(
        kern,
        out_shape=jax.ShapeDtypeStruct((T, SA_WIDTH), F32),
        grid=(B, nb),
        in_specs=[
            pl.BlockSpec((QBLOCK, SA_WIDTH), lambda b, j: (b * nb + j, SQ_OFF // SA_WIDTH)),
            pl.BlockSpec((QBLOCK, IDX_HEADS * IDX_DIM),
                         lambda b, j: (b * nb + j, IQ_OFF // (IDX_HEADS * IDX_DIM))),
            pl.BlockSpec((QBLOCK, LANES), lambda b, j: (b * nb + j, IKW_OFF // LANES)),
            pl.BlockSpec((S, SA_LATENT), lambda b, j: (b, SC_OFF // SA_LATENT)),
            pl.BlockSpec((S, LANES), lambda b, j: (b, IKW_OFF // LANES)),
            pl.BlockSpec((1, SA_LATENT), lambda b, j: (0, 0)),
            pl.BlockSpec((SA_LATENT, SA_DH), lambda b, j: (0, 0)),
            pl.BlockSpec((SA_DH, SA_LATENT), lambda b, j: (0, 0)),
        ],
        out_specs=pl.BlockSpec((QBLOCK, SA_WIDTH), lambda b, j: (b * nb + j, 0)),
        scratch_shapes=[
            pltpu.VMEM((nb, QBLOCK, SA_DH), BF16),
            pltpu.VMEM((nb, SA_DH, QBLOCK), BF16),
            pltpu.VMEM((nb, QBLOCK, IDX_DIM), BF16),
            pltpu.VMEM((nb, QBLOCK, QBLOCK), I32),
            pltpu.VMEM((32, nb // 2 * 8, QBLOCK), I32),
            pltpu.VMEM((SA_WIDTH, QBLOCK), BF16),
            pltpu.VMEM((SA_HEADS, QBLOCK), F32),
            pltpu.VMEM((SA_HEADS, QBLOCK), F32),
            pltpu.VMEM((SA_HEADS, SA_DH, QBLOCK), F32),
        ],
        compiler_params=_cparams(("parallel", "arbitrary")),
        name="dsa",
    )(z, z, z, z, z, cnorm, wk, wvt)


def _sb_kernel(q_ref, k_ref, v_ref, uo_ref, o_ref, kt_ref, vb_ref, run_ref, acc_ref,
               ls_ref, lkh_ref, lkl_ref, tot_ref, a_ref):
    QB = QBLOCK
    j = pl.program_id(1)
    S = k_ref.shape[0]
    nb = S // QB
    npair = SB_HEADS // 2
    lane = lax.broadcasted_iota(I32, (QB, QB), 1)
    row = lax.broadcasted_iota(I32, (QB, QB), 0)

    @pl.when(j == 0)
    def _():
        for p in range(npair):
            cols = slice(p * LANES, (p + 1) * LANES)
            kt = k_ref[:, cols].T
            vv = v_ref[:, cols]
            for hh in range(2):
                lo = hh * SB_DH
                ktz = jnp.where((row >= lo) & (row < lo + SB_DH), 1.0, 0.0)
                vz = jnp.where((lane >= lo) & (lane < lo + SB_DH), 1.0, 0.0)
                for kb in range(nb):
                    ks = slice(kb * QB, (kb + 1) * QB)
                    kt_ref[kb, 2 * p + hh] = (kt[:, ks] * ktz).astype(BF16)
                    vb_ref[kb, 2 * p + hh] = (vv[ks, :] * vz).astype(BF16)

    uo = uo_ref[...]
    strict = lane < row
    qs = (q_ref[...] * (SB_DH ** -0.5)).astype(BF16)

    def all_heads(kbs, diag):
        nk = len(kbs)
        for h in range(SB_HEADS):
            p = h // 2
            for i, kb in enumerate(kbs):
                s = h * nk + i
                z = _dot(qs[:, p * LANES:(p + 1) * LANES], kt_ref[kb, h])
                l1p = jnp.log(1.0 + jnp.exp(-jnp.abs(z)))
                ls_pos = jnp.minimum(z, 0.0) - l1p
                lk = ls_pos - z
                if diag and i == 0:
                    lk = jnp.where(strict, lk, 0.0)
                lk_hi, lk_lo = _split_bf16(lk)
                ls_ref[s] = ls_pos
                lkh_ref[s] = lk_hi
                lkl_ref[s] = lk_lo
                tot_ref[s] = jnp.sum(lk, axis=1, keepdims=True)
        top = None
        for h in range(SB_HEADS):
            run = None if diag else run_ref[h]
            for i in range(nk):
                s = h * nk + i
                after = _dot(lkh_ref[s], uo) + _dot(lkl_ref[s], uo)
                if diag and i == 0:
                    a = jnp.where(strict, jnp.exp(ls_ref[s] + after), 0.0)
                    run = tot_ref[s]
                else:
                    a = jnp.exp(ls_ref[s] + after + run)
                    run = run + tot_ref[s]
                a_ref[s] = a.astype(BF16)
            run_ref[h] = run
            top = run if top is None else jnp.maximum(top, run)
        for p in range(npair):
            pv = None
            for h in (2 * p, 2 * p + 1):
                for i, kb in enumerate(kbs):
                    out = _dot(a_ref[h * nk + i], vb_ref[kb, h])
                    pv = out if pv is None else pv + out
            if diag:
                acc_ref[p] = pv
            else:
                acc_ref[p] += pv
        return jnp.max(top) > SB_UNDERFLOW

    n_first = 3
    alive = lax.cond(
        j >= 2, lambda: all_heads([j, j - 1, j - 2], True),
        lambda: lax.cond(j == 1, lambda: all_heads([j, j - 1], True), lambda: all_heads([j], True)))

    def cond(c):
        return (c[0] >= 1) & c[1]

    def body(c):
        return c[0] - 2, all_heads([c[0], c[0] - 1], False)

    kb_left, alive = lax.while_loop(cond, body, (j - n_first, alive))

    @pl.when((kb_left == 0) & alive)
    def _():
        all_heads([0], False)

    o_ref[...] = jnp.concatenate([acc_ref[p] for p in range(npair)], axis=1)


def _sb(z, B, S):
    T = B * S
    nb = S // QBLOCK
    u = np.tril(np.ones((QBLOCK, QBLOCK), np.float32), -1)
    uo = jnp.asarray(u, BF16)
    qoff = SB_OFF // SB_WIDTH
    return pl.pallas_call(
        _sb_kernel,
        out_shape=jax.ShapeDtypeStruct((T, SB_WIDTH), F32),
        grid=(B, nb),
        in_specs=[
            pl.BlockSpec((QBLOCK, SB_WIDTH), lambda b, j: (b * nb + j, qoff)),
            pl.BlockSpec((S, SB_WIDTH), lambda b, j: (b, qoff + 1)),
            pl.BlockSpec((S, SB_WIDTH), lambda b, j: (b, qoff + 2)),
            pl.BlockSpec(uo.shape, lambda b, j: (0, 0)),
        ],
        out_specs=pl.BlockSpec((QBLOCK, SB_WIDTH), lambda b, j: (b * nb + j, 0)),
        scratch_shapes=[
            pltpu.VMEM((nb, SB_HEADS, LANES, QBLOCK), BF16),
            pltpu.VMEM((nb, SB_HEADS, QBLOCK, LANES), BF16),
            pltpu.VMEM((SB_HEADS, QBLOCK, 1), F32),
            pltpu.VMEM((SB_HEADS // 2, QBLOCK, LANES), F32),
            pltpu.VMEM((3 * SB_HEADS, QBLOCK, QBLOCK), F32),
            pltpu.VMEM((3 * SB_HEADS, QBLOCK, QBLOCK), BF16),
            pltpu.VMEM((3 * SB_HEADS, QBLOCK, QBLOCK), BF16),
            pltpu.VMEM((3 * SB_HEADS, QBLOCK, 1), F32),
            pltpu.VMEM((3 * SB_HEADS, QBLOCK, QBLOCK), BF16),
        ],
        compiler_params=_cparams(("parallel", "arbitrary")),
        name="stickbreak",
    )(z, z, z, uo)


def _merge_kernel(a_ref, b_ref, c_ref, u_ref, h_ref, wgz_ref, wb_ref, wo_ref, gn_ref, ho_ref, v_ref):
    u = u_ref[...]
    merged = None
    for n, br in enumerate((a_ref, b_ref, c_ref)):
        proj = _dot(br[...].astype(BF16), wb_ref[n])
        gate = _sigmoid(_dot(u, wgz_ref[:, n * D_MODEL:(n + 1) * D_MODEL]))
        merged = gate * proj if merged is None else merged + gate * proj
    hn = h_ref[...] + _dot(merged.astype(BF16), wo_ref[...])
    ho_ref[...] = hn
    ms = jnp.mean(hn * hn, axis=-1, keepdims=True)
    v_ref[...] = (hn * lax.rsqrt(ms + EPS) * gn_ref[...]).astype(BF16)


def _merge(a, b, c, u, h, wgz, wb, wo, gn, tm=256):
    T, D = h.shape
    bw = BRANCH_WIDTH
    return pl.pallas_call(
        _merge_kernel,
        out_shape=(jax.ShapeDtypeStruct((T, D), F32), jax.ShapeDtypeStruct((T, D), BF16)),
        grid=(T // tm,),
        in_specs=[
            pl.BlockSpec((tm, bw), lambda i: (i, 0)),
            pl.BlockSpec((tm, bw), lambda i: (i, 0)),
            pl.BlockSpec((tm, bw), lambda i: (i, 0)),
            pl.BlockSpec((tm, D), lambda i: (i, 0)),
            pl.BlockSpec((tm, D), lambda i: (i, 0)),
            pl.BlockSpec((D, GZW), lambda i: (0, 0)),
            pl.BlockSpec((N_BRANCH, bw, D), lambda i: (0, 0, 0)),
            pl.BlockSpec((D, D), lambda i: (0, 0)),
            pl.BlockSpec((1, D), lambda i: (0, 0)),
        ],
        out_specs=(pl.BlockSpec((tm, D), lambda i: (i, 0)), pl.BlockSpec((tm, D), lambda i: (i, 0))),
        compiler_params=_cparams(("parallel",)),
        name="merge",
    )(a, b, c, u, h, wgz, wb, wo, gn)


def _swiglu_kernel(v_ref, h_ref, wg_ref, wu_ref, wd_ref, o_ref, acc_ref):
    f = pl.program_id(1)

    @pl.when(f == 0)
    def _():
        acc_ref[...] = h_ref[...]

    v = v_ref[...]
    gt = _dot(v, wg_ref[...])
    up = _dot(v, wu_ref[...])
    act = (gt * _sigmoid(gt) * up).astype(BF16)
    acc_ref[...] += _dot(act, wd_ref[...])

    @pl.when(f == pl.num_programs(1) - 1)
    def _():
        o_ref[...] = acc_ref[...]


def _swiglu(v, h, wg, wu, wd, tm=512, tf=1408):
    T, D = h.shape
    F = wg.shape[1]
    return pl.pallas_call(
        _swiglu_kernel,
        out_shape=jax.ShapeDtypeStruct((T, D), F32),
        grid=(T // tm, F // tf),
        in_specs=[
            pl.BlockSpec((tm, D), lambda i, f: (i, 0)),
            pl.BlockSpec((tm, D), lambda i, f: (i, 0)),
            pl.BlockSpec((D, tf), lambda i, f: (0, f)),
            pl.BlockSpec((D, tf), lambda i, f: (0, f)),
            pl.BlockSpec((tf, D), lambda i, f: (f, 0)),
        ],
        out_specs=pl.BlockSpec((tm, D), lambda i, f: (i, 0)),
        scratch_shapes=[pltpu.VMEM((tm, D), F32)],
        compiler_params=_cparams(("parallel", "arbitrary")),
        name="swiglu",
    )(v, h, wg, wu, wd)


MOE_TILE = 1024
MOE_CHUNK = 288


def _router_kernel(h_ref, gn_ref, r_ref, us_ref, cw_ref, cwt_ref, rkt_ref):
    hh = h_ref[...]
    ms = jnp.mean(hh * hh, axis=-1, keepdims=True)
    vf = hh * lax.rsqrt(ms + EPS) * gn_ref[...]
    v_hi, v_lo = _split_bf16(vf)
    r_hi, r_lo = _split_bf16(r_ref[...])
    logits = _dot(v_hi, r_hi) + _dot(v_hi, r_lo) + _dot(v_lo, r_hi)
    lane = lax.broadcasted_iota(I32, logits.shape, 1)
    logits = jnp.where(lane < N_EXPERTS, logits, -jnp.inf)
    m1 = jnp.max(logits, axis=-1, keepdims=True)
    i1 = jnp.min(jnp.where(logits == m1, lane, LANES), axis=-1, keepdims=True)
    rest = jnp.where(lane == i1, -jnp.inf, logits)
    m2 = jnp.max(rest, axis=-1, keepdims=True)
    i2 = jnp.min(jnp.where(rest == m2, lane, LANES), axis=-1, keepdims=True)
    e2 = jnp.exp(m2 - m1)
    w1 = 1.0 / (1.0 + e2)
    cw = jnp.where(lane == i1, w1, 0.0) + jnp.where(lane == i2, e2 * w1, 0.0)
    cw_ref[...] = cw
    cwt = cw.T
    cwt_ref[...] = cwt[0:N_EXPERTS]
    sel = jnp.where(cwt != 0.0, 1.0, 0.0).astype(BF16)
    rkt_ref[...] = _dot(sel, us_ref[...])[0:N_EXPERTS]


def _router(h, gn, router, tm=MOE_TILE):
    T, D = h.shape
    us = jnp.asarray(np.triu(np.ones((tm, tm), np.float32), 1), BF16)
    return pl.pallas_call(
        _router_kernel,
        out_shape=(jax.ShapeDtypeStruct((T, LANES), F32),
                   jax.ShapeDtypeStruct((N_EXPERTS, T), F32),
                   jax.ShapeDtypeStruct((N_EXPERTS, T), F32)),
        grid=(T // tm,),
        in_specs=[
            pl.BlockSpec((tm, D), lambda i: (i, 0)),
            pl.BlockSpec((1, D), lambda i: (0, 0)),
            pl.BlockSpec((D, LANES), lambda i: (0, 0)),
            pl.BlockSpec((tm, tm), lambda i: (0, 0)),
        ],
        out_specs=(pl.BlockSpec((tm, LANES), lambda i: (i, 0)),
                   pl.BlockSpec((N_EXPERTS, tm), lambda i: (0, i)),
                   pl.BlockSpec((N_EXPERTS, tm), lambda i: (0, i))),
        compiler_params=_cparams(("parallel",)),
        name="router",
    )(h, gn, router, us)


MOE_SUB = 256
MOE_WIN = 112


def _moe_kernel(v_ref, h_ref, cw_ref, cwt_ref, rkt_ref, wg_ref, wu_ref, wd_ref, o_ref,
                xg_ref, ce_ref, ya_ref):
    e = pl.program_id(1)
    f = pl.program_id(2)
    nf = pl.num_programs(2)
    tm = v_ref.shape[0]
    CH, SUB, WIN = MOE_CHUNK, MOE_SUB, MOE_WIN
    nsub = tm // SUB

    @pl.when((e == 0) & (f == 0))
    def _():
        o_ref[...] = h_ref[...]

    ce_row = cwt_ref[pl.ds(e, 1), :]
    routed = ce_row != 0.0
    rank_row = rkt_ref[pl.ds(e, 1), :]
    slot_row = jnp.where(routed, rank_row, -1.0)
    count = jnp.sum(jnp.where(routed, 1, 0).astype(I32))
    n_chunks = (count + (CH - 1)) // CH
    bounds = [jnp.int32(0)]
    for s in range(1, nsub):
        bounds.append(rank_row[0, s * SUB].astype(I32))
    bounds.append(count)
    wiota = lax.broadcasted_iota(I32, (WIN, SUB), 0).astype(F32)

    def for_windows(fn):
        def window(s, r0):
            slots = slot_row[:, s * SUB:(s + 1) * SUB]
            pick = jnp.where(slots == wiota + r0.astype(F32), 1.0, 0.0).astype(BF16)
            fn(s, r0, pick)

        starts = [pl.multiple_of((bounds[s] // 16) * 16, 16) for s in range(nsub)]
        for s in range(nsub):
            window(s, starts[s])
        for s in range(nsub):
            nw = (bounds[s + 1] - starts[s] + (WIN - 1)) // WIN

            def body(m, carry, s=s):
                window(s, pl.multiple_of(starts[s] + m * WIN, 16))
                return carry

            lax.fori_loop(1, nw, body, 0)

    @pl.when(f == 0)
    def _():
        def clear(i, carry):
            rows = pl.ds(pl.multiple_of(i * CH, 16), CH)
            xg_ref[rows, :] = jnp.zeros((CH, xg_ref.shape[1]), BF16)
            ce_ref[rows, :] = jnp.zeros((CH, LANES), F32)
            ya_ref[rows, :] = jnp.zeros((CH, ya_ref.shape[1]), F32)
            return carry

        lax.fori_loop(0, jnp.minimum(n_chunks + 1, xg_ref.shape[0] // CH), clear, 0)
        lane = lax.broadcasted_iota(I32, (WIN, LANES), 1)

        def gather(s, r0, pick):
            toks = slice(s * SUB, (s + 1) * SUB)
            rows = pl.ds(r0, WIN)
            got = _dot(pick, v_ref[toks, :])
            xg_ref[rows, :] = (xg_ref[rows, :].astype(F32) + got).astype(BF16)
            c_hi, c_lo = _split_bf16(cw_ref[toks, :])
            cg = _dot(pick, c_hi) + _dot(pick, c_lo)
            ce = jnp.sum(jnp.where(lane == e, cg, 0.0), axis=-1, keepdims=True)
            ce_ref[rows, :] += jnp.broadcast_to(ce, (WIN, LANES))

        for_windows(gather)

    def expert(i, carry):
        rows = pl.ds(pl.multiple_of(i * CH, 16), CH)
        xg = xg_ref[rows, :]
        gt = _dot(xg, wg_ref[0])
        up = _dot(xg, wu_ref[0])
        act = (gt * _sigmoid(gt) * up * ce_ref[rows, 0:1]).astype(BF16)
        ya_ref[rows, :] += _dot(act, wd_ref[0])
        return carry

    lax.fori_loop(0, n_chunks, expert, 0)

    @pl.when(f == nf - 1)
    def _():
        def scatter(s, r0, pick):
            ye = ya_ref[pl.ds(r0, WIN), :].astype(BF16)
            o_ref[s * SUB:(s + 1) * SUB, :] += _dot_tn(pick, ye)

        for_windows(scatter)


def _moe(v, h, cw, cwt, rkt, wg, wu, wd, tm=MOE_TILE, fsplit=2):
    T, D = h.shape
    E, _, F = wg.shape
    tf = F // fsplit
    rows = (-(-tm // MOE_CHUNK)) * MOE_CHUNK
    assert rows >= tm + MOE_WIN and tm % MOE_SUB == 0
    return pl.pallas_call(
        _moe_kernel,
        out_shape=jax.ShapeDtypeStruct((T, D), F32),
        grid=(T // tm, E, fsplit),
        in_specs=[
            pl.BlockSpec((tm, D), lambda i, e, f: (i, 0), pipeline_mode=pl.Buffered(1)),
            pl.BlockSpec((tm, D), lambda i, e, f: (i, 0), pipeline_mode=pl.Buffered(1)),
            pl.BlockSpec((tm, LANES), lambda i, e, f: (i, 0), pipeline_mode=pl.Buffered(1)),
            pl.BlockSpec((N_EXPERTS, tm), lambda i, e, f: (0, i)),
            pl.BlockSpec((N_EXPERTS, tm), lambda i, e, f: (0, i)),
            pl.BlockSpec((1, D, tf), lambda i, e, f: (e, 0, f)),
            pl.BlockSpec((1, D, tf), lambda i, e, f: (e, 0, f)),
            pl.BlockSpec((1, tf, D), lambda i, e, f: (e, f, 0)),
        ],
        out_specs=pl.BlockSpec((tm, D), lambda i, e, f: (i, 0)),
        scratch_shapes=[
            pltpu.VMEM((rows, D), BF16),
            pltpu.VMEM((rows, LANES), F32),
            pltpu.VMEM((rows, D), F32),
        ],
        compiler_params=_cparams(("parallel", "arbitrary", "arbitrary")),
        name="moe",
    )(v, h, cw, cwt, rkt, wg, wu, wd)


def _ple_kernel(final, h_ref, p_ref, gn_ref, wg_ref, wp_ref, fn_ref, o_ref):
    h = h_ref[...]
    ms = jnp.mean(h * h, axis=-1, keepdims=True)
    u = (h * lax.rsqrt(ms + EPS) * gn_ref[...]).astype(BF16)
    gate = _sigmoid(_dot(u, wg_ref[...]))
    hn = h + gate * _dot(p_ref[0].astype(BF16), wp_ref[...])
    if final:
        ms = jnp.mean(hn * hn, axis=-1, keepdims=True)
        hn = hn * lax.rsqrt(ms + EPS) * fn_ref[...]
    o_ref[...] = hn


def _ple(h, p, layer, gn, wg, wp, fn, final, tm=512):
    T, D = h.shape
    return pl.pallas_call(
        functools.partial(_ple_kernel, final),
        out_shape=jax.ShapeDtypeStruct((T, D), F32),
        grid=(T // tm,),
        in_specs=[
            pl.BlockSpec((tm, D), lambda i: (i, 0)),
            pl.BlockSpec((1, tm, PLE_DIM), lambda i: (layer, i, 0)),
            pl.BlockSpec((1, D), lambda i: (0, 0)),
            pl.BlockSpec((D, D), lambda i: (0, 0)),
            pl.BlockSpec((PLE_DIM, D), lambda i: (0, 0)),
            pl.BlockSpec((1, D), lambda i: (0, 0)),
        ],
        out_specs=pl.BlockSpec((tm, D), lambda i: (i, 0)),
        compiler_params=_cparams(("parallel",)),
        name="ple",
    )(h, p, gn, wg, wp, fn)


def _reorder_kernel(w_ref, o_ref, ogz_ref):
    o_sq = 4 * HG_WIDTH
    o_sc = o_sq + SA_WIDTH
    o_iq = o_sc + SA_LATENT
    o_ik = o_iq + IDX_HEADS * IDX_DIM
    o_bq = o_ik + IDX_DIM + IDX_HEADS
    o_gz = o_bq + 3 * SB_WIDTH

    def put(dst, lo, hi):
        o_ref[:, dst:dst + (hi - lo)] = w_ref[0, :, lo:hi].astype(BF16)

    ogz_ref[...] = w_ref[0, :, o_gz:o_gz + GZW].astype(BF16)
    put(HG_OFF, 0, o_sq)
    put(SQ_OFF, o_sq, o_sc)
    put(IQ_OFF, o_iq, o_ik)
    put(SC_OFF, o_sc, o_iq)
    put(IKW_OFF, o_ik, o_bq)
    used = IDX_DIM + IDX_HEADS
    o_ref[:, IKW_OFF + used:SB_OFF] = jnp.zeros((o_ref.shape[0], LANES - used), BF16)
    put(SB_OFF, o_bq, o_gz)


def _reorder_w_in(w_in, layer, tr=256):
    _, R, C = w_in.shape
    return pl.pallas_call(
        _reorder_kernel,
        out_shape=(jax.ShapeDtypeStruct((R, ZW), BF16), jax.ShapeDtypeStruct((R, GZW), BF16)),
        grid=(R // tr,),
        in_specs=[pl.BlockSpec((1, tr, C), lambda i: (layer, i, 0))],
        out_specs=(pl.BlockSpec((tr, ZW), lambda i: (i, 0)), pl.BlockSpec((tr, GZW), lambda i: (i, 0))),
        compiler_params=_cparams(("parallel",)),
        name="reorder_w_in",
    )(w_in)


def kernel(x, p, lb_param, norm_mix, w_in, hg_onorm, sa_cnorm, sa_wk, sa_wv, w_branch, w_out,
           norm_ffn, dense_wg, dense_wu, dense_wd, moe_router, moe_wg, moe_wu, moe_wd, norm_ple,
           ple_gate, ple_proj, norm_final):
    B, S, D = x.shape
    T = B * S
    depth = w_in.shape[0]
    lb_sm = jax.nn.softmax(lb_param.astype(F32), axis=0)
    lb_all = jnp.cumsum(lb_sm, axis=0) - lb_sm[0:1]
    h = x.reshape(T, D)
    for i in range(depth):
        w_mix, w_gz = _reorder_w_in(w_in, i)
        z, u = _inproj(h, norm_mix[i][None, :], w_mix)
        a = _hgrn(z, lb_all[i][None, :], hg_onorm[i][None, :], B, S)
        b = _dsa(z, sa_cnorm[i][None, :], sa_wk[i].astype(BF16), sa_wv[i].T.astype(BF16), B, S)
        c = _sb(z, B, S)
        h, v = _merge(a, b, c, u, h, w_gz, w_branch[i].astype(BF16), w_out[i].astype(BF16),
                      norm_ffn[i][None, :])
        jj = i // 2
        if i % 2 == 0:
            h = _swiglu(v, h, dense_wg[jj].astype(BF16), dense_wu[jj].astype(BF16),
                        dense_wd[jj].astype(BF16))
        else:
            router = jnp.pad(moe_router[jj], ((0, 0), (0, LANES - N_EXPERTS)))
            cw, cwt, rkt = _router(h, norm_ffn[i][None, :], router)
            h = _moe(v, h, cw, cwt, rkt, moe_wg[jj].astype(BF16), moe_wu[jj].astype(BF16),
                     moe_wd[jj].astype(BF16))
        h = _ple(h, p.reshape(depth, T, PLE_DIM), i, norm_ple[i][None, :],
                 ple_gate[i].astype(BF16), ple_proj[i].astype(BF16), norm_final[None, :],
                 i == depth - 1)
    return h.reshape(B, S, D)
```

```python
import functools

import numpy as np
import jax
import jax.numpy as jnp
from jax import lax
from jax.experimental import pallas as pl
from jax.experimental.pallas import tpu as pltpu

F32 = jnp.float32
BF16 = jnp.bfloat16
I32 = jnp.int32

D_MODEL = 1024
CHUNK = 64
QBLOCK = 128
HG_HEADS = 4
HG_DK = 128
HG_WIDTH = HG_HEADS * HG_DK
SA_HEADS = 8
SA_DH = 64
SA_WIDTH = SA_HEADS * SA_DH
SA_LATENT = 128
IDX_HEADS = 4
IDX_DIM = 64
TOPK_MAX = 256
SB_HEADS = 8
SB_DH = 64
SB_WIDTH = SB_HEADS * SB_DH
N_BRANCH = 3
BRANCH_WIDTH = 512
N_EXPERTS = 8
PLE_DIM = 256
EPS = 1e-6

LANES = 128

HG_OFF = 0
SQ_OFF = HG_OFF + 4 * HG_WIDTH
IQ_OFF = SQ_OFF + SA_WIDTH
SC_OFF = IQ_OFF + IDX_HEADS * IDX_DIM
IKW_OFF = SC_OFF + SA_LATENT
SB_OFF = IKW_OFF + LANES
ZW = SB_OFF + 3 * SB_WIDTH
GZW = N_BRANCH * D_MODEL

VMEM_LIMIT = 56 * 1024 * 1024

NEG_BIG = -1e30
INT_MIN = -2147483648
LOG2E = 1.4426950408889634
SB_UNDERFLOW = -104.0


def _cparams(sem):
    return pltpu.CompilerParams(dimension_semantics=sem, vmem_limit_bytes=VMEM_LIMIT)


def _sigmoid(x):
    e = jnp.exp(-jnp.abs(x))
    inv = 1.0 / (1.0 + e)
    return jnp.where(x >= 0, inv, e * inv)


def _dot(a, b):
    return jnp.dot(a, b, preferred_element_type=F32)


def _dot_nt(a, b):
    return lax.dot_general(a, b, (((1,), (1,)), ((), ())), preferred_element_type=F32)


def _dot_tn(a, b):
    return lax.dot_general(a, b, (((0,), (0,)), ((), ())), preferred_element_type=F32)


def _split_bf16(x):
    hi = x.astype(BF16)
    lo = (x - hi.astype(F32)).astype(BF16)
    return hi, lo


def _inproj_kernel(x_ref, g_ref, w_ref, o_ref, u_ref):
    @pl.when(pl.program_id(1) == 0)
    def _():
        x = x_ref[...]
        ms = jnp.mean(x * x, axis=-1, keepdims=True)
        u_ref[...] = (x * lax.rsqrt(ms + EPS) * g_ref[...]).astype(BF16)

    o_ref[...] = _dot(u_ref[...], w_ref[...])


def _inproj(h, gain, w, tm=512, tn=ZW):
    T, D = h.shape
    N = w.shape[1]
    return pl.pallas_call(
        _inproj_kernel,
        out_shape=(jax.ShapeDtypeStruct((T, N), F32), jax.ShapeDtypeStruct((T, D), BF16)),
        grid=(T // tm, N // tn),
        in_specs=[
            pl.BlockSpec((tm, D), lambda i, j: (i, 0)),
            pl.BlockSpec((1, D), lambda i, j: (0, 0)),
            pl.BlockSpec((D, tn), lambda i, j: (0, j)),
        ],
        out_specs=(pl.BlockSpec((tm, tn), lambda i, j: (i, j)),
                   pl.BlockSpec((tm, D), lambda i, j: (i, 0))),
        compiler_params=_cparams(("parallel", "arbitrary")),
        name="inproj",
    )(h, gain, w)


_HG_LEVELS = (64, 32, 16, 8, 4, 2)


def _hgrn_consts():
    C = CHUNK
    blocks, masks = [], []
    for n in _HG_LEVELS:
        half = n // 2
        L = np.zeros((C, C), np.float32)
        M = np.zeros((C, C), np.float32)
        for t in range(C):
            base = (t // n) * n
            mid = base + half
            if t >= mid:
                L[t, mid:t + 1] = 1.0
                M[t, base:mid] = 1.0
            else:
                L[t, t + 1:mid] = 1.0
        blocks.append(L)
        masks.append(M)
    masks.append(np.eye(C, dtype=np.float32))
    blocks.append(np.tril(np.ones((C, C), np.float32)))
    blocks.append(np.triu(np.ones((C, C), np.float32), 1))
    return np.concatenate(blocks, 0), np.stack(masks)


def _hgrn_kernel(q_ref, f_ref, i_ref, g_ref, lb_ref, on_ref, lc_ref, mc_ref, o_ref, st_ref,
                 qq_ref, kk_ref, lfh_ref, lfl_ref, ex_ref, sc_ref):
    C = CHUNK
    nlev = len(_HG_LEVELS)

    @pl.when(pl.program_id(1) == 0)
    def _():
        st_ref[...] = jnp.zeros(st_ref.shape, F32)

    lc = lc_ref[...]
    onorm = on_ref[...]
    n_chunks = q_ref.shape[0] // C

    def gates(c, slot):
        rows = pl.ds(pl.multiple_of(c * C, C), C)
        fpre = f_ref[rows, :]
        lb = lb_ref[...]
        e = jnp.exp(-jnp.abs(fpre))
        inv = 1.0 / (1.0 + e)
        sg_pos = jnp.where(fpre >= 0, inv, e * inv)
        sg_neg = jnp.where(fpre >= 0, e * inv, inv)
        lf = jnp.log(lb + (1.0 - lb) * sg_pos)
        kk_ref[slot] = (1.0 - lb) * sg_neg
        qin = q_ref[rows, :]
        qq_ref[slot] = qin * _sigmoid(qin)
        lf_hi, lf_lo = _split_bf16(lf)
        lfh_ref[slot] = lf_hi
        lfl_ref[slot] = lf_lo

    gates(0, 0)

    def chunk(c, carry):
        r0 = pl.multiple_of(c * C, C)
        rows = pl.ds(r0, C)
        slot = c % 2
        ex_ref[...] = jnp.exp(_dot(lc, lfh_ref[slot]) + _dot(lc, lfl_ref[slot]))
        gates(jnp.minimum(c + 1, n_chunks - 1), 1 - slot)
        for h in range(HG_HEADS):
            cs = slice(h * HG_DK, (h + 1) * HG_DK)
            q = qq_ref[slot, :, cs]
            k = kk_ref[slot, :, cs]
            scores = _dot_nt(q.astype(BF16), k.astype(BF16)) * mc_ref[nlev]
            for l in range(nlev):
                el = ex_ref[l * C:(l + 1) * C, cs]
                s_l = _dot_nt((q * el).astype(BF16), (k * el).astype(BF16))
                scores = scores + s_l * mc_ref[l]
            sc_ref[h] = scores.astype(BF16)
        for h in range(HG_HEADS):
            cs = slice(h * HG_DK, (h + 1) * HG_DK)
            eb = ex_ref[nlev * C:(nlev + 1) * C, cs]
            er = ex_ref[(nlev + 1) * C:(nlev + 2) * C, cs]
            st = st_ref[h]
            vb = i_ref[rows, cs].astype(BF16)
            o = _dot(sc_ref[h], vb) + \
                _dot_nt((qq_ref[slot, :, cs] * eb).astype(BF16), st.astype(BF16))
            st_ref[h] = st * eb[C - 1:C, :] + _dot_tn(vb, (kk_ref[slot, :, cs] * er).astype(BF16))
            ms = jnp.mean(o * o, axis=-1, keepdims=True)
            on = o * lax.rsqrt(ms + EPS) * onorm
            g = g_ref[rows, cs]
            o_ref[rows, cs] = on * (g * _sigmoid(g))
        return carry

    lax.fori_loop(0, n_chunks, chunk, 0)


def _hgrn(z, lb, onorm, B, S, sblk=1024):
    T = B * S
    sblk = min(sblk, S)
    ns = S // sblk
    lc_np, mc_np = _hgrn_consts()
    lc = jnp.asarray(lc_np, BF16)
    mc = jnp.asarray(mc_np, F32)
    cb = HG_OFF // HG_WIDTH

    def zspec(k):
        return pl.BlockSpec((sblk, HG_WIDTH), lambda b, s, k=k: (b * ns + s, cb + k))

    return pl.pallas_call(
        _hgrn_kernel,
        out_shape=jax.ShapeDtypeStruct((T, HG_WIDTH), F32),
        grid=(B, ns),
        in_specs=[
            zspec(0), zspec(1), zspec(2), zspec(3),
            pl.BlockSpec((1, HG_WIDTH), lambda b, s: (0, 0)),
            pl.BlockSpec((1, HG_DK), lambda b, s: (0, 0)),
            pl.BlockSpec(lc.shape, lambda b, s: (0, 0)),
            pl.BlockSpec(mc.shape, lambda b, s: (0, 0, 0)),
        ],
        out_specs=pl.BlockSpec((sblk, HG_WIDTH), lambda b, s: (b * ns + s, 0)),
        scratch_shapes=[
            pltpu.VMEM((HG_HEADS, HG_DK, HG_DK), F32),
            pltpu.VMEM((2, CHUNK, HG_WIDTH), F32),
            pltpu.VMEM((2, CHUNK, HG_WIDTH), F32),
            pltpu.VMEM((2, CHUNK, HG_WIDTH), BF16),
            pltpu.VMEM((2, CHUNK, HG_WIDTH), BF16),
            pltpu.VMEM(((len(_HG_LEVELS) + 2) * CHUNK, HG_WIDTH), F32),
            pltpu.VMEM((HG_HEADS, CHUNK, CHUNK), BF16),
        ],
        compiler_params=_cparams(("parallel", "arbitrary")),
        name="hgrn2",
    )(z, z, z, z, lb, onorm, lc, mc)


def _dsa_kernel(topk, sq_ref, iq_ref, iwq_ref, sc_ref, ikw_ref, cn_ref, wk_ref, wvt_ref,
                o_ref, kk_ref, vt_ref, ik_ref, key_ref, plane_ref, qt_ref, m_ref, l_ref, acc_ref):
    QB = QBLOCK
    j = pl.program_id(1)
    nkb = j + 1
    S = sc_ref.shape[0]
    nb = S // QB
    idx_bits = (S - 1).bit_length()

    @pl.when(j == 0)
    def _():
        c = sc_ref[...]
        ms = jnp.mean(c * c, axis=-1, keepdims=True)
        cb = (c * lax.rsqrt(ms + EPS) * cn_ref[...]).astype(BF16)
        kfull = _dot(cb, wk_ref[...]).astype(BF16)
        vtfull = _dot_nt(wvt_ref[...], cb).astype(BF16)
        ikfull = ikw_ref[:, 0:IDX_DIM].astype(BF16)
        for kb in range(nb):
            kk_ref[kb] = kfull[kb * QB:(kb + 1) * QB]
            vt_ref[kb] = vtfull[:, kb * QB:(kb + 1) * QB]
            ik_ref[kb] = ikfull[kb * QB:(kb + 1) * QB]

    qt_ref[...] = (sq_ref[...] * (SA_DH ** -0.5 * LOG2E)).T.astype(BF16)
    iqt = iq_ref[...].T.astype(BF16)
    iwt = iwq_ref[...].T * (IDX_HEADS ** -0.5)

    row = lax.broadcasted_iota(I32, (QB, QB), 0)
    lane = lax.broadcasted_iota(I32, (QB, QB), 1)
    diag_bad = (row >= CHUNK) & (lane < CHUNK)

    n_pairs = (nkb + 1) // 2

    def score_block(kb):
        ikb = ik_ref[kb]
        acc = jnp.zeros((QB, QB), F32)
        for h in range(IDX_HEADS):
            s_h = _dot(ikb, iqt[h * IDX_DIM:(h + 1) * IDX_DIM])
            acc = acc + jnp.maximum(s_h, 0.0) * iwt[IDX_DIM + h:IDX_DIM + h + 1, :]
        acc = acc + 0.0
        acc = jnp.where((kb == j) & diag_bad, -jnp.inf, acc)
        bits = pltpu.bitcast(acc, I32)
        key_ref[kb] = jnp.where(bits < 0, bits ^ jnp.int32(0x7FFFFFFF), bits)

    def score_quad(g, carry):
        for i in range(4):
            score_block(4 * g + i)
        return carry

    lax.fori_loop(0, nkb // 4, score_quad, 0)

    @pl.when((nkb & 2) != 0)
    def _():
        score_block(4 * (nkb // 4))
        score_block(4 * (nkb // 4) + 1)

    @pl.when((nkb & 1) != 0)
    def _():
        score_block(nkb - 1)

    @pl.when(j + 1 < 2 * n_pairs)
    def _():
        key_ref[j + 1] = jnp.full((QB, QB), INT_MIN, I32)

    def build_planes(g, carry):
        a = []
        for half in range(2):
            blk = key_ref[2 * g + half] ^ jnp.int32(INT_MIN)
            a.extend(blk[r * 8:(r + 1) * 8, :] for r in range(QB // 8))
        for sh, m in ((16, 0x0000FFFF), (8, 0x00FF00FF), (4, 0x0F0F0F0F), (2, 0x33333333),
                      (1, 0x55555555)):
            for k in range(32):
                if k & sh == 0:
                    t = (lax.shift_right_logical(a[k], jnp.int32(sh)) ^ a[k + sh]) & jnp.int32(m)
                    a[k + sh] = a[k + sh] ^ t
                    a[k] = a[k] ^ lax.shift_left(t, jnp.int32(sh))
        rows = pl.ds(pl.multiple_of(g * 8, 8), 8)
        for b in range(32):
            plane_ref[b, rows, :] = a[b]
        return carry

    def search(_):
        lax.fori_loop(0, n_pairs, build_planes, 0)
        wrow = lax.broadcasted_iota(I32, (nb // 2 * 8, QB), 0)
        cand = jnp.where((wrow >> 3) < n_pairs, jnp.int32(-1), jnp.int32(0))
        above = jnp.zeros((1, QB), I32)
        u = jnp.zeros((1, QB), I32)
        for b in range(31, -1, -1):
            ones = cand & plane_ref[b]
            cnt1 = jnp.sum(lax.population_count(ones), axis=0, keepdims=True)
            take = (above + cnt1) >= topk
            cand = jnp.where(take, ones, cand ^ ones)
            above = jnp.where(take, above, above + cnt1)
            u = u | jnp.where(take, jnp.int32(INT_MIN if b == 31 else 1 << b), jnp.int32(0))
        thr = u ^ jnp.int32(INT_MIN)
        need = topk - above
        n_tied = jnp.sum(lax.population_count(cand), axis=0, keepdims=True)

        def break_ties(_):
            base = (wrow >> 3) * (2 * QB) + (wrow & 7)

            def ibit(i, jp):
                c = jp + lax.shift_left(jnp.int32(1), idx_bits - 1 - i)
                nbits = jnp.clip((c - base + 7) >> 3, 0, 32)
                below = jnp.where(nbits >= 32, jnp.int32(-1),
                                  lax.shift_left(jnp.int32(1), nbits) - 1)
                cnt = jnp.sum(lax.population_count(cand & below), axis=0, keepdims=True)
                return jnp.where(cnt < need, c, jp)

            return lax.fori_loop(0, idx_bits, ibit, jnp.zeros((1, QB), I32))

        def keep_ties(_):
            return jnp.full((1, QB), S, I32)

        jp = lax.cond(jnp.max(n_tied - need) > 0, break_ties, keep_ties, 0)
        return thr, jp

    def take_all(_):
        return (jnp.full((1, QB), INT_MIN, I32), jnp.full((1, QB), -1, I32))

    thr, jp = lax.cond(nkb * QB > topk, search, take_all, 0)

    m_ref[...] = jnp.full(m_ref.shape, NEG_BIG, F32)
    l_ref[...] = jnp.zeros(l_ref.shape, F32)
    acc_ref[...] = jnp.zeros(acc_ref.shape, F32)

    def attend_block(kb):
        key = key_ref[kb]
        kidx = row + kb * QB
        sel = (key > thr) | ((key == thr) & (kidx <= jp))
        hidden = (kidx >= j * QB + CHUNK) & (lane < CHUNK)
        bias = jnp.where(sel & jnp.logical_not(hidden), 0.0, -jnp.inf)
        kblk = kk_ref[kb]
        vtb = vt_ref[kb]
        for h in range(SA_HEADS):
            lg = _dot(kblk, qt_ref[h * SA_DH:(h + 1) * SA_DH, :]) + bias
            m_old = m_ref[h:h + 1, :]
            m_new = jnp.maximum(m_old, jnp.max(lg, axis=0, keepdims=True))
            alpha = jnp.exp2(m_old - m_new)
            p = jnp.exp2(lg - m_new)
            l_ref[h:h + 1, :] = alpha * l_ref[h:h + 1, :] + jnp.sum(p, axis=0, keepdims=True)
            acc_ref[h] = acc_ref[h] * alpha + _dot(vtb, p.astype(BF16))
            m_ref[h:h + 1, :] = m_new

    def attend(g, carry):
        for i in range(8):
            attend_block(8 * g + i)
        return carry

    n_oct = nkb // 8
    lax.fori_loop(0, n_oct, attend, 0)

    @pl.when((nkb & 4) != 0)
    def _():
        for i in range(4):
            attend_block(8 * n_oct + i)

    n_quads = nkb // 4

    @pl.when((nkb & 2) != 0)
    def _():
        attend_block(4 * n_quads)
        attend_block(4 * n_quads + 1)

    @pl.when((nkb & 1) != 0)
    def _():
        attend_block(nkb - 1)

    outs = []
    for h in range(SA_HEADS):
        outs.append(acc_ref[h] * (1.0 / l_ref[h:h + 1, :]))
    o_ref[...] = jnp.concatenate(outs, axis=0).T


def _dsa(z, cnorm, wk, wvt, B, S):
    T = B * S
    nb = S // QBLOCK
    topk = min(TOPK_MAX, S // 4)
    assert nb % 2 == 0
    kern = functools.partial(_dsa_kernel, topk)
    return pl.pallas_call(
        kern,
        out_shape=jax.ShapeDtypeStruct((T, SA_WIDTH), F32),
        grid=(B, nb),
        in_specs=[
            pl.BlockSpec((QBLOCK, SA_WIDTH), lambda b, j: (b * nb + j, SQ_OFF // SA_WIDTH)),
            pl.BlockSpec((QBLOCK, IDX_HEADS * IDX_DIM),
                         lambda b, j: (b * nb + j, IQ_OFF // (IDX_HEADS * IDX_DIM))),
            pl.BlockSpec((QBLOCK, LANES), lambda b, j: (b * nb + j, IKW_OFF // LANES)),
            pl.BlockSpec((S, SA_LATENT), lambda b, j: (b, SC_OFF // SA_LATENT)),
            pl.BlockSpec((S, LANES), lambda b, j: (b, IKW_OFF // LANES)),
            pl.BlockSpec((1, SA_LATENT), lambda b, j: (0, 0)),
            pl.BlockSpec((SA_LATENT, SA_DH), lambda b, j: (0, 0)),
            pl.BlockSpec((SA_DH, SA_LATENT), lambda b, j: (0, 0)),
        ],
        out_specs=pl.BlockSpec((QBLOCK, SA_WIDTH), lambda b, j: (b * nb + j, 0)),
        scratch_shapes=[
            pltpu.VMEM((nb, QBLOCK, SA_DH), BF16),
            pltpu.VMEM((nb, SA_DH, QBLOCK), BF16),
            pltpu.VMEM((nb, QBLOCK, IDX_DIM), BF16),
            pltpu.VMEM((nb, QBLOCK, QBLOCK), I32),
            pltpu.VMEM((32, nb // 2 * 8, QBLOCK), I32),
            pltpu.VMEM((SA_WIDTH, QBLOCK), BF16),
            pltpu.VMEM((SA_HEADS, QBLOCK), F32),
            pltpu.VMEM((SA_HEADS, QBLOCK), F32),
            pltpu.VMEM((SA_HEADS, SA_DH, QBLOCK), F32),
        ],
        compiler_params=_cparams(("parallel", "arbitrary")),
        name="dsa",
    )(z, z, z, z, z, cnorm, wk, wvt)


def _sb_kernel(q_ref, k_ref, v_ref, uo_ref, o_ref, kt_ref, vb_ref, run_ref, acc_ref,
               ls_ref, lkh_ref, lkl_ref, tot_ref, a_ref):
    QB = QBLOCK
    j = pl.program_id(1)
    S = k_ref.shape[0]
    nb = S // QB
    npair = SB_HEADS // 2
    lane = lax.broadcasted_iota(I32, (QB, QB), 1)
    row = lax.broadcasted_iota(I32, (QB, QB), 0)

    @pl.when(j == 0)
    def _():
        for p in range(npair):
            cols = slice(p * LANES, (p + 1) * LANES)
            kt = k_ref[:, cols].T
            vv = v_ref[:, cols]
            for hh in range(2):
                lo = hh * SB_DH
                ktz = jnp.where((row >= lo) & (row < lo + SB_DH), 1.0, 0.0)
                vz = jnp.where((lane >= lo) & (lane < lo + SB_DH), 1.0, 0.0)
                for kb in range(nb):
                    ks = slice(kb * QB, (kb + 1) * QB)
                    kt_ref[kb, 2 * p + hh] = (kt[:, ks] * ktz).astype(BF16)
                    vb_ref[kb, 2 * p + hh] = (vv[ks, :] * vz).astype(BF16)

    uo = uo_ref[...]
    strict = lane < row
    qs = (q_ref[...] * (SB_DH ** -0.5)).astype(BF16)

    def all_heads(kbs, diag):
        nk = len(kbs)
        for h in range(SB_HEADS):
            p = h // 2
            for i, kb in enumerate(kbs):
                s = h * nk + i
                z = _dot(qs[:, p * LANES:(p + 1) * LANES], kt_ref[kb, h])
                l1p = jnp.log(1.0 + jnp.exp(-jnp.abs(z)))
                ls_pos = jnp.minimum(z, 0.0) - l1p
                lk = ls_pos - z
                if diag and i == 0:
                    lk = jnp.where(strict, lk, 0.0)
                lk_hi, lk_lo = _split_bf16(lk)
                ls_ref[s] = ls_pos
                lkh_ref[s] = lk_hi
                lkl_ref[s] = lk_lo
                tot_ref[s] = jnp.sum(lk, axis=1, keepdims=True)
        top = None
        for h in range(SB_HEADS):
            run = None if diag else run_ref[h]
            for i in range(nk):
                s = h * nk + i
                after = _dot(lkh_ref[s], uo) + _dot(lkl_ref[s], uo)
                if diag and i == 0:
                    a = jnp.where(strict, jnp.exp(ls_ref[s] + after), 0.0)
                    run = tot_ref[s]
                else:
                    a = jnp.exp(ls_ref[s] + after + run)
                    run = run + tot_ref[s]
                a_ref[s] = a.astype(BF16)
            run_ref[h] = run
            top = run if top is None else jnp.maximum(top, run)
        for p in range(npair):
            pv = None
            for h in (2 * p, 2 * p + 1):
                for i, kb in enumerate(kbs):
                    out = _dot(a_ref[h * nk + i], vb_ref[kb, h])
                    pv = out if pv is None else pv + out
            if diag:
                acc_ref[p] = pv
            else:
                acc_ref[p] += pv
        return jnp.max(top) > SB_UNDERFLOW

    n_first = 3
    alive = lax.cond(
        j >= 2, lambda: all_heads([j, j - 1, j - 2], True),
        lambda: lax.cond(j == 1, lambda: all_heads([j, j - 1], True), lambda: all_heads([j], True)))

    def cond(c):
        return (c[0] >= 1) & c[1]

    def body(c):
        return c[0] - 2, all_heads([c[0], c[0] - 1], False)

    kb_left, alive = lax.while_loop(cond, body, (j - n_first, alive))

    @pl.when((kb_left == 0) & alive)
    def _():
        all_heads([0], False)

    o_ref[...] = jnp.concatenate([acc_ref[p] for p in range(npair)], axis=1)


def _sb(z, B, S):
    T = B * S
    nb = S // QBLOCK
    u = np.tril(np.ones((QBLOCK, QBLOCK), np.float32), -1)
    uo = jnp.asarray(u, BF16)
    qoff = SB_OFF // SB_WIDTH
    return pl.pallas_call(
        _sb_kernel,
        out_shape=jax.ShapeDtypeStruct((T, SB_WIDTH), F32),
        grid=(B, nb),
        in_specs=[
            pl.BlockSpec((QBLOCK, SB_WIDTH), lambda b, j: (b * nb + j, qoff)),
            pl.BlockSpec((S, SB_WIDTH), lambda b, j: (b, qoff + 1)),
            pl.BlockSpec((S, SB_WIDTH), lambda b, j: (b, qoff + 2)),
            pl.BlockSpec(uo.shape, lambda b, j: (0, 0)),
        ],
        out_specs=pl.BlockSpec((QBLOCK, SB_WIDTH), lambda b, j: (b * nb + j, 0)),
        scratch_shapes=[
            pltpu.VMEM((nb, SB_HEADS, LANES, QBLOCK), BF16),
            pltpu.VMEM((nb, SB_HEADS, QBLOCK, LANES), BF16),
            pltpu.VMEM((SB_HEADS, QBLOCK, 1), F32),
            pltpu.VMEM((SB_HEADS // 2, QBLOCK, LANES), F32),
            pltpu.VMEM((3 * SB_HEADS, QBLOCK, QBLOCK), F32),
            pltpu.VMEM((3 * SB_HEADS, QBLOCK, QBLOCK), BF16),
            pltpu.VMEM((3 * SB_HEADS, QBLOCK, QBLOCK), BF16),
            pltpu.VMEM((3 * SB_HEADS, QBLOCK, 1), F32),
            pltpu.VMEM((3 * SB_HEADS, QBLOCK, QBLOCK), BF16),
        ],
        compiler_params=_cparams(("parallel", "arbitrary")),
        name="stickbreak",
    )(z, z, z, uo)


def _merge_kernel(a_ref, b_ref, c_ref, u_ref, h_ref, wgz_ref, wb_ref, wo_ref, gn_ref, ho_ref, v_ref):
    u = u_ref[...]
    merged = None
    for n, br in enumerate((a_ref, b_ref, c_ref)):
        proj = _dot(br[...].astype(BF16), wb_ref[n])
        gate = _sigmoid(_dot(u, wgz_ref[:, n * D_MODEL:(n + 1) * D_MODEL]))
        merged = gate * proj if merged is None else merged + gate * proj
    hn = h_ref[...] + _dot(merged.astype(BF16), wo_ref[...])
    ho_ref[...] = hn
    ms = jnp.mean(hn * hn, axis=-1, keepdims=True)
    v_ref[...] = (hn * lax.rsqrt(ms + EPS) * gn_ref[...]).astype(BF16)


def _merge(a, b, c, u, h, wgz, wb, wo, gn, tm=256):
    T, D = h.shape
    bw = BRANCH_WIDTH
    return pl.pallas_call(
        _merge_kernel,
        out_shape=(jax.ShapeDtypeStruct((T, D), F32), jax.ShapeDtypeStruct((T, D), BF16)),
        grid=(T // tm,),
        in_specs=[
            pl.BlockSpec((tm, bw), lambda i: (i, 0)),
            pl.BlockSpec((tm, bw), lambda i: (i, 0)),
            pl.BlockSpec((tm, bw), lambda i: (i, 0)),
            pl.BlockSpec((tm, D), lambda i: (i, 0)),
            pl.BlockSpec((tm, D), lambda i: (i, 0)),
            pl.BlockSpec((D, GZW), lambda i: (0, 0)),
            pl.BlockSpec((N_BRANCH, bw, D), lambda i: (0, 0, 0)),
            pl.BlockSpec((D, D), lambda i: (0, 0)),
            pl.BlockSpec((1, D), lambda i: (0, 0)),
        ],
        out_specs=(pl.BlockSpec((tm, D), lambda i: (i, 0)), pl.BlockSpec((tm, D), lambda i: (i, 0))),
        compiler_params=_cparams(("parallel",)),
        name="merge",
    )(a, b, c, u, h, wgz, wb, wo, gn)


def _swiglu_kernel(v_ref, h_ref, wg_ref, wu_ref, wd_ref, o_ref, acc_ref):
    f = pl.program_id(1)

    @pl.when(f == 0)
    def _():
        acc_ref[...] = h_ref[...]

    v = v_ref[...]
    gt = _dot(v, wg_ref[...])
    up = _dot(v, wu_ref[...])
    act = (gt * _sigmoid(gt) * up).astype(BF16)
    acc_ref[...] += _dot(act, wd_ref[...])

    @pl.when(f == pl.num_programs(1) - 1)
    def _():
        o_ref[...] = acc_ref[...]


def _swiglu(v, h, wg, wu, wd, tm=512, tf=1408):
    T, D = h.shape
    F = wg.shape[1]
    return pl.pallas_call(
        _swiglu_kernel,
        out_shape=jax.ShapeDtypeStruct((T, D), F32),
        grid=(T // tm, F // tf),
        in_specs=[
            pl.BlockSpec((tm, D), lambda i, f: (i, 0)),
            pl.BlockSpec((tm, D), lambda i, f: (i, 0)),
            pl.BlockSpec((D, tf), lambda i, f: (0, f)),
            pl.BlockSpec((D, tf), lambda i, f: (0, f)),
            pl.BlockSpec((tf, D), lambda i, f: (f, 0)),
        ],
        out_specs=pl.BlockSpec((tm, D), lambda i, f: (i, 0)),
        scratch_shapes=[pltpu.VMEM((tm, D), F32)],
        compiler_params=_cparams(("parallel", "arbitrary")),
        name="swiglu",
    )(v, h, wg, wu, wd)


MOE_TILE = 1024
MOE_CHUNK = 288


def _router_kernel(h_ref, gn_ref, r_ref, us_ref, cw_ref, cwt_ref, rkt_ref):
    hh = h_ref[...]
    ms = jnp.mean(hh * hh, axis=-1, keepdims=True)
    vf = hh * lax.rsqrt(ms + EPS) * gn_ref[...]
    v_hi, v_lo = _split_bf16(vf)
    r_hi, r_lo = _split_bf16(r_ref[...])
    logits = _dot(v_hi, r_hi) + _dot(v_hi, r_lo) + _dot(v_lo, r_hi)
    lane = lax.broadcasted_iota(I32, logits.shape, 1)
    logits = jnp.where(lane < N_EXPERTS, logits, -jnp.inf)
    m1 = jnp.max(logits, axis=-1, keepdims=True)
    i1 = jnp.min(jnp.where(logits == m1, lane, LANES), axis=-1, keepdims=True)
    rest = jnp.where(lane == i1, -jnp.inf, logits)
    m2 = jnp.max(rest, axis=-1, keepdims=True)
    i2 = jnp.min(jnp.where(rest == m2, lane, LANES), axis=-1, keepdims=True)
    e2 = jnp.exp(m2 - m1)
    w1 = 1.0 / (1.0 + e2)
    cw = jnp.where(lane == i1, w1, 0.0) + jnp.where(lane == i2, e2 * w1, 0.0)
    cw_ref[...] = cw
    cwt = cw.T
    cwt_ref[...] = cwt[0:N_EXPERTS]
    sel = jnp.where(cwt != 0.0, 1.0, 0.0).astype(BF16)
    rkt_ref[...] = _dot(sel, us_ref[...])[0:N_EXPERTS]


def _router(h, gn, router, tm=MOE_TILE):
    T, D = h.shape
    us = jnp.asarray(np.triu(np.ones((tm, tm), np.float32), 1), BF16)
    return pl.pallas_call(
        _router_kernel,
        out_shape=(jax.ShapeDtypeStruct((T, LANES), F32),
                   jax.ShapeDtypeStruct((N_EXPERTS, T), F32),
                   jax.ShapeDtypeStruct((N_EXPERTS, T), F32)),
        grid=(T // tm,),
        in_specs=[
            pl.BlockSpec((tm, D), lambda i: (i, 0)),
            pl.BlockSpec((1, D), lambda i: (0, 0)),
            pl.BlockSpec((D, LANES), lambda i: (0, 0)),
            pl.BlockSpec((tm, tm), lambda i: (0, 0)),
        ],
        out_specs=(pl.BlockSpec((tm, LANES), lambda i: (i, 0)),
                   pl.BlockSpec((N_EXPERTS, tm), lambda i: (0, i)),
                   pl.BlockSpec((N_EXPERTS, tm), lambda i: (0, i))),
        compiler_params=_cparams(("parallel",)),
        name="router",
    )(h, gn, router, us)


MOE_SUB = 256
MOE_WIN = 112


def _moe_kernel(v_ref, h_ref, cw_ref, cwt_ref, rkt_ref, wg_ref, wu_ref, wd_ref, o_ref,
                xg_ref, ce_ref, ya_ref):
    e = pl.program_id(1)
    f = pl.program_id(2)
    nf = pl.num_programs(2)
    tm = v_ref.shape[0]
    CH, SUB, WIN = MOE_CHUNK, MOE_SUB, MOE_WIN
    nsub = tm // SUB

    @pl.when((e == 0) & (f == 0))
    def _():
        o_ref[...] = h_ref[...]

    ce_row = cwt_ref[pl.ds(e, 1), :]
    routed = ce_row != 0.0
    rank_row = rkt_ref[pl.ds(e, 1), :]
    slot_row = jnp.where(routed, rank_row, -1.0)
    count = jnp.sum(jnp.where(routed, 1, 0).astype(I32))
    n_chunks = (count + (CH - 1)) // CH
    bounds = [jnp.int32(0)]
    for s in range(1, nsub):
        bounds.append(rank_row[0, s * SUB].astype(I32))
    bounds.append(count)
    wiota = lax.broadcasted_iota(I32, (WIN, SUB), 0).astype(F32)

    def for_windows(fn):
        def window(s, r0):
            slots = slot_row[:, s * SUB:(s + 1) * SUB]
            pick = jnp.where(slots == wiota + r0.astype(F32), 1.0, 0.0).astype(BF16)
            fn(s, r0, pick)

        starts = [pl.multiple_of((bounds[s] // 16) * 16, 16) for s in range(nsub)]
        for s in range(nsub):
            window(s, starts[s])
        for s in range(nsub):
            nw = (bounds[s + 1] - starts[s] + (WIN - 1)) // WIN

            def body(m, carry, s=s):
                window(s, pl.multiple_of(starts[s] + m * WIN, 16))
                return carry

            lax.fori_loop(1, nw, body, 0)

    @pl.when(f == 0)
    def _():
        def clear(i, carry):
            rows = pl.ds(pl.multiple_of(i * CH, 16), CH)
            xg_ref[rows, :] = jnp.zeros((CH, xg_ref.shape[1]), BF16)
            ce_ref[rows, :] = jnp.zeros((CH, LANES), F32)
            ya_ref[rows, :] = jnp.zeros((CH, ya_ref.shape[1]), F32)
            return carry

        lax.fori_loop(0, jnp.minimum(n_chunks + 1, xg_ref.shape[0] // CH), clear, 0)
        lane = lax.broadcasted_iota(I32, (WIN, LANES), 1)

        def gather(s, r0, pick):
            toks = slice(s * SUB, (s + 1) * SUB)
            rows = pl.ds(r0, WIN)
            got = _dot(pick, v_ref[toks, :])
            xg_ref[rows, :] = (xg_ref[rows, :].astype(F32) + got).astype(BF16)
            c_hi, c_lo = _split_bf16(cw_ref[toks, :])
            cg = _dot(pick, c_hi) + _dot(pick, c_lo)
            ce = jnp.sum(jnp.where(lane == e, cg, 0.0), axis=-1, keepdims=True)
            ce_ref[rows, :] += jnp.broadcast_to(ce, (WIN, LANES))

        for_windows(gather)

    def expert(i, carry):
        rows = pl.ds(pl.multiple_of(i * CH, 16), CH)
        xg = xg_ref[rows, :]
        gt = _dot(xg, wg_ref[0])
        up = _dot(xg, wu_ref[0])
        act = (gt * _sigmoid(gt) * up * ce_ref[rows, 0:1]).astype(BF16)
        ya_ref[rows, :] += _dot(act, wd_ref[0])
        return carry

    lax.fori_loop(0, n_chunks, expert, 0)

    @pl.when(f == nf - 1)
    def _():
        def scatter(s, r0, pick):
            ye = ya_ref[pl.ds(r0, WIN), :].astype(BF16)
            o_ref[s * SUB:(s + 1) * SUB, :] += _dot_tn(pick, ye)

        for_windows(scatter)


def _moe(v, h, cw, cwt, rkt, wg, wu, wd, tm=MOE_TILE, fsplit=2):
    T, D = h.shape
    E, _, F = wg.shape
    tf = F // fsplit
    rows = (-(-tm // MOE_CHUNK)) * MOE_CHUNK
    assert rows >= tm + MOE_WIN and tm % MOE_SUB == 0
    return pl.pallas_call(
        _moe_kernel,
        out_shape=jax.ShapeDtypeStruct((T, D), F32),
        grid=(T // tm, E, fsplit),
        in_specs=[
            pl.BlockSpec((tm, D), lambda i, e, f: (i, 0), pipeline_mode=pl.Buffered(1)),
            pl.BlockSpec((tm, D), lambda i, e, f: (i, 0), pipeline_mode=pl.Buffered(1)),
            pl.BlockSpec((tm, LANES), lambda i, e, f: (i, 0), pipeline_mode=pl.Buffered(1)),
            pl.BlockSpec((N_EXPERTS, tm), lambda i, e, f: (0, i)),
            pl.BlockSpec((N_EXPERTS, tm), lambda i, e, f: (0, i)),
            pl.BlockSpec((1, D, tf), lambda i, e, f: (e, 0, f)),
            pl.BlockSpec((1, D, tf), lambda i, e, f: (e, 0, f)),
            pl.BlockSpec((1, tf, D), lambda i, e, f: (e, f, 0)),
        ],
        out_specs=pl.BlockSpec((tm, D), lambda i, e, f: (i, 0)),
        scratch_shapes=[
            pltpu.VMEM((rows, D), BF16),
            pltpu.VMEM((rows, LANES), F32),
            pltpu.VMEM((rows, D), F32),
        ],
        compiler_params=_cparams(("parallel", "arbitrary", "arbitrary")),
        name="moe",
    )(v, h, cw, cwt, rkt, wg, wu, wd)


def _ple_kernel(final, h_ref, p_ref, gn_ref, wg_ref, wp_ref, fn_ref, o_ref):
    h = h_ref[...]
    ms = jnp.mean(h * h, axis=-1, keepdims=True)
    u = (h * lax.rsqrt(ms + EPS) * gn_ref[...]).astype(BF16)
    gate = _sigmoid(_dot(u, wg_ref[...]))
    hn = h + gate * _dot(p_ref[0].astype(BF16), wp_ref[...])
    if final:
        ms = jnp.mean(hn * hn, axis=-1, keepdims=True)
        hn = hn * lax.rsqrt(ms + EPS) * fn_ref[...]
    o_ref[...] = hn


def _ple(h, p, layer, gn, wg, wp, fn, final, tm=512):
    T, D = h.shape
    return pl.pallas_call(
        functools.partial(_ple_kernel, final),
        out_shape=jax.ShapeDtypeStruct((T, D), F32),
        grid=(T // tm,),
        in_specs=[
            pl.BlockSpec((tm, D), lambda i: (i, 0)),
            pl.BlockSpec((1, tm, PLE_DIM), lambda i: (layer, i, 0)),
            pl.BlockSpec((1, D), lambda i: (0, 0)),
            pl.BlockSpec((D, D), lambda i: (0, 0)),
            pl.BlockSpec((PLE_DIM, D), lambda i: (0, 0)),
            pl.BlockSpec((1, D), lambda i: (0, 0)),
        ],
        out_specs=pl.BlockSpec((tm, D), lambda i: (i, 0)),
        compiler_params=_cparams(("parallel",)),
        name="ple",
    )(h, p, gn, wg, wp, fn)


def _reorder_kernel(w_ref, o_ref, ogz_ref):
    o_sq = 4 * HG_WIDTH
    o_sc = o_sq + SA_WIDTH
    o_iq = o_sc + SA_LATENT
    o_ik = o_iq + IDX_HEADS * IDX_DIM
    o_bq = o_ik + IDX_DIM + IDX_HEADS
    o_gz = o_bq + 3 * SB_WIDTH

    def put(dst, lo, hi):
        o_ref[:, dst:dst + (hi - lo)] = w_ref[0, :, lo:hi].astype(BF16)

    ogz_ref[...] = w_ref[0, :, o_gz:o_gz + GZW].astype(BF16)
    put(HG_OFF, 0, o_sq)
    put(SQ_OFF, o_sq, o_sc)
    put(IQ_OFF, o_iq, o_ik)
    put(SC_OFF, o_sc, o_iq)
    put(IKW_OFF, o_ik, o_bq)
    used = IDX_DIM + IDX_HEADS
    o_ref[:, IKW_OFF + used:SB_OFF] = jnp.zeros((o_ref.shape[0], LANES - used), BF16)
    put(SB_OFF, o_bq, o_gz)


def _reorder_w_in(w_in, layer, tr=256):
    _, R, C = w_in.shape
    return pl.pallas_call(
        _reorder_kernel,
        out_shape=(jax.ShapeDtypeStruct((R, ZW), BF16), jax.ShapeDtypeStruct((R, GZW), BF16)),
        grid=(R // tr,),
        in_specs=[pl.BlockSpec((1, tr, C), lambda i: (layer, i, 0))],
        out_specs=(pl.BlockSpec((tr, ZW), lambda i: (i, 0)), pl.BlockSpec((tr, GZW), lambda i: (i, 0))),
        compiler_params=_cparams(("parallel",)),
        name="reorder_w_in",
    )(w_in)


def kernel(x, p, lb_param, norm_mix, w_in, hg_onorm, sa_cnorm, sa_wk, sa_wv, w_branch, w_out,
           norm_ffn, dense_wg, dense_wu, dense_wd, moe_router, moe_wg, moe_wu, moe_wd, norm_ple,
           ple_gate, ple_proj, norm_final):
    B, S, D = x.shape
    T = B * S
    depth = w_in.shape[0]
    lb_sm = jax.nn.softmax(lb_param.astype(F32), axis=0)
    lb_all = jnp.cumsum(lb_sm, axis=0) - lb_sm[0:1]
    h = x.reshape(T, D)
    for i in range(depth):
        w_mix, w_gz = _reorder_w_in(w_in, i)
        z, u = _inproj(h, norm_mix[i][None, :], w_mix)
        a = _hgrn(z, lb_all[i][None, :], hg_onorm[i][None, :], B, S)
        b = _dsa(z, sa_cnorm[i][None, :], sa_wk[i].astype(BF16), sa_wv[i].T.astype(BF16), B, S)
        c = _sb(z, B, S)
        h, v = _merge(a, b, c, u, h, w_gz, w_branch[i].astype(BF16), w_out[i].astype(BF16),
                      norm_ffn[i][None, :])
        jj = i // 2
        if i % 2 == 0:
            h = _swiglu(v, h, dense_wg[jj].astype(BF16), dense_wu[jj].astype(BF16),
                        dense_wd[jj].astype(BF16))
        else:
            router = jnp.pad(moe_router[jj], ((0, 0), (0, LANES - N_EXPERTS)))
            cw, cwt, rkt = _router(h, norm_ffn[i][None, :], router)
            h = _moe(v, h, cw, cwt, rkt, moe_wg[jj].astype(BF16), moe_wu[jj].astype(BF16),
                     moe_wd[jj].astype(BF16))
        h = _ple(h, p.reshape(depth, T, PLE_DIM), i, norm_ple[i][None, :],
                 ple_gate[i].astype(BF16), ple_proj[i].astype(BF16), norm_final[None, :],
                 i == depth - 1)
    return h.reshape(B, S, D)
```

```python
import functools

import numpy as np
import jax
import jax.numpy as jnp
from jax import lax
from jax.experimental import pallas as pl
from jax.experimental.pallas import tpu as pltpu

F32 = jnp.float32
BF16 = jnp.bfloat16
I32 = jnp.int32

D_MODEL = 1024
CHUNK = 64
QBLOCK = 128
HG_HEADS = 4
HG_DK = 128
HG_WIDTH = HG_HEADS * HG_DK
SA_HEADS = 8
SA_DH = 64
SA_WIDTH = SA_HEADS * SA_DH
SA_LATENT = 128
IDX_HEADS = 4
IDX_DIM = 64
TOPK_MAX = 256
SB_HEADS = 8
SB_DH = 64
SB_WIDTH = SB_HEADS * SB_DH
N_BRANCH = 3
BRANCH_WIDTH = 512
N_EXPERTS = 8
PLE_DIM = 256
EPS = 1e-6

LANES = 128

HG_OFF = 0
SQ_OFF = HG_OFF + 4 * HG_WIDTH
IQ_OFF = SQ_OFF + SA_WIDTH
SC_OFF = IQ_OFF + IDX_HEADS * IDX_DIM
IKW_OFF = SC_OFF + SA_LATENT
SB_OFF = IKW_OFF + LANES
ZW = SB_OFF + 3 * SB_WIDTH
GZW = N_BRANCH * D_MODEL

VMEM_LIMIT = 56 * 1024 * 1024

NEG_BIG = -1e30
INT_MIN = -2147483648
LOG2E = 1.4426950408889634
SB_UNDERFLOW = -104.0


def _cparams(sem):
    return pltpu.CompilerParams(dimension_semantics=sem, vmem_limit_bytes=VMEM_LIMIT)


def _sigmoid(x):
    e = jnp.exp(-jnp.abs(x))
    inv = 1.0 / (1.0 + e)
    return jnp.where(x >= 0, inv, e * inv)


def _dot(a, b):
    return jnp.dot(a, b, preferred_element_type=F32)


def _dot_nt(a, b):
    return lax.dot_general(a, b, (((1,), (1,)), ((), ())), preferred_element_type=F32)


def _dot_tn(a, b):
    return lax.dot_general(a, b, (((0,), (0,)), ((), ())), preferred_element_type=F32)


def _split_bf16(x):
    hi = x.astype(BF16)
    lo = (x - hi.astype(F32)).astype(BF16)
    return hi, lo


def _inproj_kernel(x_ref, g_ref, w_ref, o_ref, u_ref):
    @pl.when(pl.program_id(1) == 0)
    def _():
        x = x_ref[...]
        ms = jnp.mean(x * x, axis=-1, keepdims=True)
        u_ref[...] = (x * lax.rsqrt(ms + EPS) * g_ref[...]).astype(BF16)

    o_ref[...] = _dot(u_ref[...], w_ref[...])


def _inproj(h, gain, w, tm=512, tn=ZW):
    T, D = h.shape
    N = w.shape[1]
    return pl.pallas_call(
        _inproj_kernel,
        out_shape=(jax.ShapeDtypeStruct((T, N), F32), jax.ShapeDtypeStruct((T, D), BF16)),
        grid=(T // tm, N // tn),
        in_specs=[
            pl.BlockSpec((tm, D), lambda i, j: (i, 0)),
            pl.BlockSpec((1, D), lambda i, j: (0, 0)),
            pl.BlockSpec((D, tn), lambda i, j: (0, j)),
        ],
        out_specs=(pl.BlockSpec((tm, tn), lambda i, j: (i, j)),
                   pl.BlockSpec((tm, D), lambda i, j: (i, 0))),
        compiler_params=_cparams(("parallel", "arbitrary")),
        name="inproj",
    )(h, gain, w)


_HG_LEVELS = (64, 32, 16, 8, 4, 2)


def _hgrn_consts():
    C = CHUNK
    blocks, masks = [], []
    for n in _HG_LEVELS:
        half = n // 2
        L = np.zeros((C, C), np.float32)
        M = np.zeros((C, C), np.float32)
        for t in range(C):
            base = (t // n) * n
            mid = base + half
            if t >= mid:
                L[t, mid:t + 1] = 1.0
                M[t, base:mid] = 1.0
            else:
                L[t, t + 1:mid] = 1.0
        blocks.append(L)
        masks.append(M)
    masks.append(np.eye(C, dtype=np.float32))
    blocks.append(np.tril(np.ones((C, C), np.float32)))
    blocks.append(np.triu(np.ones((C, C), np.float32), 1))
    return np.concatenate(blocks, 0), np.stack(masks)


def _hgrn_kernel(q_ref, f_ref, i_ref, g_ref, lb_ref, on_ref, lc_ref, mc_ref, o_ref, st_ref,
                 qq_ref, kk_ref, lfh_ref, lfl_ref, ex_ref, sc_ref):
    C = CHUNK
    nlev = len(_HG_LEVELS)

    @pl.when(pl.program_id(1) == 0)
    def _():
        st_ref[...] = jnp.zeros(st_ref.shape, F32)

    lc = lc_ref[...]
    onorm = on_ref[...]
    n_chunks = q_ref.shape[0] // C

    def gates(c, slot):
        rows = pl.ds(pl.multiple_of(c * C, C), C)
        fpre = f_ref[rows, :]
        lb = lb_ref[...]
        e = jnp.exp(-jnp.abs(fpre))
        inv = 1.0 / (1.0 + e)
        sg_pos = jnp.where(fpre >= 0, inv, e * inv)
        sg_neg = jnp.where(fpre >= 0, e * inv, inv)
        lf = jnp.log(lb + (1.0 - lb) * sg_pos)
        kk_ref[slot] = (1.0 - lb) * sg_neg
        qin = q_ref[rows, :]
        qq_ref[slot] = qin * _sigmoid(qin)
        lf_hi, lf_lo = _split_bf16(lf)
        lfh_ref[slot] = lf_hi
        lfl_ref[slot] = lf_lo

    gates(0, 0)

    def chunk(c, carry):
        r0 = pl.multiple_of(c * C, C)
        rows = pl.ds(r0, C)
        slot = c % 2
        ex_ref[...] = jnp.exp(_dot(lc, lfh_ref[slot]) + _dot(lc, lfl_ref[slot]))
        gates(jnp.minimum(c + 1, n_chunks - 1), 1 - slot)
        for h in range(HG_HEADS):
            cs = slice(h * HG_DK, (h + 1) * HG_DK)
            q = qq_ref[slot, :, cs]
            k = kk_ref[slot, :, cs]
            scores = _dot_nt(q.astype(BF16), k.astype(BF16)) * mc_ref[nlev]
            for l in range(nlev):
                el = ex_ref[l * C:(l + 1) * C, cs]
                s_l = _dot_nt((q * el).astype(BF16), (k * el).astype(BF16))
                scores = scores + s_l * mc_ref[l]
            sc_ref[h] = scores.astype(BF16)
        for h in range(HG_HEADS):
            cs = slice(h * HG_DK, (h + 1) * HG_DK)
            eb = ex_ref[nlev * C:(nlev + 1) * C, cs]
            er = ex_ref[(nlev + 1) * C:(nlev + 2) * C, cs]
            st = st_ref[h]
            vb = i_ref[rows, cs].astype(BF16)
            o = _dot(sc_ref[h], vb) + \
                _dot_nt((qq_ref[slot, :, cs] * eb).astype(BF16), st.astype(BF16))
            st_ref[h] = st * eb[C - 1:C, :] + _dot_tn(vb, (kk_ref[slot, :, cs] * er).astype(BF16))
            ms = jnp.mean(o * o, axis=-1, keepdims=True)
            on = o * lax.rsqrt(ms + EPS) * onorm
            g = g_ref[rows, cs]
            o_ref[rows, cs] = on * (g * _sigmoid(g))
        return carry

    lax.fori_loop(0, n_chunks, chunk, 0)


def _hgrn(z, lb, onorm, B, S, sblk=1024):
    T = B * S
    sblk = min(sblk, S)
    ns = S // sblk
    lc_np, mc_np = _hgrn_consts()
    lc = jnp.asarray(lc_np, BF16)
    mc = jnp.asarray(mc_np, F32)
    cb = HG_OFF // HG_WIDTH

    def zspec(k):
        return pl.BlockSpec((sblk, HG_WIDTH), lambda b, s, k=k: (b * ns + s, cb + k))

    return pl.pallas_call(
        _hgrn_kernel,
        out_shape=jax.ShapeDtypeStruct((T, HG_WIDTH), F32),
        grid=(B, ns),
        in_specs=[
            zspec(0), zspec(1), zspec(2), zspec(3),
            pl.BlockSpec((1, HG_WIDTH), lambda b, s: (0, 0)),
            pl.BlockSpec((1, HG_DK), lambda b, s: (0, 0)),
            pl.BlockSpec(lc.shape, lambda b, s: (0, 0)),
            pl.BlockSpec(mc.shape, lambda b, s: (0, 0, 0)),
        ],
        out_specs=pl.BlockSpec((sblk, HG_WIDTH), lambda b, s: (b * ns + s, 0)),
        scratch_shapes=[
            pltpu.VMEM((HG_HEADS, HG_DK, HG_DK), F32),
            pltpu.VMEM((2, CHUNK, HG_WIDTH), F32),
            pltpu.VMEM((2, CHUNK, HG_WIDTH), F32),
            pltpu.VMEM((2, CHUNK, HG_WIDTH), BF16),
            pltpu.VMEM((2, CHUNK, HG_WIDTH), BF16),
            pltpu.VMEM(((len(_HG_LEVELS) + 2) * CHUNK, HG_WIDTH), F32),
            pltpu.VMEM((HG_HEADS, CHUNK, CHUNK), BF16),
        ],
        compiler_params=_cparams(("parallel", "arbitrary")),
        name="hgrn2",
    )(z, z, z, z, lb, onorm, lc, mc)


def _dsa_kernel(topk, sq_ref, iq_ref, iwq_ref, sc_ref, ikw_ref, cn_ref, wk_ref, wvt_ref,
                o_ref, kk_ref, vt_ref, ik_ref, key_ref, plane_ref, qt_ref, m_ref, l_ref, acc_ref):
    QB = QBLOCK
    j = pl.program_id(1)
    nkb = j + 1
    S = sc_ref.shape[0]
    nb = S // QB
    idx_bits = (S - 1).bit_length()

    @pl.when(j == 0)
    def _():
        c = sc_ref[...]
        ms = jnp.mean(c * c, axis=-1, keepdims=True)
        cb = (c * lax.rsqrt(ms + EPS) * cn_ref[...]).astype(BF16)
        kfull = _dot(cb, wk_ref[...]).astype(BF16)
        vtfull = _dot_nt(wvt_ref[...], cb).astype(BF16)
        ikfull = ikw_ref[:, 0:IDX_DIM].astype(BF16)
        for kb in range(nb):
            kk_ref[kb] = kfull[kb * QB:(kb + 1) * QB]
            vt_ref[kb] = vtfull[:, kb * QB:(kb + 1) * QB]
            ik_ref[kb] = ikfull[kb * QB:(kb + 1) * QB]

    qt_ref[...] = (sq_ref[...] * (SA_DH ** -0.5 * LOG2E)).T.astype(BF16)
    iqt = iq_ref[...].T.astype(BF16)
    iwt = iwq_ref[...].T * (IDX_HEADS ** -0.5)

    row = lax.broadcasted_iota(I32, (QB, QB), 0)
    lane = lax.broadcasted_iota(I32, (QB, QB), 1)
    diag_bad = (row >= CHUNK) & (lane < CHUNK)

    n_pairs = (nkb + 1) // 2

    def score_block(kb):
        ikb = ik_ref[kb]
        acc = jnp.zeros((QB, QB), F32)
        for h in range(IDX_HEADS):
            s_h = _dot(ikb, iqt[h * IDX_DIM:(h + 1) * IDX_DIM])
            acc = acc + jnp.maximum(s_h, 0.0) * iwt[IDX_DIM + h:IDX_DIM + h + 1, :]
        acc = acc + 0.0
        acc = jnp.where((kb == j) & diag_bad, -jnp.inf, acc)
        bits = pltpu.bitcast(acc, I32)
        key_ref[kb] = jnp.where(bits < 0, bits ^ jnp.int32(0x7FFFFFFF), bits)

    def score_quad(g, carry):
        for i in range(4):
            score_block(4 * g + i)
        return carry

    lax.fori_loop(0, nkb // 4, score_quad, 0)

    @pl.when((nkb & 2) != 0)
    def _():
        score_block(4 * (nkb // 4))
        score_block(4 * (nkb // 4) + 1)

    @pl.when((nkb & 1) != 0)
    def _():
        score_block(nkb - 1)

    @pl.when(j + 1 < 2 * n_pairs)
    def _():
        key_ref[j + 1] = jnp.full((QB, QB), INT_MIN, I32)

    def build_planes(g, carry):
        a = []
        for half in range(2):
            blk = key_ref[2 * g + half] ^ jnp.int32(INT_MIN)
            a.extend(blk[r * 8:(r + 1) * 8, :] for r in range(QB // 8))
        for sh, m in ((16, 0x0000FFFF), (8, 0x00FF00FF), (4, 0x0F0F0F0F), (2, 0x33333333),
                      (1, 0x55555555)):
            for k in range(32):
                if k & sh == 0:
                    t = (lax.shift_right_logical(a[k], jnp.int32(sh)) ^ a[k + sh]) & jnp.int32(m)
                    a[k + sh] = a[k + sh] ^ t
                    a[k] = a[k] ^ lax.shift_left(t, jnp.int32(sh))
        rows = pl.ds(pl.multiple_of(g * 8, 8), 8)
        for b in range(32):
            plane_ref[b, rows, :] = a[b]
        return carry

    def search(_):
        lax.fori_loop(0, n_pairs, build_planes, 0)
        wrow = lax.broadcasted_iota(I32, (nb // 2 * 8, QB), 0)
        cand = jnp.where((wrow >> 3) < n_pairs, jnp.int32(-1), jnp.int32(0))
        above = jnp.zeros((1, QB), I32)
        u = jnp.zeros((1, QB), I32)
        for b in range(31, -1, -1):
            ones = cand & plane_ref[b]
            cnt1 = jnp.sum(lax.population_count(ones), axis=0, keepdims=True)
            take = (above + cnt1) >= topk
            cand = jnp.where(take, ones, cand ^ ones)
            above = jnp.where(take, above, above + cnt1)
            u = u | jnp.where(take, jnp.int32(INT_MIN if b == 31 else 1 << b), jnp.int32(0))
        thr = u ^ jnp.int32(INT_MIN)
        need = topk - above
        n_tied = jnp.sum(lax.population_count(cand), axis=0, keepdims=True)

        def break_ties(_):
            base = (wrow >> 3) * (2 * QB) + (wrow & 7)

            def ibit(i, jp):
                c = jp + lax.shift_left(jnp.int32(1), idx_bits - 1 - i)
                nbits = jnp.clip((c - base + 7) >> 3, 0, 32)
                below = jnp.where(nbits >= 32, jnp.int32(-1),
                                  lax.shift_left(jnp.int32(1), nbits) - 1)
                cnt = jnp.sum(lax.population_count(cand & below), axis=0, keepdims=True)
                return jnp.where(cnt < need, c, jp)

            return lax.fori_loop(0, idx_bits, ibit, jnp.zeros((1, QB), I32))

        def keep_ties(_):
            return jnp.full((1, QB), S, I32)

        jp = lax.cond(jnp.max(n_tied - need) > 0, break_ties, keep_ties, 0)
        return thr, jp

    def take_all(_):
        return (jnp.full((1, QB), INT_MIN, I32), jnp.full((1, QB), -1, I32))

    thr, jp = lax.cond(nkb * QB > topk, search, take_all, 0)

    m_ref[...] = jnp.full(m_ref.shape, NEG_BIG, F32)
    l_ref[...] = jnp.zeros(l_ref.shape, F32)
    acc_ref[...] = jnp.zeros(acc_ref.shape, F32)

    def attend_block(kb):
        key = key_ref[kb]
        kidx = row + kb * QB
        sel = (key > thr) | ((key == thr) & (kidx <= jp))
        hidden = (kidx >= j * QB + CHUNK) & (lane < CHUNK)
        bias = jnp.where(sel & jnp.logical_not(hidden), 0.0, -jnp.inf)
        kblk = kk_ref[kb]
        vtb = vt_ref[kb]
        for h in range(SA_HEADS):
            lg = _dot(kblk, qt_ref[h * SA_DH:(h + 1) * SA_DH, :]) + bias
            m_old = m_ref[h:h + 1, :]
            m_new = jnp.maximum(m_old, jnp.max(lg, axis=0, keepdims=True))
            alpha = jnp.exp2(m_old - m_new)
            p = jnp.exp2(lg - m_new)
            l_ref[h:h + 1, :] = alpha * l_ref[h:h + 1, :] + jnp.sum(p, axis=0, keepdims=True)
            acc_ref[h] = acc_ref[h] * alpha + _dot(vtb, p.astype(BF16))
            m_ref[h:h + 1, :] = m_new

    def attend(g, carry):
        for i in range(8):
            attend_block(8 * g + i)
        return carry

    n_oct = nkb // 8
    lax.fori_loop(0, n_oct, attend, 0)

    @pl.when((nkb & 4) != 0)
    def _():
        for i in range(4):
            attend_block(8 * n_oct + i)

    n_quads = nkb // 4

    @pl.when((nkb & 2) != 0)
    def _():
        attend_block(4 * n_quads)
        attend_block(4 * n_quads + 1)

    @pl.when((nkb & 1) != 0)
    def _():
        attend_block(nkb - 1)

    outs = []
    for h in range(SA_HEADS):
        outs.append(acc_ref[h] * (1.0 / l_ref[h:h + 1, :]))
    o_ref[...] = jnp.concatenate(outs, axis=0).T


def _dsa(z, cnorm, wk, wvt, B, S):
    T = B * S
    nb = S // QBLOCK
    topk = min(TOPK_MAX, S // 4)
    assert nb % 2 == 0
    kern = functools.partial(_dsa_kernel, topk)
    return pl.pallas_call(
        kern,
        out_shape=jax.ShapeDtypeStruct((T, SA_WIDTH), F32),
        grid=(B, nb),
        in_specs=[
            pl.BlockSpec((QBLOCK, SA_WIDTH), lambda b, j: (b * nb + j, SQ_OFF // SA_WIDTH)),
            pl.BlockSpec((QBLOCK, IDX_HEADS * IDX_DIM),
                         lambda b, j: (b * nb + j, IQ_OFF // (IDX_HEADS * IDX_DIM))),
            pl.BlockSpec((QBLOCK, LANES), lambda b, j: (b * nb + j, IKW_OFF // LANES)),
            pl.BlockSpec((S, SA_LATENT), lambda b, j: (b, SC_OFF // SA_LATENT)),
            pl.BlockSpec((S, LANES), lambda b, j: (b, IKW_OFF // LANES)),
            pl.BlockSpec((1, SA_LATENT), lambda b, j: (0, 0)),
            pl.BlockSpec((SA_LATENT, SA_DH), lambda b, j: (0, 0)),
            pl.BlockSpec((SA_DH, SA_LATENT), lambda b, j: (0, 0)),
        ],
        out_specs=pl.BlockSpec((QBLOCK, SA_WIDTH), lambda b, j: (b * nb + j, 0)),
        scratch_shapes=[
            pltpu.VMEM((nb, QBLOCK, SA_DH), BF16),
            pltpu.VMEM((nb, SA_DH, QBLOCK), BF16),
            pltpu.VMEM((nb, QBLOCK, IDX_DIM), BF16),
            pltpu.VMEM((nb, QBLOCK, QBLOCK), I32),
            pltpu.VMEM((32, nb // 2 * 8, QBLOCK), I32),
            pltpu.VMEM((SA_WIDTH, QBLOCK), BF16),
            pltpu.VMEM((SA_HEADS, QBLOCK), F32),
            pltpu.VMEM((SA_HEADS, QBLOCK), F32),
            pltpu.VMEM((SA_HEADS, SA_DH, QBLOCK), F32),
        ],
        compiler_params=_cparams(("parallel", "arbitrary")),
        name="dsa",
    )(z, z, z, z, z, cnorm, wk, wvt)


def _sb_kernel(q_ref, k_ref, v_ref, uo_ref, o_ref, kt_ref, vb_ref, run_ref, acc_ref,
               ls_ref, lkh_ref, lkl_ref, tot_ref, a_ref):
    QB = QBLOCK
    j = pl.program_id(1)
    S = k_ref.shape[0]
    nb = S // QB
    npair = SB_HEADS // 2
    lane = lax.broadcasted_iota(I32, (QB, QB), 1)
    row = lax.broadcasted_iota(I32, (QB, QB), 0)

    @pl.when(j == 0)
    def _():
        for p in range(npair):
            cols = slice(p * LANES, (p + 1) * LANES)
            kt = k_ref[:, cols].T
            vv = v_ref[:, cols]
            for hh in range(2):
                lo = hh * SB_DH
                ktz = jnp.where((row >= lo) & (row < lo + SB_DH), 1.0, 0.0)
                vz = jnp.where((lane >= lo) & (lane < lo + SB_DH), 1.0, 0.0)
                for kb in range(nb):
                    ks = slice(kb * QB, (kb + 1) * QB)
                    kt_ref[kb, 2 * p + hh] = (kt[:, ks] * ktz).astype(BF16)
                    vb_ref[kb, 2 * p + hh] = (vv[ks, :] * vz).astype(BF16)

    uo = uo_ref[...]
    strict = lane < row
    qs = (q_ref[...] * (SB_DH ** -0.5)).astype(BF16)

    def all_heads(kbs, diag):
        nk = len(kbs)
        for h in range(SB_HEADS):
            p = h // 2
            for i, kb in enumerate(kbs):
                s = h * nk + i
                z = _dot(qs[:, p * LANES:(p + 1) * LANES], kt_ref[kb, h])
                l1p = jnp.log(1.0 + jnp.exp(-jnp.abs(z)))
                ls_pos = jnp.minimum(z, 0.0) - l1p
                lk = ls_pos - z
                if diag and i == 0:
                    lk = jnp.where(strict, lk, 0.0)
                lk_hi, lk_lo = _split_bf16(lk)
                ls_ref[s] = ls_pos
                lkh_ref[s] = lk_hi
                lkl_ref[s] = lk_lo
                tot_ref[s] = jnp.sum(lk, axis=1, keepdims=True)
        top = None
        for h in range(SB_HEADS):
            run = None if diag else run_ref[h]
            for i in range(nk):
                s = h * nk + i
                after = _dot(lkh_ref[s], uo) + _dot(lkl_ref[s], uo)
                if diag and i == 0:
                    a = jnp.where(strict, jnp.exp(ls_ref[s] + after), 0.0)
                    run = tot_ref[s]
                else:
                    a = jnp.exp(ls_ref[s] + after + run)
                    run = run + tot_ref[s]
                a_ref[s] = a.astype(BF16)
            run_ref[h] = run
            top = run if top is None else jnp.maximum(top, run)
        for p in range(npair):
            pv = None
            for h in (2 * p, 2 * p + 1):
                for i, kb in enumerate(kbs):
                    out = _dot(a_ref[h * nk + i], vb_ref[kb, h])
                    pv = out if pv is None else pv + out
            if diag:
                acc_ref[p] = pv
            else:
                acc_ref[p] += pv
        return jnp.max(top) > SB_UNDERFLOW

    n_first = 3
    alive = lax.cond(
        j >= 2, lambda: all_heads([j, j - 1, j - 2], True),
        lambda: lax.cond(j == 1, lambda: all_heads([j, j - 1], True), lambda: all_heads([j], True)))

    def cond(c):
        return (c[0] >= 1) & c[1]

    def body(c):
        return c[0] - 2, all_heads([c[0], c[0] - 1], False)

    kb_left, alive = lax.while_loop(cond, body, (j - n_first, alive))

    @pl.when((kb_left == 0) & alive)
    def _():
        all_heads([0], False)

    o_ref[...] = jnp.concatenate([acc_ref[p] for p in range(npair)], axis=1)


def _sb(z, B, S):
    T = B * S
    nb = S // QBLOCK
    u = np.tril(np.ones((QBLOCK, QBLOCK), np.float32), -1)
    uo = jnp.asarray(u, BF16)
    qoff = SB_OFF // SB_WIDTH
    return pl.pallas_call(
        _sb_kernel,
        out_shape=jax.ShapeDtypeStruct((T, SB_WIDTH), F32),
        grid=(B, nb),
        in_specs=[
            pl.BlockSpec((QBLOCK, SB_WIDTH), lambda b, j: (b * nb + j, qoff)),
            pl.BlockSpec((S, SB_WIDTH), lambda b, j: (b, qoff + 1)),
            pl.BlockSpec((S, SB_WIDTH), lambda b, j: (b, qoff + 2)),
            pl.BlockSpec(uo.shape, lambda b, j: (0, 0)),
        ],
        out_specs=pl.BlockSpec((QBLOCK, SB_WIDTH), lambda b, j: (b * nb + j, 0)),
        scratch_shapes=[
            pltpu.VMEM((nb, SB_HEADS, LANES, QBLOCK), BF16),
            pltpu.VMEM((nb, SB_HEADS, QBLOCK, LANES), BF16),
            pltpu.VMEM((SB_HEADS, QBLOCK, 1), F32),
            pltpu.VMEM((SB_HEADS // 2, QBLOCK, LANES), F32),
            pltpu.VMEM((3 * SB_HEADS, QBLOCK, QBLOCK), F32),
            pltpu.VMEM((3 * SB_HEADS, QBLOCK, QBLOCK), BF16),
            pltpu.VMEM((3 * SB_HEADS, QBLOCK, QBLOCK), BF16),
            pltpu.VMEM((3 * SB_HEADS, QBLOCK, 1), F32),
            pltpu.VMEM((3 * SB_HEADS, QBLOCK, QBLOCK), BF16),
        ],
        compiler_params=_cparams(("parallel", "arbitrary")),
        name="stickbreak",
    )(z, z, z, uo)


def _merge_kernel(a_ref, b_ref, c_ref, u_ref, h_ref, wgz_ref, wb_ref, wo_ref, gn_ref, ho_ref, v_ref):
    u = u_ref[...]
    merged = None
    for n, br in enumerate((a_ref, b_ref, c_ref)):
        proj = _dot(br[...].astype(BF16), wb_ref[n])
        gate = _sigmoid(_dot(u, wgz_ref[:, n * D_MODEL:(n + 1) * D_MODEL]))
        merged = gate * proj if merged is None else merged + gate * proj
    hn = h_ref[...] + _dot(merged.astype(BF16), wo_ref[...])
    ho_ref[...] = hn
    ms = jnp.mean(hn * hn, axis=-1, keepdims=True)
    v_ref[...] = (hn * lax.rsqrt(ms + EPS) * gn_ref[...]).astype(BF16)


def _merge(a, b, c, u, h, wgz, wb, wo, gn, tm=256):
    T, D = h.shape
    bw = BRANCH_WIDTH
    return pl.pallas_call(
        _merge_kernel,
        out_shape=(jax.ShapeDtypeStruct((T, D), F32), jax.ShapeDtypeStruct((T, D), BF16)),
        grid=(T // tm,),
        in_specs=[
            pl.BlockSpec((tm, bw), lambda i: (i, 0)),
            pl.BlockSpec((tm, bw), lambda i: (i, 0)),
            pl.BlockSpec((tm, bw), lambda i: (i, 0)),
            pl.BlockSpec((tm, D), lambda i: (i, 0)),
            pl.BlockSpec((tm, D), lambda i: (i, 0)),
            pl.BlockSpec((D, GZW), lambda i: (0, 0)),
            pl.BlockSpec((N_BRANCH, bw, D), lambda i: (0, 0, 0)),
            pl.BlockSpec((D, D), lambda i: (0, 0)),
            pl.BlockSpec((1, D), lambda i: (0, 0)),
        ],
        out_specs=(pl.BlockSpec((tm, D), lambda i: (i, 0)), pl.BlockSpec((tm, D), lambda i: (i, 0))),
        compiler_params=_cparams(("parallel",)),
        name="merge",
    )(a, b, c, u, h, wgz, wb, wo, gn)


def _swiglu_kernel(v_ref, h_ref, wg_ref, wu_ref, wd_ref, o_ref, acc_ref):
    f = pl.program_id(1)

    @pl.when(f == 0)
    def _():
        acc_ref[...] = h_ref[...]

    v = v_ref[...]
    gt = _dot(v, wg_ref[...])
    up = _dot(v, wu_ref[...])
    act = (gt * _sigmoid(gt) * up).astype(BF16)
    acc_ref[...] += _dot(act, wd_ref[...])

    @pl.when(f == pl.num_programs(1) - 1)
    def _():
        o_ref[...] = acc_ref[...]


def _swiglu(v, h, wg, wu, wd, tm=512, tf=1408):
    T, D = h.shape
    F = wg.shape[1]
    return pl.pallas_call(
        _swiglu_kernel,
        out_shape=jax.ShapeDtypeStruct((T, D), F32),
        grid=(T // tm, F // tf),
        in_specs=[
            pl.BlockSpec((tm, D), lambda i, f: (i, 0)),
            pl.BlockSpec((tm, D), lambda i, f: (i, 0)),
            pl.BlockSpec((D, tf), lambda i, f: (0, f)),
            pl.BlockSpec((D, tf), lambda i, f: (0, f)),
            pl.BlockSpec((tf, D), lambda i, f: (f, 0)),
        ],
        out_specs=pl.BlockSpec((tm, D), lambda i, f: (i, 0)),
        scratch_shapes=[pltpu.VMEM((tm, D), F32)],
        compiler_params=_cparams(("parallel", "arbitrary")),
        name="swiglu",
    )(v, h, wg, wu, wd)


MOE_TILE = 1024
MOE_CHUNK = 288


def _router_kernel(h_ref, gn_ref, r_ref, us_ref, cw_ref, cwt_ref, rkt_ref):
    hh = h_ref[...]
    ms = jnp.mean(hh * hh, axis=-1, keepdims=True)
    vf = hh * lax.rsqrt(ms + EPS) * gn_ref[...]
    v_hi, v_lo = _split_bf16(vf)
    r_hi, r_lo = _split_bf16(r_ref[...])
    logits = _dot(v_hi, r_hi) + _dot(v_hi, r_lo) + _dot(v_lo, r_hi)
    lane = lax.broadcasted_iota(I32, logits.shape, 1)
    logits = jnp.where(lane < N_EXPERTS, logits, -jnp.inf)
    m1 = jnp.max(logits, axis=-1, keepdims=True)
    i1 = jnp.min(jnp.where(logits == m1, lane, LANES), axis=-1, keepdims=True)
    rest = jnp.where(lane == i1, -jnp.inf, logits)
    m2 = jnp.max(rest, axis=-1, keepdims=True)
    i2 = jnp.min(jnp.where(rest == m2, lane, LANES), axis=-1, keepdims=True)
    e2 = jnp.exp(m2 - m1)
    w1 = 1.0 / (1.0 + e2)
    cw = jnp.where(lane == i1, w1, 0.0) + jnp.where(lane == i2, e2 * w1, 0.0)
    cw_ref[...] = cw
    cwt = cw.T
    cwt_ref[...] = cwt[0:N_EXPERTS]
    sel = jnp.where(cwt != 0.0, 1.0, 0.0).astype(BF16)
    rkt_ref[...] = _dot(sel, us_ref[...])[0:N_EXPERTS]


def _router(h, gn, router, tm=MOE_TILE):
    T, D = h.shape
    us = jnp.asarray(np.triu(np.ones((tm, tm), np.float32), 1), BF16)
    return pl.pallas_call(
        _router_kernel,
        out_shape=(jax.ShapeDtypeStruct((T, LANES), F32),
                   jax.ShapeDtypeStruct((N_EXPERTS, T), F32),
                   jax.ShapeDtypeStruct((N_EXPERTS, T), F32)),
        grid=(T // tm,),
        in_specs=[
            pl.BlockSpec((tm, D), lambda i: (i, 0)),
            pl.BlockSpec((1, D), lambda i: (0, 0)),
            pl.BlockSpec((D, LANES), lambda i: (0, 0)),
            pl.BlockSpec((tm, tm), lambda i: (0, 0)),
        ],
        out_specs=(pl.BlockSpec((tm, LANES), lambda i: (i, 0)),
                   pl.BlockSpec((N_EXPERTS, tm), lambda i: (0, i)),
                   pl.BlockSpec((N_EXPERTS, tm), lambda i: (0, i))),
        compiler_params=_cparams(("parallel",)),
        name="router",
    )(h, gn, router, us)


MOE_SUB = 256
MOE_WIN = 112


MOE_RING = 3


def _moe_kernel(v_ref, h_ref, cw_ref, cwt_ref, rkt_ref, wg_hbm, wu_hbm, wd_hbm, o_ref,
                xg_ref, ce_ref, ya_ref, wg_buf, wu_buf, wd_buf, wsem):
    e = pl.program_id(1)
    f = pl.program_id(2)
    ne = pl.num_programs(1)
    nf = pl.num_programs(2)
    tm = v_ref.shape[0]
    CH, SUB, WIN = MOE_CHUNK, MOE_SUB, MOE_WIN
    nsub = tm // SUB
    tf = wg_buf.shape[2]

    step = (pl.program_id(0) * ne + e) * nf + f
    n_steps = pl.num_programs(0) * ne * nf

    def weight_copies(s):
        slot = s % MOE_RING
        es = (s // nf) % ne
        cols = pl.ds(pl.multiple_of((s % nf) * tf, LANES), tf)
        return (
            pltpu.make_async_copy(wg_hbm.at[es, :, cols], wg_buf.at[slot], wsem.at[slot, 0]),
            pltpu.make_async_copy(wu_hbm.at[es, :, cols], wu_buf.at[slot], wsem.at[slot, 1]),
            pltpu.make_async_copy(wd_hbm.at[es, cols, :], wd_buf.at[slot], wsem.at[slot, 2]),
        )

    @pl.when(step == 0)
    def _():
        for s0 in range(MOE_RING - 1):
            for cp in weight_copies(jnp.int32(s0)):
                cp.start()

    @pl.when(step + (MOE_RING - 1) < n_steps)
    def _():
        for cp in weight_copies(step + (MOE_RING - 1)):
            cp.start()

    for cp in weight_copies(step):
        cp.wait()
    wslot = step % MOE_RING

    @pl.when((e == 0) & (f == 0))
    def _():
        o_ref[...] = h_ref[...]

    ce_row = cwt_ref[pl.ds(e, 1), :]
    routed = ce_row != 0.0
    rank_row = rkt_ref[pl.ds(e, 1), :]
    slot_row = jnp.where(routed, rank_row, -1.0)
    count = jnp.sum(jnp.where(routed, 1, 0).astype(I32))
    n_chunks = (count + (CH - 1)) // CH
    bounds = [jnp.int32(0)]
    for s in range(1, nsub):
        bounds.append(rank_row[0, s * SUB].astype(I32))
    bounds.append(count)
    wiota = lax.broadcasted_iota(I32, (WIN, SUB), 0).astype(F32)

    def for_windows(fn):
        def window(s, r0):
            slots = slot_row[:, s * SUB:(s + 1) * SUB]
            pick = jnp.where(slots == wiota + r0.astype(F32), 1.0, 0.0).astype(BF16)
            fn(s, r0, pick)

        starts = [pl.multiple_of((bounds[s] // 16) * 16, 16) for s in range(nsub)]
        for s in range(nsub):
            window(s, starts[s])
        for s in range(nsub):
            nw = (bounds[s + 1] - starts[s] + (WIN - 1)) // WIN

            def body(m, carry, s=s):
                window(s, pl.multiple_of(starts[s] + m * WIN, 16))
                return carry

            lax.fori_loop(1, nw, body, 0)

    @pl.when(f == 0)
    def _():
        def clear(i, carry):
            rows = pl.ds(pl.multiple_of(i * CH, 16), CH)
            xg_ref[rows, :] = jnp.zeros((CH, xg_ref.shape[1]), BF16)
            ce_ref[rows, :] = jnp.zeros((CH, LANES), F32)
            ya_ref[rows, :] = jnp.zeros((CH, ya_ref.shape[1]), F32)
            return carry

        lax.fori_loop(0, jnp.minimum(n_chunks + 1, xg_ref.shape[0] // CH), clear, 0)
        lane = lax.broadcasted_iota(I32, (WIN, LANES), 1)

        def gather(s, r0, pick):
            toks = slice(s * SUB, (s + 1) * SUB)
            rows = pl.ds(r0, WIN)
            got = _dot(pick, v_ref[toks, :])
            xg_ref[rows, :] = (xg_ref[rows, :].astype(F32) + got).astype(BF16)
            c_hi, c_lo = _split_bf16(cw_ref[toks, :])
            cg = _dot(pick, c_hi) + _dot(pick, c_lo)
            ce = jnp.sum(jnp.where(lane == e, cg, 0.0), axis=-1, keepdims=True)
            ce_ref[rows, :] += jnp.broadcast_to(ce, (WIN, LANES))

        for_windows(gather)

    def expert(i, carry):
        rows = pl.ds(pl.multiple_of(i * CH, 16), CH)
        xg = xg_ref[rows, :]
        gt = _dot(xg, wg_buf[wslot])
        up = _dot(xg, wu_buf[wslot])
        act = (gt * _sigmoid(gt) * up * ce_ref[rows, 0:1]).astype(BF16)
        ya_ref[rows, :] += _dot(act, wd_buf[wslot])
        return carry

    lax.fori_loop(0, n_chunks, expert, 0)

    @pl.when(f == nf - 1)
    def _():
        def scatter(s, r0, pick):
            ye = ya_ref[pl.ds(r0, WIN), :].astype(BF16)
            o_ref[s * SUB:(s + 1) * SUB, :] += _dot_tn(pick, ye)

        for_windows(scatter)


def _moe(v, h, cw, cwt, rkt, wg, wu, wd, tm=MOE_TILE, fsplit=2):
    T, D = h.shape
    E, _, F = wg.shape
    tf = F // fsplit
    rows = (-(-tm // MOE_CHUNK)) * MOE_CHUNK
    assert rows >= tm + MOE_WIN and tm % MOE_SUB == 0
    return pl.pallas_call(
        _moe_kernel,
        out_shape=jax.ShapeDtypeStruct((T, D), F32),
        grid=(T // tm, E, fsplit),
        in_specs=[
            pl.BlockSpec((tm, D), lambda i, e, f: (i, 0), pipeline_mode=pl.Buffered(1)),
            pl.BlockSpec((tm, D), lambda i, e, f: (i, 0), pipeline_mode=pl.Buffered(1)),
            pl.BlockSpec((tm, LANES), lambda i, e, f: (i, 0), pipeline_mode=pl.Buffered(1)),
            pl.BlockSpec((N_EXPERTS, tm), lambda i, e, f: (0, i)),
            pl.BlockSpec((N_EXPERTS, tm), lambda i, e, f: (0, i)),
            pl.BlockSpec(memory_space=pl.ANY),
            pl.BlockSpec(memory_space=pl.ANY),
            pl.BlockSpec(memory_space=pl.ANY),
        ],
        out_specs=pl.BlockSpec((tm, D), lambda i, e, f: (i, 0)),
        scratch_shapes=[
            pltpu.VMEM((rows, D), BF16),
            pltpu.VMEM((rows, LANES), F32),
            pltpu.VMEM((rows, D), F32),
            pltpu.VMEM((MOE_RING, D, tf), BF16),
            pltpu.VMEM((MOE_RING, D, tf), BF16),
            pltpu.VMEM((MOE_RING, tf, D), BF16),
            pltpu.SemaphoreType.DMA((MOE_RING, 3)),
        ],
        compiler_params=_cparams(("arbitrary", "arbitrary", "arbitrary")),
        name="moe",
    )(v, h, cw, cwt, rkt, wg, wu, wd)


def _ple_kernel(final, h_ref, p_ref, gn_ref, wg_ref, wp_ref, fn_ref, o_ref):
    h = h_ref[...]
    ms = jnp.mean(h * h, axis=-1, keepdims=True)
    u = (h * lax.rsqrt(ms + EPS) * gn_ref[...]).astype(BF16)
    gate = _sigmoid(_dot(u, wg_ref[...]))
    hn = h + gate * _dot(p_ref[0].astype(BF16), wp_ref[...])
    if final:
        ms = jnp.mean(hn * hn, axis=-1, keepdims=True)
        hn = hn * lax.rsqrt(ms + EPS) * fn_ref[...]
    o_ref[...] = hn


def _ple(h, p, layer, gn, wg, wp, fn, final, tm=512):
    T, D = h.shape
    return pl.pallas_call(
        functools.partial(_ple_kernel, final),
        out_shape=jax.ShapeDtypeStruct((T, D), F32),
        grid=(T // tm,),
        in_specs=[
            pl.BlockSpec((tm, D), lambda i: (i, 0)),
            pl.BlockSpec((1, tm, PLE_DIM), lambda i: (layer, i, 0)),
            pl.BlockSpec((1, D), lambda i: (0, 0)),
            pl.BlockSpec((D, D), lambda i: (0, 0)),
            pl.BlockSpec((PLE_DIM, D), lambda i: (0, 0)),
            pl.BlockSpec((1, D), lambda i: (0, 0)),
        ],
        out_specs=pl.BlockSpec((tm, D), lambda i: (i, 0)),
        compiler_params=_cparams(("parallel",)),
        name="ple",
    )(h, p, gn, wg, wp, fn)


def _reorder_kernel(w_ref, o_ref, ogz_ref):
    o_sq = 4 * HG_WIDTH
    o_sc = o_sq + SA_WIDTH
    o_iq = o_sc + SA_LATENT
    o_ik = o_iq + IDX_HEADS * IDX_DIM
    o_bq = o_ik + IDX_DIM + IDX_HEADS
    o_gz = o_bq + 3 * SB_WIDTH

    def put(dst, lo, hi):
        o_ref[:, dst:dst + (hi - lo)] = w_ref[0, :, lo:hi].astype(BF16)

    ogz_ref[...] = w_ref[0, :, o_gz:o_gz + GZW].astype(BF16)
    put(HG_OFF, 0, o_sq)
    put(SQ_OFF, o_sq, o_sc)
    put(IQ_OFF, o_iq, o_ik)
    put(SC_OFF, o_sc, o_iq)
    put(IKW_OFF, o_ik, o_bq)
    used = IDX_DIM + IDX_HEADS
    o_ref[:, IKW_OFF + used:SB_OFF] = jnp.zeros((o_ref.shape[0], LANES - used), BF16)
    put(SB_OFF, o_bq, o_gz)


def _reorder_w_in(w_in, layer, tr=256):
    _, R, C = w_in.shape
    return pl.pallas_call(
        _reorder_kernel,
        out_shape=(jax.ShapeDtypeStruct((R, ZW), BF16), jax.ShapeDtypeStruct((R, GZW), BF16)),
        grid=(R // tr,),
        in_specs=[pl.BlockSpec((1, tr, C), lambda i: (layer, i, 0))],
        out_specs=(pl.BlockSpec((tr, ZW), lambda i: (i, 0)), pl.BlockSpec((tr, GZW), lambda i: (i, 0))),
        compiler_params=_cparams(("parallel",)),
        name="reorder_w_in",
    )(w_in)


def kernel(x, p, lb_param, norm_mix, w_in, hg_onorm, sa_cnorm, sa_wk, sa_wv, w_branch, w_out,
           norm_ffn, dense_wg, dense_wu, dense_wd, moe_router, moe_wg, moe_wu, moe_wd, norm_ple,
           ple_gate, ple_proj, norm_final):
    B, S, D = x.shape
    T = B * S
    depth = w_in.shape[0]
    lb_sm = jax.nn.softmax(lb_param.astype(F32), axis=0)
    lb_all = jnp.cumsum(lb_sm, axis=0) - lb_sm[0:1]
    h = x.reshape(T, D)
    for i in range(depth):
        w_mix, w_gz = _reorder_w_in(w_in, i)
        z, u = _inproj(h, norm_mix[i][None, :], w_mix)
        a = _hgrn(z, lb_all[i][None, :], hg_onorm[i][None, :], B, S)
        b = _dsa(z, sa_cnorm[i][None, :], sa_wk[i].astype(BF16), sa_wv[i].T.astype(BF16), B, S)
        c = _sb(z, B, S)
        h, v = _merge(a, b, c, u, h, w_gz, w_branch[i].astype(BF16), w_out[i].astype(BF16),
                      norm_ffn[i][None, :])
        jj = i // 2
        if i % 2 == 0:
            h = _swiglu(v, h, dense_wg[jj].astype(BF16), dense_wu[jj].astype(BF16),
                        dense_wd[jj].astype(BF16))
        else:
            router = jnp.pad(moe_router[jj], ((0, 0), (0, LANES - N_EXPERTS)))
            cw, cwt, rkt = _router(h, norm_ffn[i][None, :], router)
            h = _moe(v, h, cw, cwt, rkt, moe_wg[jj].astype(BF16), moe_wu[jj].astype(BF16),
                     moe_wd[jj].astype(BF16))
        h = _ple(h, p.reshape(depth, T, PLE_DIM), i, norm_ple[i][None, :],
                 ple_gate[i].astype(BF16), ple_proj[i].astype(BF16), norm_final[None, :],
                 i == depth - 1)
    return h.reshape(B, S, D)
```
